```python
import jax, jax.numpy as jnp
from jax import lax
import numpy as np

D_MODEL = 1024
BATCH = 32
SEQ = 256
DEPTH = 2
DEC_BATCH = 4
DEC_SEQ = 1024
PAST_LEN = 512

GRID_W = 64
N_EVEN = (DEPTH + 1) // 2
N_ODD = DEPTH // 2
N_MOD = 9
D_FF = ((8 * D_MODEL // 3 + 127) // 128) * 128
D_CONV = D_MODEL // 2
CONV_WIDTH = 31
D_POOL = D_MODEL // 2
POOL_WINDOWS = (2, 4, 8, 16)
POOL_GROUP = D_POOL // len(POOL_WINDOWS)
N_HEADS_C = 8
QK_NOPE = 128
QK_ROPE = 64
V_DIM = 128
KV_LORA = D_MODEL // 4
Q_LORA = 3 * D_MODEL // 8
ROPE_AXIS_PAIRS = QK_ROPE // 4
ROPE_BASE = 10000.0
Q_BLOCK = 128
ALPHA = (2 * DEPTH) ** 0.25
BETA = (8 * DEPTH) ** -0.25
LN_EPS = 1e-5
RMS_EPS = 1e-6

kernel_name = 'hybrid_conv_pool_mla_prefix_dit'


def layer_norm(x, g, b):
    xf = x.astype(jnp.float32)
    mu = jnp.mean(xf, -1, keepdims=True)
    var = jnp.mean(jnp.square(xf - mu), -1, keepdims=True)
    return ((xf - mu) * lax.rsqrt(var + LN_EPS)).astype(x.dtype) * g + b


def rms_norm(x, g):
    xf = x.astype(jnp.float32)
    return (xf * lax.rsqrt(jnp.mean(xf * xf, -1, keepdims=True) + RMS_EPS)).astype(x.dtype) * g


def modulate(x, shift, scale):
    return x * (1 + scale) + shift


def swiglu(x, w1, w3, w2):
    return (jax.nn.silu(x @ w1) * (x @ w3)) @ w2


def ffn_sub(x, shift, scale, gate, g, b, w1, w3, w2):
    h = modulate(x, shift, scale)
    return layer_norm(ALPHA * x + 0.5 * gate * swiglu(h, w1, w3, w2), g, b)


def conv_module(u, gate, conv_w, conv_b, norm_g, norm_b):
    h = u * jax.nn.sigmoid(gate)
    pad = CONV_WIDTH // 2
    h = lax.conv_general_dilated(h, conv_w[:, None, :], window_strides=(1,), padding=((pad, pad),),
                                 dimension_numbers=('NWC', 'WIO', 'NWC'),
                                 feature_group_count=h.shape[-1]) + conv_b
    return jax.nn.silu(layer_norm(h, norm_g, norm_b))


def pool_module(h, pool_w, pool_scale):
    n = h.shape[1]
    t = jnp.arange(n)
    cs = jnp.pad(jnp.cumsum(h.astype(jnp.float32), axis=1), ((0, 0), (1, 0), (0, 0)))
    outs = []
    for gi, w in enumerate(POOL_WINDOWS):
        left = w // 2
        right = w - 1 - left
        lo = jnp.clip(t - left, 0, n - 1)
        hi = jnp.clip(t + right, 0, n - 1)
        sl = slice(gi * POOL_GROUP, (gi + 1) * POOL_GROUP)
        cg = cs[..., sl]
        mean = (jnp.take(cg, hi + 1, axis=1) - jnp.take(cg, lo, axis=1)) / (hi - lo + 1).astype(jnp.float32)[:, None]
        outs.append((mean.astype(h.dtype) - h[..., sl]) @ pool_w[gi])
    return jnp.concatenate(outs, -1) * pool_scale


def conv_pool_mixer(x, w_in, conv_w, conv_b, norm_g, norm_b, pool_w, pool_scale, w_out):
    proj = x @ w_in
    u = proj[..., :D_CONV]
    gate = proj[..., D_CONV:2 * D_CONV]
    hp = proj[..., 2 * D_CONV:]
    a = conv_module(u, gate, conv_w, conv_b, norm_g, norm_b)
    b = pool_module(hp, pool_w, pool_scale)
    return jnp.concatenate([a, b], -1) @ w_out


def axial_angles(n):
    rows = n // GRID_W
    row = jnp.repeat(jnp.arange(rows), GRID_W)
    col = jnp.tile(jnp.arange(GRID_W), rows)
    inv = ROPE_BASE ** (-jnp.arange(ROPE_AXIS_PAIRS, dtype=jnp.float32) / ROPE_AXIS_PAIRS)
    ang = jnp.concatenate([row[:, None] * inv, col[:, None] * inv], -1)
    return jnp.cos(ang), jnp.sin(ang)


def apply_rope_2d(x, cos, sin):
    n = cos.shape[0]
    xs = x.reshape(x.shape[:-1] + (2, 2, ROPE_AXIS_PAIRS))
    x1, x2 = xs[..., 0, :], xs[..., 1, :]
    bshape = (n,) + (1,) * (x.ndim - 3) + (2, ROPE_AXIS_PAIRS)
    c = cos.reshape(bshape)
    s = sin.reshape(bshape)
    out = jnp.stack([x1 * c - x2 * s, x1 * s + x2 * c], axis=-2)
    return out.reshape(x.shape).astype(x.dtype)


def mla_q(x, w_dq, q_norm_g, w_uq):
    q = rms_norm(x @ w_dq, q_norm_g) @ w_uq
    q = q.reshape(x.shape[:2] + (N_HEADS_C, QK_NOPE + QK_ROPE))
    return q[..., :QK_NOPE], q[..., QK_NOPE:]


def mla_compress_kv(x, w_dkv, kv_norm_g):
    kv = x @ w_dkv
    return rms_norm(kv[..., :KV_LORA], kv_norm_g), kv[..., KV_LORA:]


def mla_expand(c_kv, w_ukv):
    kv = (c_kv @ w_ukv).reshape(c_kv.shape[:2] + (N_HEADS_C, QK_NOPE + V_DIM))
    return kv[..., :QK_NOPE], kv[..., QK_NOPE:]


def mla_attend(q_nope, q_rope, k_nope, k_rope, v):
    b, n = q_nope.shape[:2]
    qb = min(Q_BLOCK, n)
    nb = n // qb
    scale = (QK_NOPE + QK_ROPE) ** -0.5

    def block(qs):
        qn, qr = qs
        s = (jnp.einsum('bqhd,bkhd->bhqk', qn, k_nope, preferred_element_type=jnp.float32)
             + jnp.einsum('bqhr,bkr->bhqk', qr, k_rope, preferred_element_type=jnp.float32))
        p = jax.nn.softmax(s * scale, axis=-1).astype(v.dtype)
        return jnp.einsum('bhqk,bkhd->bqhd', p, v)

    def to_blocks(a):
        return jnp.moveaxis(a.reshape((b, nb, qb) + a.shape[2:]), 1, 0)

    out = lax.map(block, (to_blocks(q_nope), to_blocks(q_rope)))
    return jnp.moveaxis(out, 0, 1).reshape(b, n, N_HEADS_C * V_DIM)


def setup_inputs(seed: int = 0) -> dict:
    key = jax.random.key(seed)
    ks = jax.random.split(key, 32)

    def nrm(k, shape, scale):
        return jax.random.normal(k, shape, jnp.float32) * scale

    d = D_MODEL
    return {
        'x_prompt': nrm(ks[0], (BATCH, SEQ, d), 1.0),
        'x_sample': nrm(ks[1], (DEC_BATCH, DEC_SEQ, d), 1.0),
        'cache_mla_ckv': nrm(ks[2], (DEC_BATCH, N_ODD, PAST_LEN, KV_LORA), 1.0),
        'cache_mla_krope': nrm(ks[3], (DEC_BATCH, N_ODD, PAST_LEN, QK_ROPE), 1.0),
        'c': nrm(ks[4], (DEC_BATCH, d), 1.0),
        'c_ctx': nrm(ks[5], (d,), 1.0),
        'w_ada': nrm(ks[6], (DEPTH, d, N_MOD * d), d ** -0.5),
        'b_ada': nrm(ks[7], (DEPTH, N_MOD * d), 0.01),
        'ln_g': 1.0 + nrm(ks[8], (DEPTH, 3, d), 0.05),
        'ln_b': nrm(ks[9], (DEPTH, 3, d), 0.02),
        'ffn_w1': nrm(ks[10], (DEPTH, 2, d, D_FF), d ** -0.5),
        'ffn_w3': nrm(ks[11], (DEPTH, 2, d, D_FF), d ** -0.5),
        'ffn_w2': nrm(ks[12], (DEPTH, 2, D_FF, d), BETA * D_FF ** -0.5),
        'cp_w_in': nrm(ks[13], (N_EVEN, d, 2 * D_CONV + D_POOL), d ** -0.5),
        'conv_w': nrm(ks[14], (N_EVEN, CONV_WIDTH, D_CONV), CONV_WIDTH ** -0.5),
        'conv_b': nrm(ks[15], (N_EVEN, D_CONV), 0.02),
        'conv_norm_g': 1.0 + nrm(ks[16], (N_EVEN, D_CONV), 0.05),
        'conv_norm_b': nrm(ks[17], (N_EVEN, D_CONV), 0.02),
        'pool_w': nrm(ks[18], (N_EVEN, len(POOL_WINDOWS), POOL_GROUP, POOL_GROUP), POOL_GROUP ** -0.5),
        'pool_scale': 1.0 + nrm(ks[19], (N_EVEN, D_POOL), 0.1),
        'cp_w_out': nrm(ks[20], (N_EVEN, D_CONV + D_POOL, d), BETA * (D_CONV + D_POOL) ** -0.5),
        'mla_w_dq': nrm(ks[21], (N_ODD, d, Q_LORA), d ** -0.5),
        'mla_q_norm_g': 1.0 + nrm(ks[22], (N_ODD, Q_LORA), 0.05),
        'mla_w_uq': nrm(ks[23], (N_ODD, Q_LORA, N_HEADS_C * (QK_NOPE + QK_ROPE)), Q_LORA ** -0.5),
        'mla_w_dkv': nrm(ks[24], (N_ODD, d, KV_LORA + QK_ROPE), d ** -0.5),
        'mla_kv_norm_g': 1.0 + nrm(ks[25], (N_ODD, KV_LORA), 0.05),
        'mla_w_ukv': nrm(ks[26], (N_ODD, KV_LORA, N_HEADS_C * (QK_NOPE + V_DIM)), KV_LORA ** -0.5),
        'mla_w_o': nrm(ks[27], (N_ODD, N_HEADS_C * V_DIM, d), BETA * (N_HEADS_C * V_DIM) ** -0.5),
    }


def reference(x_prompt, x_sample, cache_mla_ckv, cache_mla_krope, c, c_ctx,
              w_ada, b_ada, ln_g, ln_b, ffn_w1, ffn_w3, ffn_w2,
              cp_w_in, conv_w, conv_b, conv_norm_g, conv_norm_b, pool_w, pool_scale, cp_w_out,
              mla_w_dq, mla_q_norm_g, mla_w_uq, mla_w_dkv, mla_kv_norm_g, mla_w_ukv, mla_w_o):
    ada_ctx = jnp.einsum('d,lde->le', jax.nn.silu(c_ctx), w_ada) + b_ada
    ada_lat = jnp.einsum('bd,lde->lbe', jax.nn.silu(c), w_ada) + b_ada[:, None, :]
    cos, sin = axial_angles(x_sample.shape[1])

    xp, xs = x_prompt, x_sample
    new_ckv, new_krope = [], []
    for i in range(DEPTH):
        mp = jnp.split(ada_ctx[i][None, None, :], N_MOD, axis=-1)
        ms = jnp.split(ada_lat[i][:, None, :], N_MOD, axis=-1)
        xp = ffn_sub(xp, mp[0], mp[1], mp[2], ln_g[i, 0], ln_b[i, 0], ffn_w1[i, 0], ffn_w3[i, 0], ffn_w2[i, 0])
        xs = ffn_sub(xs, ms[0], ms[1], ms[2], ln_g[i, 0], ln_b[i, 0], ffn_w1[i, 0], ffn_w3[i, 0], ffn_w2[i, 0])
        hp = modulate(xp, mp[3], mp[4])
        hs = modulate(xs, ms[3], ms[4])
        j = i // 2
        if i % 2 == 0:
            yp = conv_pool_mixer(hp, cp_w_in[j], conv_w[j], conv_b[j], conv_norm_g[j], conv_norm_b[j],
                                 pool_w[j], pool_scale[j], cp_w_out[j])
            ys = conv_pool_mixer(hs, cp_w_in[j], conv_w[j], conv_b[j], conv_norm_g[j], conv_norm_b[j],
                                 pool_w[j], pool_scale[j], cp_w_out[j])
        else:
            ckv_p, kr_p = mla_compress_kv(hp, mla_w_dkv[j], mla_kv_norm_g[j])
            new_ckv.append(ckv_p)
            new_krope.append(kr_p)
            qn_p, qr_p = mla_q(hp, mla_w_dq[j], mla_q_norm_g[j], mla_w_uq[j])
            kn_p, v_p = mla_expand(ckv_p, mla_w_ukv[j])
            yp = mla_attend(qn_p, qr_p, kn_p, kr_p, v_p) @ mla_w_o[j]
            ckv_s, kr_s = mla_compress_kv(hs, mla_w_dkv[j], mla_kv_norm_g[j])
            kr_s = apply_rope_2d(kr_s, cos, sin)
            qn_s, qr_s = mla_q(hs, mla_w_dq[j], mla_q_norm_g[j], mla_w_uq[j])
            qr_s = apply_rope_2d(qr_s, cos, sin)
            ckv_all = jnp.concatenate([ckv_s, cache_mla_ckv[:, j]], axis=1)
            kr_all = jnp.concatenate([kr_s, cache_mla_krope[:, j]], axis=1)
            kn_s, v_s = mla_expand(ckv_all, mla_w_ukv[j])
            ys = mla_attend(qn_s, qr_s, kn_s, kr_all, v_s) @ mla_w_o[j]
        xp = layer_norm(ALPHA * xp + mp[5] * yp, ln_g[i, 1], ln_b[i, 1])
        xs = layer_norm(ALPHA * xs + ms[5] * ys, ln_g[i, 1], ln_b[i, 1])
        xp = ffn_sub(xp, mp[6], mp[7], mp[8], ln_g[i, 2], ln_b[i, 2], ffn_w1[i, 1], ffn_w3[i, 1], ffn_w2[i, 1])
        xs = ffn_sub(xs, ms[6], ms[7], ms[8], ln_g[i, 2], ln_b[i, 2], ffn_w1[i, 1], ffn_w3[i, 1], ffn_w2[i, 1])

    new_mla_ckv = jnp.stack(new_ckv, axis=1)
    new_mla_krope = jnp.stack(new_krope, axis=1)
    return (xp, xs, new_mla_ckv, new_mla_krope)
```

```python
import functools

import jax
import jax.numpy as jnp
import numpy as np
from jax import lax
from jax.experimental import pallas as pl
from jax.experimental.pallas import tpu as pltpu

F32 = jnp.float32
BF16 = jnp.bfloat16

D_MODEL = 1024
BATCH = 32
SEQ = 256
DEPTH = 2
DEC_BATCH = 4
DEC_SEQ = 1024
PAST_LEN = 512
GRID_W = 64
N_MOD = 9
D_FF = 2816
D_CONV = 512
CONV_WIDTH = 31
D_POOL = 512
POOL_WINDOWS = (2, 4, 8, 16)
POOL_GROUP = 128
N_HEADS = 8
QK_NOPE = 128
QK_ROPE = 64
V_DIM = 128
KV_LORA = 256
Q_LORA = 384
ROPE_AXIS_PAIRS = 16
ROPE_BASE = 10000.0
ALPHA = (2 * DEPTH) ** 0.25
LN_EPS = 1e-5
RMS_EPS = 1e-6

N_PROMPT = BATCH * SEQ
N_TOKENS = N_PROMPT + DEC_BATCH * DEC_SEQ
N_GROUPS = 8
HEAD_W = 256
CONV_PAD = 16
CONV_ROWS = 64

VMEM_LIMIT_BYTES = 56 * 1024 * 1024

FFN_TILE = 512
MIX_TILE = 1024
PROJ_TILE = 512
ATTN_Q_TILE = 256


def _layer_norm(z, g, b):
    mu = jnp.mean(z, axis=-1, keepdims=True)
    zc = z - mu
    var = jnp.mean(zc * zc, axis=-1, keepdims=True)
    return zc * lax.rsqrt(var + LN_EPS) * g + b


def _rms_norm(z, g):
    return z * lax.rsqrt(jnp.mean(z * z, axis=-1, keepdims=True) + RMS_EPS) * g


def _silu(z):
    return z * jax.nn.sigmoid(z)


def _dot(a, b):
    return jnp.dot(a, b, preferred_element_type=F32)


def _dot_nt(a, b):
    return lax.dot_general(a, b, (((1,), (1,)), ((), ())), preferred_element_type=F32)


def _const_spec(shape):
    nd = len(shape)
    return pl.BlockSpec(shape, lambda *_: (0,) * nd, pipeline_mode=pl.Buffered(1))


def _group_of_tile(t, tile):
    n_p = N_PROMPT // tile
    per_seq = DEC_SEQ // tile
    return jnp.where(t < n_p, 0, 1 + (t - n_p) // per_seq)


def _params(n_axes=1):
    return pltpu.CompilerParams(
        dimension_semantics=("arbitrary",) * n_axes,
        vmem_limit_bytes=VMEM_LIMIT_BYTES,
    )


def _ada_body(c_ref, w_ref, b_ref, o_ref):
    s = _silu(c_ref[...]).astype(BF16)
    o_ref[...] = _dot(s, w_ref[...].astype(BF16)) + b_ref[...]


def _ada(c_all, w_ada, b_ada):
    tn = D_MODEL
    return pl.pallas_call(
        _ada_body,
        out_shape=jax.ShapeDtypeStruct((DEPTH, N_GROUPS, N_MOD * D_MODEL), F32),
        grid=(DEPTH, N_MOD),
        in_specs=[
            pl.BlockSpec((N_GROUPS, D_MODEL), lambda l, k: (0, 0)),
            pl.BlockSpec((None, D_MODEL, tn), lambda l, k: (l, 0, k)),
            pl.BlockSpec((None, 1, tn), lambda l, k: (l, 0, k)),
        ],
        out_specs=pl.BlockSpec((None, N_GROUPS, tn), lambda l, k: (l, 0, k)),
        compiler_params=_params(2),
        name="ada",
    )(c_all, w_ada, b_ada.reshape(DEPTH, 1, N_MOD * D_MODEL))


def _ffn_body(x_ref, mod_ref, g_ref, b_ref, w1_ref, w3_ref, w2_ref, o_ref, *, k0):
    x = x_ref[...]
    shift = mod_ref[k0:k0 + 1, :]
    scale = mod_ref[k0 + 1:k0 + 2, :]
    gate = mod_ref[k0 + 2:k0 + 3, :]
    h = (x * (1.0 + scale) + shift).astype(BF16)
    a = (_silu(_dot(h, w1_ref[...])) * _dot(h, w3_ref[...])).astype(BF16)
    y = _dot(a, w2_ref[...])
    o_ref[...] = _layer_norm(ALPHA * x + (0.5 * gate) * y, g_ref[...], b_ref[...])


def _ffn(x, mod, ln_g, ln_b, w1, w3, w2, *, k0):
    tm = FFN_TILE
    return pl.pallas_call(
        functools.partial(_ffn_body, k0=k0),
        out_shape=jax.ShapeDtypeStruct((N_TOKENS, D_MODEL), F32),
        grid=(N_TOKENS // tm,),
        in_specs=[
            pl.BlockSpec((tm, D_MODEL), lambda t: (t, 0)),
            pl.BlockSpec((None, N_MOD, D_MODEL), lambda t: (_group_of_tile(t, tm), 0, 0)),
            _const_spec((1, D_MODEL)),
            _const_spec((1, D_MODEL)),
            _const_spec((D_MODEL, D_FF)),
            _const_spec((D_MODEL, D_FF)),
            _const_spec((D_FF, D_MODEL)),
        ],
        out_specs=pl.BlockSpec((tm, D_MODEL), lambda t: (t, 0)),
        compiler_params=_params(),
        name="ffn",
    )(x, mod, ln_g.reshape(1, D_MODEL), ln_b.reshape(1, D_MODEL), w1, w3, w2)


def _conv_pool_chunk(i, carry, *, pad_ref, conv_ref, pool_ref, cw_ref, cb_ref, seq_len):
    chunks_per_seq = seq_len // CONV_ROWS
    s = i // chunks_per_seq
    r0 = pl.multiple_of((i % chunks_per_seq) * CONV_ROWS, CONV_ROWS)
    o0 = pl.multiple_of(i * CONV_ROWS, CONV_ROWS)
    win_rows = CONV_ROWS + 2 * CONV_PAD
    for c in range(D_CONV // 128):
        lanes = slice(c * 128, (c + 1) * 128)
        win = pad_ref[s, pl.ds(r0, win_rows), lanes]
        acc = jnp.zeros((CONV_ROWS, 128), F32)
        for sub in range(8):
            shifted = win[sub:sub + win_rows - 8, :]
            for blk in range(4):
                k = 8 * blk + sub - 1
                if 0 <= k < CONV_WIDTH:
                    acc = acc + cw_ref[k:k + 1, lanes] * shifted[8 * blk:8 * blk + CONV_ROWS, :]
        conv_ref[pl.ds(o0, CONV_ROWS), lanes] = acc + cb_ref[:, lanes]
    t = r0 + lax.broadcasted_iota(jnp.int32, (CONV_ROWS, 128), 0)
    for gi, w in enumerate(POOL_WINDOWS):
        left = w // 2
        right = w - 1 - left
        lanes = slice(D_CONV + gi * 128, D_CONV + (gi + 1) * 128)
        win = pad_ref[s, pl.ds(r0, win_rows), lanes]
        base = CONV_PAD - left
        tree = win
        span = 1
        rows = win_rows
        while span < w:
            rows -= span
            tree = tree[0:rows, :] + tree[span:span + rows, :]
            span *= 2
        total = tree[base:base + CONV_ROWS, :]
        lo = jnp.maximum(t - left, 0)
        hi = jnp.minimum(t + right, seq_len - 1)
        cnt = (hi - lo + 1).astype(F32)
        tok = win[CONV_PAD:CONV_PAD + CONV_ROWS, :]
        pool_ref[pl.ds(o0, CONV_ROWS), gi * 128:(gi + 1) * 128] = total / cnt - tok
    return carry


def _cp_body(x_ref, mod_ref, g_ref, b_ref, win_ref, cw_ref, cb_ref, cng_ref, cnb_ref,
             pw_ref, ps_ref, wout_ref, o_ref, pad_ref, conv_ref, pool_ref, *, n_seq, seq_len):
    x = x_ref[...]
    shift = mod_ref[3:4, :]
    scale = mod_ref[4:5, :]
    gate = mod_ref[5:6, :]
    h = (x * (1.0 + scale) + shift).astype(BF16)
    proj = _dot(h, win_ref[...])
    glu = proj[:, :D_CONV] * jax.nn.sigmoid(proj[:, D_CONV:2 * D_CONV])
    hp = proj[:, 2 * D_CONV:]
    zeros = jnp.zeros((CONV_PAD, D_CONV + D_POOL), F32)
    for s in range(n_seq):
        rows = slice(s * seq_len, (s + 1) * seq_len)
        pad_ref[s, 0:CONV_PAD, :] = zeros
        pad_ref[s, CONV_PAD + seq_len:2 * CONV_PAD + seq_len, :] = zeros
        pad_ref[s, CONV_PAD:CONV_PAD + seq_len, 0:D_CONV] = glu[rows]
        pad_ref[s, CONV_PAD:CONV_PAD + seq_len, D_CONV:D_CONV + D_POOL] = hp[rows]
    lax.fori_loop(
        0, n_seq * seq_len // CONV_ROWS,
        functools.partial(_conv_pool_chunk, pad_ref=pad_ref, conv_ref=conv_ref, pool_ref=pool_ref,
                          cw_ref=cw_ref, cb_ref=cb_ref, seq_len=seq_len),
        0)
    a = _silu(_layer_norm(conv_ref[...], cng_ref[...], cnb_ref[...]))
    d = pool_ref[...].astype(BF16)
    pooled = [_dot(d[:, gi * 128:(gi + 1) * 128], pw_ref[gi]) for gi in range(len(POOL_WINDOWS))]
    bmix = jnp.concatenate(pooled, axis=-1) * ps_ref[...]
    cat = jnp.concatenate([a, bmix], axis=-1).astype(BF16)
    y = _dot(cat, wout_ref[...])
    o_ref[...] = _layer_norm(ALPHA * x + gate * y, g_ref[...], b_ref[...])


def _conv_pool_mixer(x, mod, ln_g, ln_b, w_in, conv_w, conv_b, cn_g, cn_b, pool_w, pool_scale, w_out,
                     prev_out, *, latent):
    tm = MIX_TILE
    seq_len = DEC_SEQ if latent else SEQ
    n_seq = tm // seq_len
    first = N_PROMPT // tm if latent else 0
    n_tiles = (DEC_BATCH * DEC_SEQ if latent else N_PROMPT) // tm
    row_spec = pl.BlockSpec((tm, D_MODEL), lambda t: (first + t, 0))
    in_specs = [
        row_spec,
        pl.BlockSpec((None, N_MOD, D_MODEL), lambda t: (_group_of_tile(first + t, tm), 0, 0)),
        _const_spec((1, D_MODEL)),
        _const_spec((1, D_MODEL)),
        _const_spec((D_MODEL, 2 * D_CONV + D_POOL)),
        _const_spec((CONV_WIDTH, D_CONV)),
        _const_spec((1, D_CONV)),
        _const_spec((1, D_CONV)),
        _const_spec((1, D_CONV)),
        _const_spec((len(POOL_WINDOWS), POOL_GROUP, POOL_GROUP)),
        _const_spec((1, D_POOL)),
        _const_spec((D_CONV + D_POOL, D_MODEL)),
    ]
    args = [x, mod, ln_g.reshape(1, D_MODEL), ln_b.reshape(1, D_MODEL), w_in, conv_w,
            conv_b.reshape(1, D_CONV), cn_g.reshape(1, D_CONV), cn_b.reshape(1, D_CONV),
            pool_w, pool_scale.reshape(1, D_POOL), w_out]
    body = functools.partial(_cp_body, n_seq=n_seq, seq_len=seq_len)
    aliases = {}
    if prev_out is not None:
        in_specs.append(pl.BlockSpec(memory_space=pl.ANY))
        args.append(prev_out)
        aliases = {len(args) - 1: 0}
        inner = body

        def body(*refs):
            n_in = len(args)
            inner(*refs[:n_in - 1], *refs[n_in:])

    return pl.pallas_call(
        body,
        out_shape=jax.ShapeDtypeStruct((N_TOKENS, D_MODEL), F32),
        grid=(n_tiles,),
        in_specs=in_specs,
        out_specs=row_spec,
        scratch_shapes=[
            pltpu.VMEM((n_seq, seq_len + 2 * CONV_PAD, D_CONV + D_POOL), F32),
            pltpu.VMEM((tm, D_CONV), F32),
            pltpu.VMEM((tm, D_POOL), F32),
        ],
        input_output_aliases=aliases,
        compiler_params=_params(),
        name="conv_pool_latent" if latent else "conv_pool_prompt",
    )(*args)


def _mla_proj_body(*refs, rope, emit_cache):
    (x_ref, mod_ref, wdkv_ref, kvg_ref, wdq_ref, qg_ref, wuq_ref, wuk_ref, wuv_ref), rest = refs[:9], refs[9:]
    if rope:
        (wuqs_ref, cos_ref, sin_ref), rest = rest[:3], rest[3:]
    q_ref, k_ref, v_ref = rest[:3]
    x = x_ref[...]
    h = (x * (1.0 + mod_ref[4:5, :]) + mod_ref[3:4, :]).astype(BF16)
    kv = _dot(h, wdkv_ref[...])
    ckv = _rms_norm(kv[:, :KV_LORA], kvg_ref[...])
    kr = kv[:, KV_LORA:KV_LORA + 128]
    qd = _rms_norm(_dot(h, wdq_ref[...]), qg_ref[...]).astype(BF16)
    q = _dot(qd, wuq_ref[...])
    if emit_cache:
        ckv_ref, kr_ref = rest[3:5]
        ckv_ref[...] = ckv
        kr_ref[...] = kr[:, :QK_ROPE]
    if rope:
        cos = cos_ref[...]
        sin = sin_ref[...]
        kr = kr * cos + kv[:, KV_LORA + 128:KV_LORA + 256] * sin
        q_swapped = _dot(qd, wuqs_ref[...])
    ckv16 = ckv.astype(BF16)
    kn = _dot(ckv16, wuk_ref[...])
    v_ref[...] = _dot(ckv16, wuv_ref[...]).astype(BF16)
    kr16 = kr.astype(BF16)
    for hd in range(N_HEADS):
        lo = hd * HEAD_W
        q_ref[:, lo:lo + 128] = q[:, lo:lo + 128].astype(BF16)
        q_rope = q[:, lo + 128:lo + 256]
        if rope:
            q_rope = q_rope * cos + q_swapped[:, hd * 128:(hd + 1) * 128] * sin
        q_ref[:, lo + 128:lo + 256] = q_rope.astype(BF16)
        k_ref[:, lo:lo + 128] = kn[:, hd * 128:(hd + 1) * 128].astype(BF16)
        k_ref[:, lo + 128:lo + 256] = kr16


def _mla_proj(x, mod, w, *, latent):
    tm = PROJ_TILE
    first = N_PROMPT // tm if latent else 0
    n_rows = DEC_BATCH * DEC_SEQ if latent else N_PROMPT
    in_specs = [
        pl.BlockSpec((tm, D_MODEL), lambda t: (first + t, 0)),
        pl.BlockSpec((None, N_MOD, D_MODEL), lambda t: (_group_of_tile(first + t, tm), 0, 0)),
        _const_spec((D_MODEL, KV_LORA + 256)),
        _const_spec((1, KV_LORA)),
        _const_spec((D_MODEL, Q_LORA)),
        _const_spec((1, Q_LORA)),
        _const_spec((Q_LORA, N_HEADS * HEAD_W)),
        _const_spec((KV_LORA, N_HEADS * QK_NOPE)),
        _const_spec((KV_LORA, N_HEADS * V_DIM)),
    ]
    args = [x, mod, w["dkv"], w["kv_g"], w["dq"], w["q_g"], w["uq"], w["uk"], w["uv"]]
    out_shape = [
        jax.ShapeDtypeStruct((n_rows, N_HEADS * HEAD_W), BF16),
        jax.ShapeDtypeStruct((n_rows, N_HEADS * HEAD_W), BF16),
        jax.ShapeDtypeStruct((n_rows, N_HEADS * V_DIM), BF16),
    ]
    out_specs = [
        pl.BlockSpec((tm, N_HEADS * HEAD_W), lambda t: (t, 0)),
        pl.BlockSpec((tm, N_HEADS * HEAD_W), lambda t: (t, 0)),
        pl.BlockSpec((tm, N_HEADS * V_DIM), lambda t: (t, 0)),
    ]
    if latent:
        per_seq = DEC_SEQ // tm
        in_specs += [
            _const_spec((Q_LORA, N_HEADS * 128)),
            pl.BlockSpec((tm, 128), lambda t: (t % per_seq, 0)),
            pl.BlockSpec((tm, 128), lambda t: (t % per_seq, 0)),
        ]
        args += [w["uq_swapped"], w["cos"], w["sin"]]
    else:
        out_shape += [
            jax.ShapeDtypeStruct((n_rows, KV_LORA), F32),
            jax.ShapeDtypeStruct((n_rows, QK_ROPE), F32),
        ]
        out_specs += [
            pl.BlockSpec((tm, KV_LORA), lambda t: (t, 0)),
            pl.BlockSpec((tm, QK_ROPE), lambda t: (t, 0)),
        ]
    return pl.pallas_call(
        functools.partial(_mla_proj_body, rope=latent, emit_cache=not latent),
        out_shape=out_shape,
        grid=(n_rows // tm,),
        in_specs=in_specs,
        out_specs=out_specs,
        compiler_params=_params(),
        name="mla_proj_latent" if latent else "mla_proj_prompt",
    )(*args)


def _cache_expand_body(ckv_ref, kr_ref, wuk_ref, wuv_ref, k_ref, v_ref):
    ckv16 = ckv_ref[...].astype(BF16)
    kn = _dot(ckv16, wuk_ref[...])
    v_ref[...] = _dot(ckv16, wuv_ref[...]).astype(BF16)
    kr16 = kr_ref[...].astype(BF16)
    for hd in range(N_HEADS):
        lo = hd * HEAD_W
        k_ref[:, lo:lo + 128] = kn[:, hd * 128:(hd + 1) * 128].astype(BF16)
        k_ref[:, lo + 128:lo + 256] = kr16


def _cache_expand(ckv, kr_padded, w):
    n_rows = DEC_BATCH * PAST_LEN
    tm = PAST_LEN
    return pl.pallas_call(
        _cache_expand_body,
        out_shape=[
            jax.ShapeDtypeStruct((n_rows, N_HEADS * HEAD_W), BF16),
            jax.ShapeDtypeStruct((n_rows, N_HEADS * V_DIM), BF16),
        ],
        grid=(n_rows // tm,),
        in_specs=[
            pl.BlockSpec((tm, KV_LORA), lambda t: (t, 0)),
            pl.BlockSpec((tm, 128), lambda t: (t, 0)),
            _const_spec((KV_LORA, N_HEADS * QK_NOPE)),
            _const_spec((KV_LORA, N_HEADS * V_DIM)),
        ],
        out_specs=[
            pl.BlockSpec((tm, N_HEADS * HEAD_W), lambda t: (t, 0)),
            pl.BlockSpec((tm, N_HEADS * V_DIM), lambda t: (t, 0)),
        ],
        compiler_params=_params(),
        name="mla_cache_expand",
    )(ckv, kr_padded, w["uk"], w["uv"])


def _attn_body(*refs, with_cache):
    x_ref, mod_ref, g_ref, b_ref, wo_ref, q_ref, k_ref, v_ref = refs[:8]
    if with_cache:
        kc_ref, vc_ref = refs[8:10]
    o_ref, attn_ref = refs[-2:]
    sm_scale = (QK_NOPE + QK_ROPE) ** -0.5
    for hd in range(N_HEADS):
        ql = slice(hd * HEAD_W, (hd + 1) * HEAD_W)
        vl = slice(hd * V_DIM, (hd + 1) * V_DIM)
        qh = q_ref[:, ql]
        s1 = _dot_nt(qh, k_ref[:, ql]) * sm_scale
        m = jnp.max(s1, axis=-1, keepdims=True)
        if with_cache:
            s2 = _dot_nt(qh, kc_ref[:, ql]) * sm_scale
            m = jnp.maximum(m, jnp.max(s2, axis=-1, keepdims=True))
        e1 = jnp.exp(s1 - m)
        denom = jnp.sum(e1, axis=-1, keepdims=True)
        if with_cache:
            e2 = jnp.exp(s2 - m)
            denom = denom + jnp.sum(e2, axis=-1, keepdims=True)
        inv = 1.0 / denom
        out = _dot((e1 * inv).astype(BF16), v_ref[:, vl])
        if with_cache:
            out = out + _dot((e2 * inv).astype(BF16), vc_ref[:, vl])
        attn_ref[:, vl] = out.astype(BF16)
    x = x_ref[...]
    y = _dot(attn_ref[...], wo_ref[...])
    o_ref[...] = _layer_norm(ALPHA * x + mod_ref[5:6, :] * y, g_ref[...], b_ref[...])


def _attention(x, mod, ln_g, ln_b, w_o, q, k, v, cache_kv, prev_out, *, latent):
    tq = ATTN_Q_TILE
    seq_len = DEC_SEQ if latent else SEQ
    q_per_seq = seq_len // tq
    first = N_PROMPT // tq if latent else 0
    n_rows = DEC_BATCH * DEC_SEQ if latent else N_PROMPT
    row_spec = pl.BlockSpec((tq, D_MODEL), lambda t: (first + t, 0))
    in_specs = [
        row_spec,
        pl.BlockSpec((None, N_MOD, D_MODEL), lambda t: (_group_of_tile(first + t, tq), 0, 0)),
        _const_spec((1, D_MODEL)),
        _const_spec((1, D_MODEL)),
        _const_spec((N_HEADS * V_DIM, D_MODEL)),
        pl.BlockSpec((tq, N_HEADS * HEAD_W), lambda t: (t, 0)),
        pl.BlockSpec((seq_len, N_HEADS * HEAD_W), lambda t: (t // q_per_seq, 0)),
        pl.BlockSpec((seq_len, N_HEADS * V_DIM), lambda t: (t // q_per_seq, 0)),
    ]
    args = [x, mod, ln_g.reshape(1, D_MODEL), ln_b.reshape(1, D_MODEL), w_o, q, k, v]
    if latent:
        in_specs += [
            pl.BlockSpec((PAST_LEN, N_HEADS * HEAD_W), lambda t: (t // q_per_seq, 0)),
            pl.BlockSpec((PAST_LEN, N_HEADS * V_DIM), lambda t: (t // q_per_seq, 0)),
        ]
        args += list(cache_kv)
    body = functools.partial(_attn_body, with_cache=latent)
    aliases = {}
    if prev_out is not None:
        in_specs.append(pl.BlockSpec(memory_space=pl.ANY))
        args.append(prev_out)
        aliases = {len(args) - 1: 0}
        inner = body

        def body(*refs):
            n_in = len(args)
            inner(*refs[:n_in - 1], *refs[n_in:])

    return pl.pallas_call(
        body,
        out_shape=jax.ShapeDtypeStruct((N_TOKENS, D_MODEL), F32),
        grid=(n_rows // tq,),
        in_specs=in_specs,
        out_specs=row_spec,
        scratch_shapes=[pltpu.VMEM((tq, N_HEADS * V_DIM), BF16)],
        input_output_aliases=aliases,
        compiler_params=_params(),
        name="mla_attn_latent" if latent else "mla_attn_prompt",
    )(*args)


def _rope_swap_perm():
    idx = np.arange(QK_ROPE)
    return np.where((idx % 32) < ROPE_AXIS_PAIRS, idx + ROPE_AXIS_PAIRS, idx - ROPE_AXIS_PAIRS)


def _rope_tables():
    n = DEC_SEQ
    row = jnp.repeat(jnp.arange(n // GRID_W), GRID_W)
    col = jnp.tile(jnp.arange(GRID_W), n // GRID_W)
    inv = ROPE_BASE ** (-jnp.arange(ROPE_AXIS_PAIRS, dtype=F32) / ROPE_AXIS_PAIRS)
    ar = row[:, None] * inv
    ac = col[:, None] * inv
    zeros = jnp.zeros((n, 128 - QK_ROPE), F32)
    cos = jnp.concatenate([jnp.cos(ar), jnp.cos(ar), jnp.cos(ac), jnp.cos(ac), zeros], axis=-1)
    sin = jnp.concatenate([-jnp.sin(ar), jnp.sin(ar), -jnp.sin(ac), jnp.sin(ac), zeros], axis=-1)
    return cos, sin


def _mla_weights(w_dq, q_norm_g, w_uq, w_dkv, kv_norm_g, w_ukv, w_o):
    swap = _rope_swap_perm()
    z64 = jnp.zeros((D_MODEL, 128 - QK_ROPE), F32)
    kr_cols = w_dkv[:, KV_LORA:]
    dkv = jnp.concatenate([w_dkv[:, :KV_LORA], kr_cols, z64, kr_cols[:, swap], z64], axis=-1)
    uq = w_uq.reshape(Q_LORA, N_HEADS, QK_NOPE + QK_ROPE)
    zq = jnp.zeros((Q_LORA, N_HEADS, 128 - QK_ROPE), F32)
    uq_main = jnp.concatenate([uq, zq], axis=-1).reshape(Q_LORA, N_HEADS * HEAD_W)
    uq_swapped = jnp.concatenate([uq[:, :, QK_NOPE:][:, :, swap], zq], axis=-1).reshape(Q_LORA, N_HEADS * 128)
    ukv = w_ukv.reshape(KV_LORA, N_HEADS, QK_NOPE + V_DIM)
    cos, sin = _rope_tables()
    return {
        "dkv": dkv.astype(BF16),
        "kv_g": kv_norm_g.reshape(1, KV_LORA),
        "dq": w_dq.astype(BF16),
        "q_g": q_norm_g.reshape(1, Q_LORA),
        "uq": uq_main.astype(BF16),
        "uq_swapped": uq_swapped.astype(BF16),
        "uk": ukv[:, :, :QK_NOPE].reshape(KV_LORA, N_HEADS * QK_NOPE).astype(BF16),
        "uv": ukv[:, :, QK_NOPE:].reshape(KV_LORA, N_HEADS * V_DIM).astype(BF16),
        "o": w_o.astype(BF16),
        "cos": cos,
        "sin": sin,
    }


def kernel(x_prompt, x_sample, cache_mla_ckv, cache_mla_krope, c, c_ctx, w_ada, b_ada, ln_g, ln_b, ffn_w1, ffn_w3, ffn_w2, cp_w_in, conv_w, conv_b, conv_norm_g, conv_norm_b, pool_w, pool_scale, cp_w_out, mla_w_dq, mla_q_norm_g, mla_w_uq, mla_w_dkv, mla_kv_norm_g, mla_w_ukv, mla_w_o):
    c_all = jnp.concatenate(
        [c_ctx[None, :], c, jnp.zeros((N_GROUPS - 1 - DEC_BATCH, D_MODEL), F32)], axis=0)
    ada = _ada(c_all, w_ada, b_ada).reshape(DEPTH, N_GROUPS, N_MOD, D_MODEL)

    x = jnp.concatenate([x_prompt.reshape(N_PROMPT, D_MODEL),
                         x_sample.reshape(DEC_BATCH * DEC_SEQ, D_MODEL)], axis=0)
    w1 = ffn_w1.astype(BF16)
    w3 = ffn_w3.astype(BF16)
    w2 = ffn_w2.astype(BF16)
    new_ckv, new_krope = [], []
    for i in range(DEPTH):
        mod = ada[i]
        j = i // 2
        x = _ffn(x, mod, ln_g[i, 0], ln_b[i, 0], w1[i, 0], w3[i, 0], w2[i, 0], k0=0)
        if i % 2 == 0:
            cp_args = (x, mod, ln_g[i, 1], ln_b[i, 1], cp_w_in[j].astype(BF16), conv_w[j], conv_b[j],
                       conv_norm_g[j], conv_norm_b[j], pool_w[j].astype(BF16), pool_scale[j],
                       cp_w_out[j].astype(BF16))
            y = _conv_pool_mixer(*cp_args, None, latent=False)
            x = _conv_pool_mixer(*cp_args, y, latent=True)
        else:
            w = _mla_weights(mla_w_dq[j], mla_q_norm_g[j], mla_w_uq[j], mla_w_dkv[j],
                             mla_kv_norm_g[j], mla_w_ukv[j], mla_w_o[j])
            q_p, k_p, v_p, ckv_p, kr_p = _mla_proj(x, mod, w, latent=False)
            q_s, k_s, v_s = _mla_proj(x, mod, w, latent=True)
            new_ckv.append(ckv_p.reshape(BATCH, SEQ, KV_LORA))
            new_krope.append(kr_p.reshape(BATCH, SEQ, QK_ROPE))
            kr_cache = jnp.pad(cache_mla_krope[:, j].reshape(DEC_BATCH * PAST_LEN, QK_ROPE),
                               ((0, 0), (0, 128 - QK_ROPE)))
            cache_kv = _cache_expand(cache_mla_ckv[:, j].reshape(DEC_BATCH * PAST_LEN, KV_LORA), kr_cache, w)
            attn_args = (x, mod, ln_g[i, 1], ln_b[i, 1], w["o"])
            y = _attention(*attn_args, q_p, k_p, v_p, None, None, latent=False)
            x = _attention(*attn_args, q_s, k_s, v_s, cache_kv, y, latent=True)
        x = _ffn(x, mod, ln_g[i, 2], ln_b[i, 2], w1[i, 1], w3[i, 1], w2[i, 1], k0=6)

    y_prompt = x[:N_PROMPT].reshape(BATCH, SEQ, D_MODEL)
    y_sample = x[N_PROMPT:].reshape(DEC_BATCH, DEC_SEQ, D_MODEL)
    return (y_prompt, y_sample, jnp.stack(new_ckv, axis=1), jnp.stack(new_krope, axis=1))
```

```python
import functools

import jax
import jax.numpy as jnp
import numpy as np
from jax import lax
from jax.experimental import pallas as pl
from jax.experimental.pallas import tpu as pltpu

F32 = jnp.float32
BF16 = jnp.bfloat16

D_MODEL = 1024
BATCH = 32
SEQ = 256
DEPTH = 2
DEC_BATCH = 4
DEC_SEQ = 1024
PAST_LEN = 512
GRID_W = 64
N_MOD = 9
D_FF = 2816
D_CONV = 512
CONV_WIDTH = 31
D_POOL = 512
POOL_WINDOWS = (2, 4, 8, 16)
POOL_GROUP = 128
N_HEADS = 8
QK_NOPE = 128
QK_ROPE = 64
V_DIM = 128
KV_LORA = 256
Q_LORA = 384
ROPE_AXIS_PAIRS = 16
ROPE_BASE = 10000.0
ALPHA = (2 * DEPTH) ** 0.25
LN_EPS = 1e-5
RMS_EPS = 1e-6

N_PROMPT = BATCH * SEQ
N_TOKENS = N_PROMPT + DEC_BATCH * DEC_SEQ
N_GROUPS = 8
HEAD_W = 256
CONV_PAD = 16
CONV_ROWS = 64

VMEM_LIMIT_BYTES = 56 * 1024 * 1024

FFN_TILE = 512
MIX_TILE = 1024
PROJ_TILE = 512
ATTN_Q_TILE = 256


def _layer_norm(z, g, b):
    mu = jnp.mean(z, axis=-1, keepdims=True)
    zc = z - mu
    var = jnp.mean(zc * zc, axis=-1, keepdims=True)
    return zc * lax.rsqrt(var + LN_EPS) * g + b


def _rms_norm(z, g):
    return z * lax.rsqrt(jnp.mean(z * z, axis=-1, keepdims=True) + RMS_EPS) * g


def _silu(z):
    return z * jax.nn.sigmoid(z)


def _dot(a, b):
    return jnp.dot(a, b, preferred_element_type=F32)


def _dot_nt(a, b):
    return lax.dot_general(a, b, (((1,), (1,)), ((), ())), preferred_element_type=F32)


def _const_spec(shape):
    nd = len(shape)
    return pl.BlockSpec(shape, lambda *_: (0,) * nd, pipeline_mode=pl.Buffered(1))


def _group_of_tile(t, tile):
    n_p = N_PROMPT // tile
    per_seq = DEC_SEQ // tile
    return jnp.where(t < n_p, 0, 1 + (t - n_p) // per_seq)


def _params(n_axes=1):
    return pltpu.CompilerParams(
        dimension_semantics=("arbitrary",) * n_axes,
        vmem_limit_bytes=VMEM_LIMIT_BYTES,
    )


def _ada_body(c_ref, w_ref, b_ref, o_ref):
    s = _silu(c_ref[...]).astype(BF16)
    o_ref[...] = _dot(s, w_ref[...].astype(BF16)) + b_ref[...]


def _ada(c_all, w_ada, b_ada):
    tn = D_MODEL
    return pl.pallas_call(
        _ada_body,
        out_shape=jax.ShapeDtypeStruct((DEPTH, N_GROUPS, N_MOD * D_MODEL), F32),
        grid=(DEPTH, N_MOD),
        in_specs=[
            pl.BlockSpec((N_GROUPS, D_MODEL), lambda l, k: (0, 0)),
            pl.BlockSpec((None, D_MODEL, tn), lambda l, k: (l, 0, k)),
            pl.BlockSpec((None, 1, tn), lambda l, k: (l, 0, k)),
        ],
        out_specs=pl.BlockSpec((None, N_GROUPS, tn), lambda l, k: (l, 0, k)),
        compiler_params=_params(2),
        name="ada",
    )(c_all, w_ada, b_ada.reshape(DEPTH, 1, N_MOD * D_MODEL))


def _ffn_body(*refs, k0, split_in, split_out, cast_next):
    refs = list(refs)
    n_x = 2 if split_in else 1
    x_refs, refs = refs[:n_x], refs[n_x:]
    (mod_ref, g_ref, b_ref, w1_ref, w3_ref, w2_ref), refs = refs[:6], refs[6:]
    if cast_next:
        next_f32, refs = refs[:3], refs[3:]
    n_o = 2 if split_out else 1
    o_refs, refs = refs[:n_o], refs[n_o:]
    t = pl.program_id(0)
    is_prompt = t < N_PROMPT // FFN_TILE
    if cast_next:
        for src, dst in zip(next_f32, refs):
            dst[...] = src[...].astype(BF16)
    if split_in:
        x = jnp.where(is_prompt, x_refs[0][...], x_refs[1][...])
    else:
        x = x_refs[0][...]
    shift = mod_ref[k0:k0 + 1, :]
    scale = mod_ref[k0 + 1:k0 + 2, :]
    gate = mod_ref[k0 + 2:k0 + 3, :]
    h = (x * (1.0 + scale) + shift).astype(BF16)
    a = (_silu(_dot(h, w1_ref[...])) * _dot(h, w3_ref[...])).astype(BF16)
    y = _dot(a, w2_ref[...])
    out = _layer_norm(ALPHA * x + (0.5 * gate) * y, g_ref[...], b_ref[...])
    if split_out:
        @pl.when(is_prompt)
        def _():
            o_refs[0][...] = out

        @pl.when(jnp.logical_not(is_prompt))
        def _():
            o_refs[1][...] = out
    else:
        o_refs[0][...] = out


def _slab_rows(n_rows, n_steps):
    n_slabs = max(d for d in range(1, n_steps + 1) if n_rows % d == 0 and (n_rows // d) % 16 == 0)
    return n_rows // n_slabs, n_slabs


def _ffn(xs, mod, ln_g, ln_b, w1, w3, w2, next_w=None, next_idx=None, *, k0, split_out=False):
    tm = FFN_TILE
    n_steps = N_TOKENS // tm
    n_p = N_PROMPT // tm
    split_in = isinstance(xs, (tuple, list))
    xs = list(xs) if split_in else [xs]
    prompt_spec = pl.BlockSpec((tm, D_MODEL), lambda t: (jnp.minimum(t, n_p - 1), 0))
    latent_spec = pl.BlockSpec((tm, D_MODEL), lambda t: (jnp.maximum(t - n_p, 0), 0))
    merged_spec = pl.BlockSpec((tm, D_MODEL), lambda t: (t, 0))
    in_specs = ([prompt_spec, latent_spec] if split_in else [merged_spec]) + [
        pl.BlockSpec((None, N_MOD, D_MODEL), lambda t: (_group_of_tile(t, tm), 0, 0)),
        _const_spec((1, D_MODEL)),
        _const_spec((1, D_MODEL)),
        _const_spec((D_MODEL, D_FF)),
        _const_spec((D_MODEL, D_FF)),
        _const_spec((D_FF, D_MODEL)),
    ]
    args = xs + [mod, ln_g.reshape(1, D_MODEL), ln_b.reshape(1, D_MODEL), w1, w3, w2]
    if split_out:
        out_shape = [jax.ShapeDtypeStruct((N_PROMPT, D_MODEL), F32),
                     jax.ShapeDtypeStruct((N_TOKENS - N_PROMPT, D_MODEL), F32)]
        out_specs = [prompt_spec, latent_spec]
    else:
        out_shape = [jax.ShapeDtypeStruct((N_TOKENS, D_MODEL), F32)]
        out_specs = [merged_spec]
    if next_w is not None:
        li, lj = next_idx
        for wn in next_w:
            n_rows, n_cols = wn.shape[2:]
            rows, n_slabs = _slab_rows(n_rows, n_steps)
            in_specs.append(pl.BlockSpec(
                (None, None, rows, n_cols),
                lambda t, n_slabs=n_slabs: (li, lj, jnp.minimum(t, n_slabs - 1), 0)))
            args.append(wn)
            out_shape.append(jax.ShapeDtypeStruct((n_rows, n_cols), BF16))
            out_specs.append(pl.BlockSpec(
                (rows, n_cols), lambda t, n_slabs=n_slabs: (jnp.minimum(t, n_slabs - 1), 0)))
    outs = pl.pallas_call(
        functools.partial(_ffn_body, k0=k0, split_in=split_in, split_out=split_out,
                          cast_next=next_w is not None),
        out_shape=out_shape,
        grid=(n_steps,),
        in_specs=in_specs,
        out_specs=out_specs,
        compiler_params=_params(),
        name="ffn",
    )(*args)
    n_tok = 2 if split_out else 1
    tok = tuple(outs[:n_tok]) if split_out else outs[0]
    return tok, tuple(outs[n_tok:])


def _conv_pool_chunk(i, carry, *, pad_ref, conv_ref, pool_ref, cw_ref, cb_ref, seq_len):
    chunks_per_seq = seq_len // CONV_ROWS
    s = i // chunks_per_seq
    r0 = pl.multiple_of((i % chunks_per_seq) * CONV_ROWS, CONV_ROWS)
    o0 = pl.multiple_of(i * CONV_ROWS, CONV_ROWS)
    win_rows = CONV_ROWS + 2 * CONV_PAD
    for c in range(D_CONV // 128):
        lanes = slice(c * 128, (c + 1) * 128)
        win = pad_ref[s, pl.ds(r0, win_rows), lanes]
        acc = jnp.zeros((CONV_ROWS, 128), F32)
        for sub in range(8):
            shifted = win[sub:sub + win_rows - 8, :]
            for blk in range(4):
                k = 8 * blk + sub - 1
                if 0 <= k < CONV_WIDTH:
                    acc = acc + cw_ref[k:k + 1, lanes] * shifted[8 * blk:8 * blk + CONV_ROWS, :]
        conv_ref[pl.ds(o0, CONV_ROWS), lanes] = acc + cb_ref[:, lanes]
    t = r0 + lax.broadcasted_iota(jnp.int32, (CONV_ROWS, 128), 0)
    for gi, w in enumerate(POOL_WINDOWS):
        left = w // 2
        right = w - 1 - left
        lanes = slice(D_CONV + gi * 128, D_CONV + (gi + 1) * 128)
        win = pad_ref[s, pl.ds(r0, win_rows), lanes]
        base = CONV_PAD - left
        tree = win
        span = 1
        rows = win_rows
        while span < w:
            rows -= span
            tree = tree[0:rows, :] + tree[span:span + rows, :]
            span *= 2
        total = tree[base:base + CONV_ROWS, :]
        lo = jnp.maximum(t - left, 0)
        hi = jnp.minimum(t + right, seq_len - 1)
        cnt = (hi - lo + 1).astype(F32)
        tok = win[CONV_PAD:CONV_PAD + CONV_ROWS, :]
        pool_ref[pl.ds(o0, CONV_ROWS), gi * 128:(gi + 1) * 128] = total / cnt - tok
    return carry


def _cp_body(x_ref, mod_ref, g_ref, b_ref, win_ref, cw_ref, cb_ref, cng_ref, cnb_ref,
             pw_ref, ps_ref, wout_ref, o_ref, pad_ref, conv_ref, pool_ref, *, n_seq, seq_len):
    x = x_ref[...]
    shift = mod_ref[3:4, :]
    scale = mod_ref[4:5, :]
    gate = mod_ref[5:6, :]
    h = (x * (1.0 + scale) + shift).astype(BF16)
    proj = _dot(h, win_ref[...])
    glu = proj[:, :D_CONV] * jax.nn.sigmoid(proj[:, D_CONV:2 * D_CONV])
    hp = proj[:, 2 * D_CONV:]
    zeros = jnp.zeros((CONV_PAD, D_CONV + D_POOL), F32)
    for s in range(n_seq):
        rows = slice(s * seq_len, (s + 1) * seq_len)
        pad_ref[s, 0:CONV_PAD, :] = zeros
        pad_ref[s, CONV_PAD + seq_len:2 * CONV_PAD + seq_len, :] = zeros
        pad_ref[s, CONV_PAD:CONV_PAD + seq_len, 0:D_CONV] = glu[rows]
        pad_ref[s, CONV_PAD:CONV_PAD + seq_len, D_CONV:D_CONV + D_POOL] = hp[rows]
    lax.fori_loop(
        0, n_seq * seq_len // CONV_ROWS,
        functools.partial(_conv_pool_chunk, pad_ref=pad_ref, conv_ref=conv_ref, pool_ref=pool_ref,
                          cw_ref=cw_ref, cb_ref=cb_ref, seq_len=seq_len),
        0)
    a = _silu(_layer_norm(conv_ref[...], cng_ref[...], cnb_ref[...]))
    d = pool_ref[...].astype(BF16)
    pooled = [_dot(d[:, gi * 128:(gi + 1) * 128], pw_ref[gi]) for gi in range(len(POOL_WINDOWS))]
    bmix = jnp.concatenate(pooled, axis=-1) * ps_ref[...]
    cat = jnp.concatenate([a, bmix], axis=-1).astype(BF16)
    y = _dot(cat, wout_ref[...])
    o_ref[...] = _layer_norm(ALPHA * x + gate * y, g_ref[...], b_ref[...])


def _conv_pool_mixer(x, mod, ln_g, ln_b, w_in, conv_w, conv_b, cn_g, cn_b, pool_w, pool_scale, w_out,
                     prev_out, *, latent):
    tm = MIX_TILE
    seq_len = DEC_SEQ if latent else SEQ
    n_seq = tm // seq_len
    first = N_PROMPT // tm if latent else 0
    n_tiles = (DEC_BATCH * DEC_SEQ if latent else N_PROMPT) // tm
    row_spec = pl.BlockSpec((tm, D_MODEL), lambda t: (first + t, 0))
    in_specs = [
        row_spec,
        pl.BlockSpec((None, N_MOD, D_MODEL), lambda t: (_group_of_tile(first + t, tm), 0, 0)),
        _const_spec((1, D_MODEL)),
        _const_spec((1, D_MODEL)),
        _const_spec((D_MODEL, 2 * D_CONV + D_POOL)),
        _const_spec((CONV_WIDTH, D_CONV)),
        _const_spec((1, D_CONV)),
        _const_spec((1, D_CONV)),
        _const_spec((1, D_CONV)),
        _const_spec((len(POOL_WINDOWS), POOL_GROUP, POOL_GROUP)),
        _const_spec((1, D_POOL)),
        _const_spec((D_CONV + D_POOL, D_MODEL)),
    ]
    args = [x, mod, ln_g.reshape(1, D_MODEL), ln_b.reshape(1, D_MODEL), w_in, conv_w,
            conv_b.reshape(1, D_CONV), cn_g.reshape(1, D_CONV), cn_b.reshape(1, D_CONV),
            pool_w, pool_scale.reshape(1, D_POOL), w_out]
    body = functools.partial(_cp_body, n_seq=n_seq, seq_len=seq_len)
    aliases = {}
    if prev_out is not None:
        in_specs.append(pl.BlockSpec(memory_space=pl.ANY))
        args.append(prev_out)
        aliases = {len(args) - 1: 0}
        inner = body

        def body(*refs):
            n_in = len(args)
            inner(*refs[:n_in - 1], *refs[n_in:])

    return pl.pallas_call(
        body,
        out_shape=jax.ShapeDtypeStruct((N_TOKENS, D_MODEL), F32),
        grid=(n_tiles,),
        in_specs=in_specs,
        out_specs=row_spec,
        scratch_shapes=[
            pltpu.VMEM((n_seq, seq_len + 2 * CONV_PAD, D_CONV + D_POOL), F32),
            pltpu.VMEM((tm, D_CONV), F32),
            pltpu.VMEM((tm, D_POOL), F32),
        ],
        input_output_aliases=aliases,
        compiler_params=_params(),
        name="conv_pool_latent" if latent else "conv_pool_prompt",
    )(*args)


def _mla_proj_body(*refs, rope, emit_cache):
    (x_ref, mod_ref, wdkv_ref, kvg_ref, wdq_ref, qg_ref, wuq_ref, wuk_ref, wuv_ref), rest = refs[:9], refs[9:]
    if rope:
        (wuqs_ref, cos_ref, sin_ref), rest = rest[:3], rest[3:]
    q_ref, k_ref, v_ref = rest[:3]
    x = x_ref[...]
    h = (x * (1.0 + mod_ref[4:5, :]) + mod_ref[3:4, :]).astype(BF16)
    kv = _dot(h, wdkv_ref[...])
    ckv = _rms_norm(kv[:, :KV_LORA], kvg_ref[...])
    kr = kv[:, KV_LORA:KV_LORA + 128]
    qd = _rms_norm(_dot(h, wdq_ref[...]), qg_ref[...]).astype(BF16)
    q = _dot(qd, wuq_ref[...])
    if emit_cache:
        ckv_ref, kr_ref = rest[3:5]
        ckv_ref[...] = ckv
        kr_ref[...] = kr[:, :QK_ROPE]
    if rope:
        cos = cos_ref[...]
        sin = sin_ref[...]
        kr = kr * cos + kv[:, KV_LORA + 128:KV_LORA + 256] * sin
        q_swapped = _dot(qd, wuqs_ref[...])
    ckv16 = ckv.astype(BF16)
    kn = _dot(ckv16, wuk_ref[...])
    v_ref[...] = _dot(ckv16, wuv_ref[...]).astype(BF16)
    kr16 = kr.astype(BF16)
    for hd in range(N_HEADS):
        lo = hd * HEAD_W
        q_ref[:, lo:lo + 128] = q[:, lo:lo + 128].astype(BF16)
        q_rope = q[:, lo + 128:lo + 256]
        if rope:
            q_rope = q_rope * cos + q_swapped[:, hd * 128:(hd + 1) * 128] * sin
        q_ref[:, lo + 128:lo + 256] = q_rope.astype(BF16)
        k_ref[:, lo:lo + 128] = kn[:, hd * 128:(hd + 1) * 128].astype(BF16)
        k_ref[:, lo + 128:lo + 256] = kr16


def _mla_proj(x, mod, w, *, latent):
    tm = PROJ_TILE
    first = N_PROMPT // tm if latent else 0
    n_rows = DEC_BATCH * DEC_SEQ if latent else N_PROMPT
    in_specs = [
        pl.BlockSpec((tm, D_MODEL), lambda t: (first + t, 0)),
        pl.BlockSpec((None, N_MOD, D_MODEL), lambda t: (_group_of_tile(first + t, tm), 0, 0)),
        _const_spec((D_MODEL, KV_LORA + 256)),
        _const_spec((1, KV_LORA)),
        _const_spec((D_MODEL, Q_LORA)),
        _const_spec((1, Q_LORA)),
        _const_spec((Q_LORA, N_HEADS * HEAD_W)),
        _const_spec((KV_LORA, N_HEADS * QK_NOPE)),
        _const_spec((KV_LORA, N_HEADS * V_DIM)),
    ]
    args = [x, mod, w["dkv"], w["kv_g"], w["dq"], w["q_g"], w["uq"], w["uk"], w["uv"]]
    out_shape = [
        jax.ShapeDtypeStruct((n_rows, N_HEADS * HEAD_W), BF16),
        jax.ShapeDtypeStruct((n_rows, N_HEADS * HEAD_W), BF16),
        jax.ShapeDtypeStruct((n_rows, N_HEADS * V_DIM), BF16),
    ]
    out_specs = [
        pl.BlockSpec((tm, N_HEADS * HEAD_W), lambda t: (t, 0)),
        pl.BlockSpec((tm, N_HEADS * HEAD_W), lambda t: (t, 0)),
        pl.BlockSpec((tm, N_HEADS * V_DIM), lambda t: (t, 0)),
    ]
    if latent:
        per_seq = DEC_SEQ // tm
        in_specs += [
            _const_spec((Q_LORA, N_HEADS * 128)),
            pl.BlockSpec((tm, 128), lambda t: (t % per_seq, 0)),
            pl.BlockSpec((tm, 128), lambda t: (t % per_seq, 0)),
        ]
        args += [w["uq_swapped"], w["cos"], w["sin"]]
    else:
        out_shape += [
            jax.ShapeDtypeStruct((n_rows, KV_LORA), F32),
            jax.ShapeDtypeStruct((n_rows, QK_ROPE), F32),
        ]
        out_specs += [
            pl.BlockSpec((tm, KV_LORA), lambda t: (t, 0)),
            pl.BlockSpec((tm, QK_ROPE), lambda t: (t, 0)),
        ]
    return pl.pallas_call(
        functools.partial(_mla_proj_body, rope=latent, emit_cache=not latent),
        out_shape=out_shape,
        grid=(n_rows // tm,),
        in_specs=in_specs,
        out_specs=out_specs,
        compiler_params=_params(),
        name="mla_proj_latent" if latent else "mla_proj_prompt",
    )(*args)


def _cache_expand_body(ckv_ref, kr_ref, wuk_ref, wuv_ref, k_ref, v_ref):
    ckv16 = ckv_ref[...].astype(BF16)
    kn = _dot(ckv16, wuk_ref[...])
    v_ref[...] = _dot(ckv16, wuv_ref[...]).astype(BF16)
    kr16 = kr_ref[...].astype(BF16)
    for hd in range(N_HEADS):
        lo = hd * HEAD_W
        k_ref[:, lo:lo + 128] = kn[:, hd * 128:(hd + 1) * 128].astype(BF16)
        k_ref[:, lo + 128:lo + 256] = kr16


def _cache_expand(ckv, kr_padded, w):
    n_rows = DEC_BATCH * PAST_LEN
    tm = PAST_LEN
    return pl.pallas_call(
        _cache_expand_body,
        out_shape=[
            jax.ShapeDtypeStruct((n_rows, N_HEADS * HEAD_W), BF16),
            jax.ShapeDtypeStruct((n_rows, N_HEADS * V_DIM), BF16),
        ],
        grid=(n_rows // tm,),
        in_specs=[
            pl.BlockSpec((tm, KV_LORA), lambda t: (t, 0)),
            pl.BlockSpec((tm, 128), lambda t: (t, 0)),
            _const_spec((KV_LORA, N_HEADS * QK_NOPE)),
            _const_spec((KV_LORA, N_HEADS * V_DIM)),
        ],
        out_specs=[
            pl.BlockSpec((tm, N_HEADS * HEAD_W), lambda t: (t, 0)),
            pl.BlockSpec((tm, N_HEADS * V_DIM), lambda t: (t, 0)),
        ],
        compiler_params=_params(),
        name="mla_cache_expand",
    )(ckv, kr_padded, w["uk"], w["uv"])


def _attn_body(*refs, with_cache):
    x_ref, mod_ref, g_ref, b_ref, wo_ref, q_ref, k_ref, v_ref = refs[:8]
    if with_cache:
        kc_ref, vc_ref = refs[8:10]
    o_ref, attn_ref = refs[-2:]
    sm_scale = (QK_NOPE + QK_ROPE) ** -0.5
    for hd in range(N_HEADS):
        ql = slice(hd * HEAD_W, (hd + 1) * HEAD_W)
        vl = slice(hd * V_DIM, (hd + 1) * V_DIM)
        qh = q_ref[:, ql]
        s1 = _dot_nt(qh, k_ref[:, ql]) * sm_scale
        m = jnp.max(s1, axis=-1, keepdims=True)
        if with_cache:
            s2 = _dot_nt(qh, kc_ref[:, ql]) * sm_scale
            m = jnp.maximum(m, jnp.max(s2, axis=-1, keepdims=True))
        e1 = jnp.exp(s1 - m)
        denom = jnp.sum(e1, axis=-1, keepdims=True)
        if with_cache:
            e2 = jnp.exp(s2 - m)
            denom = denom + jnp.sum(e2, axis=-1, keepdims=True)
        inv = 1.0 / denom
        out = _dot((e1 * inv).astype(BF16), v_ref[:, vl])
        if with_cache:
            out = out + _dot((e2 * inv).astype(BF16), vc_ref[:, vl])
        attn_ref[:, vl] = out.astype(BF16)
    x = x_ref[...]
    y = _dot(attn_ref[...], wo_ref[...])
    o_ref[...] = _layer_norm(ALPHA * x + mod_ref[5:6, :] * y, g_ref[...], b_ref[...])


def _attention(x, mod, ln_g, ln_b, w_o, q, k, v, cache_kv, prev_out, *, latent):
    tq = ATTN_Q_TILE
    seq_len = DEC_SEQ if latent else SEQ
    q_per_seq = seq_len // tq
    first = N_PROMPT // tq if latent else 0
    n_rows = DEC_BATCH * DEC_SEQ if latent else N_PROMPT
    row_spec = pl.BlockSpec((tq, D_MODEL), lambda t: (first + t, 0))
    in_specs = [
        row_spec,
        pl.BlockSpec((None, N_MOD, D_MODEL), lambda t: (_group_of_tile(first + t, tq), 0, 0)),
        _const_spec((1, D_MODEL)),
        _const_spec((1, D_MODEL)),
        _const_spec((N_HEADS * V_DIM, D_MODEL)),
        pl.BlockSpec((tq, N_HEADS * HEAD_W), lambda t: (t, 0)),
        pl.BlockSpec((seq_len, N_HEADS * HEAD_W), lambda t: (t // q_per_seq, 0)),
        pl.BlockSpec((seq_len, N_HEADS * V_DIM), lambda t: (t // q_per_seq, 0)),
    ]
    args = [x, mod, ln_g.reshape(1, D_MODEL), ln_b.reshape(1, D_MODEL), w_o, q, k, v]
    if latent:
        in_specs += [
            pl.BlockSpec((PAST_LEN, N_HEADS * HEAD_W), lambda t: (t // q_per_seq, 0)),
            pl.BlockSpec((PAST_LEN, N_HEADS * V_DIM), lambda t: (t // q_per_seq, 0)),
        ]
        args += list(cache_kv)
    body = functools.partial(_attn_body, with_cache=latent)
    aliases = {}
    if prev_out is not None:
        in_specs.append(pl.BlockSpec(memory_space=pl.ANY))
        args.append(prev_out)
        aliases = {len(args) - 1: 0}
        inner = body

        def body(*refs):
            n_in = len(args)
            inner(*refs[:n_in - 1], *refs[n_in:])

    return pl.pallas_call(
        body,
        out_shape=jax.ShapeDtypeStruct((N_TOKENS, D_MODEL), F32),
        grid=(n_rows // tq,),
        in_specs=in_specs,
        out_specs=row_spec,
        scratch_shapes=[pltpu.VMEM((tq, N_HEADS * V_DIM), BF16)],
        input_output_aliases=aliases,
        compiler_params=_params(),
        name="mla_attn_latent" if latent else "mla_attn_prompt",
    )(*args)


def _rope_swap_perm():
    idx = np.arange(QK_ROPE)
    return np.where((idx % 32) < ROPE_AXIS_PAIRS, idx + ROPE_AXIS_PAIRS, idx - ROPE_AXIS_PAIRS)


def _rope_tables():
    n = DEC_SEQ
    row = jnp.repeat(jnp.arange(n // GRID_W), GRID_W)
    col = jnp.tile(jnp.arange(GRID_W), n // GRID_W)
    inv = ROPE_BASE ** (-jnp.arange(ROPE_AXIS_PAIRS, dtype=F32) / ROPE_AXIS_PAIRS)
    ar = row[:, None] * inv
    ac = col[:, None] * inv
    zeros = jnp.zeros((n, 128 - QK_ROPE), F32)
    cos = jnp.concatenate([jnp.cos(ar), jnp.cos(ar), jnp.cos(ac), jnp.cos(ac), zeros], axis=-1)
    sin = jnp.concatenate([-jnp.sin(ar), jnp.sin(ar), -jnp.sin(ac), jnp.sin(ac), zeros], axis=-1)
    return cos, sin


def _mla_weights(w_dq, q_norm_g, w_uq, w_dkv, kv_norm_g, w_ukv, w_o):
    swap = _rope_swap_perm()
    z64 = jnp.zeros((D_MODEL, 128 - QK_ROPE), F32)
    kr_cols = w_dkv[:, KV_LORA:]
    dkv = jnp.concatenate([w_dkv[:, :KV_LORA], kr_cols, z64, kr_cols[:, swap], z64], axis=-1)
    uq = w_uq.reshape(Q_LORA, N_HEADS, QK_NOPE + QK_ROPE)
    zq = jnp.zeros((Q_LORA, N_HEADS, 128 - QK_ROPE), F32)
    uq_main = jnp.concatenate([uq, zq], axis=-1).reshape(Q_LORA, N_HEADS * HEAD_W)
    uq_swapped = jnp.concatenate([uq[:, :, QK_NOPE:][:, :, swap], zq], axis=-1).reshape(Q_LORA, N_HEADS * 128)
    ukv = w_ukv.reshape(KV_LORA, N_HEADS, QK_NOPE + V_DIM)
    cos, sin = _rope_tables()
    return {
        "dkv": dkv.astype(BF16),
        "kv_g": kv_norm_g.reshape(1, KV_LORA),
        "dq": w_dq.astype(BF16),
        "q_g": q_norm_g.reshape(1, Q_LORA),
        "uq": uq_main.astype(BF16),
        "uq_swapped": uq_swapped.astype(BF16),
        "uk": ukv[:, :, :QK_NOPE].reshape(KV_LORA, N_HEADS * QK_NOPE).astype(BF16),
        "uv": ukv[:, :, QK_NOPE:].reshape(KV_LORA, N_HEADS * V_DIM).astype(BF16),
        "o": w_o.astype(BF16),
        "cos": cos,
        "sin": sin,
    }


def kernel(x_prompt, x_sample, cache_mla_ckv, cache_mla_krope, c, c_ctx, w_ada, b_ada, ln_g, ln_b, ffn_w1, ffn_w3, ffn_w2, cp_w_in, conv_w, conv_b, conv_norm_g, conv_norm_b, pool_w, pool_scale, cp_w_out, mla_w_dq, mla_q_norm_g, mla_w_uq, mla_w_dkv, mla_kv_norm_g, mla_w_ukv, mla_w_o):
    c_all = jnp.concatenate(
        [c_ctx[None, :], c, jnp.zeros((N_GROUPS - 1 - DEC_BATCH, D_MODEL), F32)], axis=0)
    ada = _ada(c_all, w_ada, b_ada).reshape(DEPTH, N_GROUPS, N_MOD, D_MODEL)

    x = (x_prompt.reshape(N_PROMPT, D_MODEL), x_sample.reshape(DEC_BATCH * DEC_SEQ, D_MODEL))
    ffn_f32 = (ffn_w1, ffn_w3, ffn_w2)
    w_bf16 = tuple(wf[0, 0].astype(BF16) for wf in ffn_f32)
    new_ckv, new_krope = [], []
    for i in range(DEPTH):
        mod = ada[i]
        j = i // 2
        x, w_bf16 = _ffn(x, mod, ln_g[i, 0], ln_b[i, 0], *w_bf16, ffn_f32, (i, 1), k0=0)
        if i % 2 == 0:
            cp_args = (x, mod, ln_g[i, 1], ln_b[i, 1], cp_w_in[j].astype(BF16), conv_w[j], conv_b[j],
                       conv_norm_g[j], conv_norm_b[j], pool_w[j].astype(BF16), pool_scale[j],
                       cp_w_out[j].astype(BF16))
            y = _conv_pool_mixer(*cp_args, None, latent=False)
            x = _conv_pool_mixer(*cp_args, y, latent=True)
        else:
            w = _mla_weights(mla_w_dq[j], mla_q_norm_g[j], mla_w_uq[j], mla_w_dkv[j],
                             mla_kv_norm_g[j], mla_w_ukv[j], mla_w_o[j])
            q_p, k_p, v_p, ckv_p, kr_p = _mla_proj(x, mod, w, latent=False)
            q_s, k_s, v_s = _mla_proj(x, mod, w, latent=True)
            new_ckv.append(ckv_p.reshape(BATCH, SEQ, KV_LORA))
            new_krope.append(kr_p.reshape(BATCH, SEQ, QK_ROPE))
            kr_cache = jnp.pad(cache_mla_krope[:, j].reshape(DEC_BATCH * PAST_LEN, QK_ROPE),
                               ((0, 0), (0, 128 - QK_ROPE)))
            cache_kv = _cache_expand(cache_mla_ckv[:, j].reshape(DEC_BATCH * PAST_LEN, KV_LORA), kr_cache, w)
            attn_args = (x, mod, ln_g[i, 1], ln_b[i, 1], w["o"])
            y = _attention(*attn_args, q_p, k_p, v_p, None, None, latent=False)
            x = _attention(*attn_args, q_s, k_s, v_s, cache_kv, y, latent=True)
        last = i == DEPTH - 1
        x, w_bf16 = _ffn(x, mod, ln_g[i, 2], ln_b[i, 2], *w_bf16,
                         None if last else ffn_f32, None if last else (i + 1, 0),
                         k0=6, split_out=last)

    y_prompt = x[0].reshape(BATCH, SEQ, D_MODEL)
    y_sample = x[1].reshape(DEC_BATCH, DEC_SEQ, D_MODEL)
    return (y_prompt, y_sample, jnp.stack(new_ckv, axis=1), jnp.stack(new_krope, axis=1))
```

```python
import functools

import jax
import jax.numpy as jnp
import numpy as np
from jax import lax
from jax.experimental import pallas as pl
from jax.experimental.pallas import tpu as pltpu

F32 = jnp.float32
BF16 = jnp.bfloat16

D_MODEL = 1024
BATCH = 32
SEQ = 256
DEPTH = 2
DEC_BATCH = 4
DEC_SEQ = 1024
PAST_LEN = 512
GRID_W = 64
N_MOD = 9
D_FF = 2816
D_CONV = 512
CONV_WIDTH = 31
D_POOL = 512
POOL_WINDOWS = (2, 4, 8, 16)
POOL_GROUP = 128
N_HEADS = 8
QK_NOPE = 128
QK_ROPE = 64
V_DIM = 128
KV_LORA = 256
Q_LORA = 384
ROPE_AXIS_PAIRS = 16
ROPE_BASE = 10000.0
ALPHA = (2 * DEPTH) ** 0.25
LN_EPS = 1e-5
RMS_EPS = 1e-6

N_PROMPT = BATCH * SEQ
N_TOKENS = N_PROMPT + DEC_BATCH * DEC_SEQ
N_GROUPS = 8
HEAD_W = 256
CONV_PAD = 16
CONV_STRIDE = 4
CONV_ROWS = 8 * CONV_STRIDE

VMEM_LIMIT_BYTES = 56 * 1024 * 1024

FFN_TILE = 512
MIX_TILE = 1024
MIX_SEG = 256
SEG_ROWS = MIX_SEG + 2 * CONV_PAD
PROJ_TILE = 512
ATTN_Q_TILE = 256


def _layer_norm(z, g, b):
    mu = jnp.mean(z, axis=-1, keepdims=True)
    zc = z - mu
    var = jnp.mean(zc * zc, axis=-1, keepdims=True)
    return zc * lax.rsqrt(var + LN_EPS) * g + b


def _rms_norm(z, g):
    return z * lax.rsqrt(jnp.mean(z * z, axis=-1, keepdims=True) + RMS_EPS) * g


def _silu(z):
    return z * jax.nn.sigmoid(z)


def _dot(a, b):
    return jnp.dot(a, b, preferred_element_type=F32)


def _dot_nt(a, b):
    return lax.dot_general(a, b, (((1,), (1,)), ((), ())), preferred_element_type=F32)


def _const_spec(shape):
    nd = len(shape)
    return pl.BlockSpec(shape, lambda *_: (0,) * nd, pipeline_mode=pl.Buffered(1))


def _group_of_tile(t, tile):
    n_p = N_PROMPT // tile
    per_seq = DEC_SEQ // tile
    return jnp.where(t < n_p, 0, 1 + (t - n_p) // per_seq)


def _params(n_axes=1):
    return pltpu.CompilerParams(
        dimension_semantics=("arbitrary",) * n_axes,
        vmem_limit_bytes=VMEM_LIMIT_BYTES,
    )


def _ada_body(c_ref, w_ref, b_ref, o_ref):
    s = _silu(c_ref[...]).astype(BF16)
    o_ref[...] = _dot(s, w_ref[...].astype(BF16)) + b_ref[...]


def _ada(c_all, w_ada, b_ada):
    tn = D_MODEL
    return pl.pallas_call(
        _ada_body,
        out_shape=jax.ShapeDtypeStruct((DEPTH, N_GROUPS, N_MOD * D_MODEL), F32),
        grid=(DEPTH, N_MOD),
        in_specs=[
            pl.BlockSpec((N_GROUPS, D_MODEL), lambda l, k: (0, 0)),
            pl.BlockSpec((None, D_MODEL, tn), lambda l, k: (l, 0, k)),
            pl.BlockSpec((None, 1, tn), lambda l, k: (l, 0, k)),
        ],
        out_specs=pl.BlockSpec((None, N_GROUPS, tn), lambda l, k: (l, 0, k)),
        compiler_params=_params(2),
        name="ada",
    )(c_all, w_ada, b_ada.reshape(DEPTH, 1, N_MOD * D_MODEL))


def _ffn_body(*refs, k0, split_in, split_out, cast_next):
    refs = list(refs)
    n_x = 2 if split_in else 1
    x_refs, refs = refs[:n_x], refs[n_x:]
    (mod_ref, g_ref, b_ref, w1_ref, w3_ref, w2_ref), refs = refs[:6], refs[6:]
    if cast_next:
        next_f32, refs = refs[:3], refs[3:]
    n_o = 2 if split_out else 1
    o_refs, refs = refs[:n_o], refs[n_o:]
    t = pl.program_id(0)
    is_prompt = t < N_PROMPT // FFN_TILE
    if cast_next:
        for src, dst in zip(next_f32, refs):
            dst[...] = src[...].astype(BF16)
    if split_in:
        x = jnp.where(is_prompt, x_refs[0][...], x_refs[1][...])
    else:
        x = x_refs[0][...]
    shift = mod_ref[k0:k0 + 1, :]
    scale = mod_ref[k0 + 1:k0 + 2, :]
    gate = mod_ref[k0 + 2:k0 + 3, :]
    h = (x * (1.0 + scale) + shift).astype(BF16)
    a = (_silu(_dot(h, w1_ref[...])) * _dot(h, w3_ref[...])).astype(BF16)
    y = _dot(a, w2_ref[...])
    out = _layer_norm(ALPHA * x + (0.5 * gate) * y, g_ref[...], b_ref[...])
    if split_out:
        @pl.when(is_prompt)
        def _():
            o_refs[0][...] = out

        @pl.when(jnp.logical_not(is_prompt))
        def _():
            o_refs[1][...] = out
    else:
        o_refs[0][...] = out


def _slab_rows(n_rows, n_steps):
    n_slabs = max(d for d in range(1, n_steps + 1) if n_rows % d == 0 and (n_rows // d) % 16 == 0)
    return n_rows // n_slabs, n_slabs


def _ffn(xs, mod, ln_g, ln_b, w1, w3, w2, next_w=None, next_idx=None, *, k0, split_out=False):
    tm = FFN_TILE
    n_steps = N_TOKENS // tm
    n_p = N_PROMPT // tm
    split_in = isinstance(xs, (tuple, list))
    xs = list(xs) if split_in else [xs]
    prompt_spec = pl.BlockSpec((tm, D_MODEL), lambda t: (jnp.minimum(t, n_p - 1), 0))
    latent_spec = pl.BlockSpec((tm, D_MODEL), lambda t: (jnp.maximum(t - n_p, 0), 0))
    merged_spec = pl.BlockSpec((tm, D_MODEL), lambda t: (t, 0))
    in_specs = ([prompt_spec, latent_spec] if split_in else [merged_spec]) + [
        pl.BlockSpec((None, N_MOD, D_MODEL), lambda t: (_group_of_tile(t, tm), 0, 0)),
        _const_spec((1, D_MODEL)),
        _const_spec((1, D_MODEL)),
        _const_spec((D_MODEL, D_FF)),
        _const_spec((D_MODEL, D_FF)),
        _const_spec((D_FF, D_MODEL)),
    ]
    args = xs + [mod, ln_g.reshape(1, D_MODEL), ln_b.reshape(1, D_MODEL), w1, w3, w2]
    if split_out:
        out_shape = [jax.ShapeDtypeStruct((N_PROMPT, D_MODEL), F32),
                     jax.ShapeDtypeStruct((N_TOKENS - N_PROMPT, D_MODEL), F32)]
        out_specs = [prompt_spec, latent_spec]
    else:
        out_shape = [jax.ShapeDtypeStruct((N_TOKENS, D_MODEL), F32)]
        out_specs = [merged_spec]
    if next_w is not None:
        li, lj = next_idx
        for wn in next_w:
            n_rows, n_cols = wn.shape[2:]
            rows, n_slabs = _slab_rows(n_rows, n_steps)
            in_specs.append(pl.BlockSpec(
                (None, None, rows, n_cols),
                lambda t, n_slabs=n_slabs: (li, lj, jnp.minimum(t, n_slabs - 1), 0)))
            args.append(wn)
            out_shape.append(jax.ShapeDtypeStruct((n_rows, n_cols), BF16))
            out_specs.append(pl.BlockSpec(
                (rows, n_cols), lambda t, n_slabs=n_slabs: (jnp.minimum(t, n_slabs - 1), 0)))
    outs = pl.pallas_call(
        functools.partial(_ffn_body, k0=k0, split_in=split_in, split_out=split_out,
                          cast_next=next_w is not None),
        out_shape=out_shape,
        grid=(n_steps,),
        in_specs=in_specs,
        out_specs=out_specs,
        compiler_params=_params(),
        name="ffn",
    )(*args)
    n_tok = 2 if split_out else 1
    tok = tuple(outs[:n_tok]) if split_out else outs[0]
    return tok, tuple(outs[n_tok:])


def _conv_pool_chunk(i, carry, *, pad_ref, conv_ref, pool_ref, cw_ref, cb_ref, is_prompt):
    chunks_per_seg = MIX_SEG // CONV_ROWS
    seg = i // chunks_per_seg
    r0 = (i % chunks_per_seg) * CONV_ROWS
    p0 = seg * SEG_ROWS + r0 + CONV_PAD
    o0 = i * CONV_ROWS

    def rows(start):
        return pl.ds(start, 8, stride=CONV_STRIDE)

    half = CONV_WIDTH // 2
    for c in range(D_CONV // 128):
        lanes = slice(c * 128, (c + 1) * 128)
        acc = [None] * CONV_STRIDE
        for off in range(-half, half + CONV_STRIDE):
            tap = pad_ref[c, rows(p0 + off), :]
            for rho in range(CONV_STRIDE):
                k = off - rho + half
                if 0 <= k < CONV_WIDTH:
                    term = cw_ref[k:k + 1, lanes] * tap
                    acc[rho] = term if acc[rho] is None else acc[rho] + term
        for rho in range(CONV_STRIDE):
            conv_ref[c, rows(o0 + rho), :] = acc[rho] + cb_ref[:, lanes]
    seq_len = jnp.where(is_prompt, SEQ, DEC_SEQ)
    t0 = jnp.where(is_prompt, r0, o0) + CONV_STRIDE * lax.broadcasted_iota(jnp.int32, (8, 128), 0)
    for gi, w in enumerate(POOL_WINDOWS):
        left = w // 2
        right = w - 1 - left
        slab = D_CONV // 128 + gi
        total = [None] * CONV_STRIDE
        for off in range(-left, right + CONV_STRIDE):
            tap = pad_ref[slab, rows(p0 + off), :]
            for rho in range(CONV_STRIDE):
                if -left <= off - rho <= right:
                    total[rho] = tap if total[rho] is None else total[rho] + tap
        for rho in range(CONV_STRIDE):
            t = t0 + rho
            cnt = (jnp.minimum(t + right, seq_len - 1) - jnp.maximum(t - left, 0) + 1).astype(F32)
            tok = pad_ref[slab, rows(p0 + rho), :]
            pool_ref[gi, rows(o0 + rho), :] = total[rho] / cnt - tok
    return carry


def _cp_body(x_ref, mod_ref, g_ref, b_ref, win_ref, cw_ref, cb_ref, cng_ref, cnb_ref,
             pw_ref, ps_ref, wout_ref, o_ref, pad_ref, conv_ref, pool_ref):
    is_prompt = pl.program_id(0) < N_PROMPT // MIX_TILE
    x = x_ref[...]
    shift = mod_ref[3:4, :]
    scale = mod_ref[4:5, :]
    gate = mod_ref[5:6, :]
    h = (x * (1.0 + scale) + shift).astype(BF16)
    proj = _dot(h, win_ref[...])
    glu = proj[:, :D_CONV] * jax.nn.sigmoid(proj[:, D_CONV:2 * D_CONV])
    hp = proj[:, 2 * D_CONV:]
    n_seg = MIX_TILE // MIX_SEG
    zeros = jnp.zeros((CONV_PAD, 128), F32)
    for slab in range((D_CONV + D_POOL) // 128):
        part = (glu if slab < D_CONV // 128 else hp)[:, (slab % (D_CONV // 128)) * 128:][:, :128]
        for s in range(n_seg):
            lo = s * MIX_SEG
            hi = lo + MIX_SEG
            base = s * SEG_ROWS
            pad_ref[slab, base:base + CONV_PAD, :] = (
                zeros if s == 0 else jnp.where(is_prompt, zeros, part[lo - CONV_PAD:lo]))
            pad_ref[slab, base + CONV_PAD:base + CONV_PAD + MIX_SEG, :] = part[lo:hi]
            pad_ref[slab, base + CONV_PAD + MIX_SEG:base + SEG_ROWS, :] = (
                zeros if s == n_seg - 1 else jnp.where(is_prompt, zeros, part[hi:hi + CONV_PAD]))
    lax.fori_loop(
        0, MIX_TILE // CONV_ROWS,
        functools.partial(_conv_pool_chunk, pad_ref=pad_ref, conv_ref=conv_ref, pool_ref=pool_ref,
                          cw_ref=cw_ref, cb_ref=cb_ref, is_prompt=is_prompt),
        0)
    conv = jnp.concatenate([conv_ref[c] for c in range(D_CONV // 128)], axis=-1)
    a = _silu(_layer_norm(conv, cng_ref[...], cnb_ref[...]))
    pooled = [_dot(pool_ref[gi].astype(BF16), pw_ref[gi]) for gi in range(len(POOL_WINDOWS))]
    bmix = jnp.concatenate(pooled, axis=-1) * ps_ref[...]
    cat = jnp.concatenate([a, bmix], axis=-1).astype(BF16)
    y = _dot(cat, wout_ref[...])
    o_ref[...] = _layer_norm(ALPHA * x + gate * y, g_ref[...], b_ref[...])


def _conv_pool_mixer(x, mod, ln_g, ln_b, w_in, conv_w, conv_b, cn_g, cn_b, pool_w, pool_scale, w_out):
    tm = MIX_TILE
    assert tm == DEC_SEQ and MIX_SEG == SEQ and N_PROMPT % tm == 0
    row_spec = pl.BlockSpec((tm, D_MODEL), lambda t: (t, 0))
    in_specs = [
        row_spec,
        pl.BlockSpec((None, N_MOD, D_MODEL), lambda t: (_group_of_tile(t, tm), 0, 0)),
        _const_spec((1, D_MODEL)),
        _const_spec((1, D_MODEL)),
        _const_spec((D_MODEL, 2 * D_CONV + D_POOL)),
        _const_spec((CONV_WIDTH, D_CONV)),
        _const_spec((1, D_CONV)),
        _const_spec((1, D_CONV)),
        _const_spec((1, D_CONV)),
        _const_spec((len(POOL_WINDOWS), POOL_GROUP, POOL_GROUP)),
        _const_spec((1, D_POOL)),
        _const_spec((D_CONV + D_POOL, D_MODEL)),
    ]
    args = [x, mod, ln_g.reshape(1, D_MODEL), ln_b.reshape(1, D_MODEL), w_in, conv_w,
            conv_b.reshape(1, D_CONV), cn_g.reshape(1, D_CONV), cn_b.reshape(1, D_CONV),
            pool_w, pool_scale.reshape(1, D_POOL), w_out]
    return pl.pallas_call(
        _cp_body,
        out_shape=jax.ShapeDtypeStruct((N_TOKENS, D_MODEL), F32),
        grid=(N_TOKENS // tm,),
        in_specs=in_specs,
        out_specs=row_spec,
        scratch_shapes=[
            pltpu.VMEM(((D_CONV + D_POOL) // 128, tm // MIX_SEG * SEG_ROWS, 128), F32),
            pltpu.VMEM((D_CONV // 128, tm, 128), F32),
            pltpu.VMEM((D_POOL // 128, tm, 128), F32),
        ],
        compiler_params=_params(),
        name="conv_pool",
    )(*args)


def _mla_proj_body(*refs, rope, emit_cache):
    (x_ref, mod_ref, wdkv_ref, kvg_ref, wdq_ref, qg_ref, wuq_ref, wuk_ref, wuv_ref), rest = refs[:9], refs[9:]
    if rope:
        (wuqs_ref, cos_ref, sin_ref), rest = rest[:3], rest[3:]
    q_ref, k_ref, v_ref = rest[:3]
    x = x_ref[...]
    h = (x * (1.0 + mod_ref[4:5, :]) + mod_ref[3:4, :]).astype(BF16)
    kv = _dot(h, wdkv_ref[...])
    ckv = _rms_norm(kv[:, :KV_LORA], kvg_ref[...])
    kr = kv[:, KV_LORA:KV_LORA + 128]
    qd = _rms_norm(_dot(h, wdq_ref[...]), qg_ref[...]).astype(BF16)
    q = _dot(qd, wuq_ref[...])
    if emit_cache:
        ckv_ref, kr_ref = rest[3:5]
        ckv_ref[...] = ckv
        kr_ref[...] = kr[:, :QK_ROPE]
    if rope:
        cos = cos_ref[...]
        sin = sin_ref[...]
        kr = kr * cos + kv[:, KV_LORA + 128:KV_LORA + 256] * sin
        q_swapped = _dot(qd, wuqs_ref[...])
    ckv16 = ckv.astype(BF16)
    kn = _dot(ckv16, wuk_ref[...])
    v_ref[...] = _dot(ckv16, wuv_ref[...]).astype(BF16)
    kr16 = kr.astype(BF16)
    for hd in range(N_HEADS):
        lo = hd * HEAD_W
        q_ref[:, lo:lo + 128] = q[:, lo:lo + 128].astype(BF16)
        q_rope = q[:, lo + 128:lo + 256]
        if rope:
            q_rope = q_rope * cos + q_swapped[:, hd * 128:(hd + 1) * 128] * sin
        q_ref[:, lo + 128:lo + 256] = q_rope.astype(BF16)
        k_ref[:, lo:lo + 128] = kn[:, hd * 128:(hd + 1) * 128].astype(BF16)
        k_ref[:, lo + 128:lo + 256] = kr16


def _mla_proj(x, mod, w, *, latent):
    tm = PROJ_TILE
    first = N_PROMPT // tm if latent else 0
    n_rows = DEC_BATCH * DEC_SEQ if latent else N_PROMPT
    in_specs = [
        pl.BlockSpec((tm, D_MODEL), lambda t: (first + t, 0)),
        pl.BlockSpec((None, N_MOD, D_MODEL), lambda t: (_group_of_tile(first + t, tm), 0, 0)),
        _const_spec((D_MODEL, KV_LORA + 256)),
        _const_spec((1, KV_LORA)),
        _const_spec((D_MODEL, Q_LORA)),
        _const_spec((1, Q_LORA)),
        _const_spec((Q_LORA, N_HEADS * HEAD_W)),
        _const_spec((KV_LORA, N_HEADS * QK_NOPE)),
        _const_spec((KV_LORA, N_HEADS * V_DIM)),
    ]
    args = [x, mod, w["dkv"], w["kv_g"], w["dq"], w["q_g"], w["uq"], w["uk"], w["uv"]]
    out_shape = [
        jax.ShapeDtypeStruct((n_rows, N_HEADS * HEAD_W), BF16),
        jax.ShapeDtypeStruct((n_rows, N_HEADS * HEAD_W), BF16),
        jax.ShapeDtypeStruct((n_rows, N_HEADS * V_DIM), BF16),
    ]
    out_specs = [
        pl.BlockSpec((tm, N_HEADS * HEAD_W), lambda t: (t, 0)),
        pl.BlockSpec((tm, N_HEADS * HEAD_W), lambda t: (t, 0)),
        pl.BlockSpec((tm, N_HEADS * V_DIM), lambda t: (t, 0)),
    ]
    if latent:
        per_seq = DEC_SEQ // tm
        in_specs += [
            _const_spec((Q_LORA, N_HEADS * 128)),
            pl.BlockSpec((tm, 128), lambda t: (t % per_seq, 0)),
            pl.BlockSpec((tm, 128), lambda t: (t % per_seq, 0)),
        ]
        args += [w["uq_swapped"], w["cos"], w["sin"]]
    else:
        out_shape += [
            jax.ShapeDtypeStruct((n_rows, KV_LORA), F32),
            jax.ShapeDtypeStruct((n_rows, QK_ROPE), F32),
        ]
        out_specs += [
            pl.BlockSpec((tm, KV_LORA), lambda t: (t, 0)),
            pl.BlockSpec((tm, QK_ROPE), lambda t: (t, 0)),
        ]
    return pl.pallas_call(
        functools.partial(_mla_proj_body, rope=latent, emit_cache=not latent),
        out_shape=out_shape,
        grid=(n_rows // tm,),
        in_specs=in_specs,
        out_specs=out_specs,
        compiler_params=_params(),
        name="mla_proj_latent" if latent else "mla_proj_prompt",
    )(*args)


def _cache_expand_body(ckv_ref, kr_ref, wuk_ref, wuv_ref, k_ref, v_ref):
    ckv16 = ckv_ref[...].astype(BF16)
    kn = _dot(ckv16, wuk_ref[...])
    v_ref[...] = _dot(ckv16, wuv_ref[...]).astype(BF16)
    kr16 = kr_ref[...].astype(BF16)
    for hd in range(N_HEADS):
        lo = hd * HEAD_W
        k_ref[:, lo:lo + 128] = kn[:, hd * 128:(hd + 1) * 128].astype(BF16)
        k_ref[:, lo + 128:lo + 256] = kr16


def _cache_expand(ckv, kr_padded, w):
    n_rows = DEC_BATCH * PAST_LEN
    tm = PAST_LEN
    return pl.pallas_call(
        _cache_expand_body,
        out_shape=[
            jax.ShapeDtypeStruct((n_rows, N_HEADS * HEAD_W), BF16),
            jax.ShapeDtypeStruct((n_rows, N_HEADS * V_DIM), BF16),
        ],
        grid=(n_rows // tm,),
        in_specs=[
            pl.BlockSpec((tm, KV_LORA), lambda t: (t, 0)),
            pl.BlockSpec((tm, 128), lambda t: (t, 0)),
            _const_spec((KV_LORA, N_HEADS * QK_NOPE)),
            _const_spec((KV_LORA, N_HEADS * V_DIM)),
        ],
        out_specs=[
            pl.BlockSpec((tm, N_HEADS * HEAD_W), lambda t: (t, 0)),
            pl.BlockSpec((tm, N_HEADS * V_DIM), lambda t: (t, 0)),
        ],
        compiler_params=_params(),
        name="mla_cache_expand",
    )(ckv, kr_padded, w["uk"], w["uv"])


def _attn_body(*refs, with_cache):
    x_ref, mod_ref, g_ref, b_ref, wo_ref, q_ref, k_ref, v_ref = refs[:8]
    if with_cache:
        kc_ref, vc_ref = refs[8:10]
    o_ref, attn_ref = refs[-2:]
    sm_scale = (QK_NOPE + QK_ROPE) ** -0.5
    for hd in range(N_HEADS):
        ql = slice(hd * HEAD_W, (hd + 1) * HEAD_W)
        vl = slice(hd * V_DIM, (hd + 1) * V_DIM)
        qh = q_ref[:, ql]
        s1 = _dot_nt(qh, k_ref[:, ql]) * sm_scale
        m = jnp.max(s1, axis=-1, keepdims=True)
        if with_cache:
            s2 = _dot_nt(qh, kc_ref[:, ql]) * sm_scale
            m = jnp.maximum(m, jnp.max(s2, axis=-1, keepdims=True))
        e1 = jnp.exp(s1 - m)
        denom = jnp.sum(e1, axis=-1, keepdims=True)
        if with_cache:
            e2 = jnp.exp(s2 - m)
            denom = denom + jnp.sum(e2, axis=-1, keepdims=True)
        inv = 1.0 / denom
        out = _dot((e1 * inv).astype(BF16), v_ref[:, vl])
        if with_cache:
            out = out + _dot((e2 * inv).astype(BF16), vc_ref[:, vl])
        attn_ref[:, vl] = out.astype(BF16)
    x = x_ref[...]
    y = _dot(attn_ref[...], wo_ref[...])
    o_ref[...] = _layer_norm(ALPHA * x + mod_ref[5:6, :] * y, g_ref[...], b_ref[...])


def _attention(x, mod, ln_g, ln_b, w_o, q, k, v, cache_kv, prev_out, *, latent):
    tq = ATTN_Q_TILE
    seq_len = DEC_SEQ if latent else SEQ
    q_per_seq = seq_len // tq
    first = N_PROMPT // tq if latent else 0
    n_rows = DEC_BATCH * DEC_SEQ if latent else N_PROMPT
    row_spec = pl.BlockSpec((tq, D_MODEL), lambda t: (first + t, 0))
    in_specs = [
        row_spec,
        pl.BlockSpec((None, N_MOD, D_MODEL), lambda t: (_group_of_tile(first + t, tq), 0, 0)),
        _const_spec((1, D_MODEL)),
        _const_spec((1, D_MODEL)),
        _const_spec((N_HEADS * V_DIM, D_MODEL)),
        pl.BlockSpec((tq, N_HEADS * HEAD_W), lambda t: (t, 0)),
        pl.BlockSpec((seq_len, N_HEADS * HEAD_W), lambda t: (t // q_per_seq, 0)),
        pl.BlockSpec((seq_len, N_HEADS * V_DIM), lambda t: (t // q_per_seq, 0)),
    ]
    args = [x, mod, ln_g.reshape(1, D_MODEL), ln_b.reshape(1, D_MODEL), w_o, q, k, v]
    if latent:
        in_specs += [
            pl.BlockSpec((PAST_LEN, N_HEADS * HEAD_W), lambda t: (t // q_per_seq, 0)),
            pl.BlockSpec((PAST_LEN, N_HEADS * V_DIM), lambda t: (t // q_per_seq, 0)),
        ]
        args += list(cache_kv)
    body = functools.partial(_attn_body, with_cache=latent)
    aliases = {}
    if prev_out is not None:
        in_specs.append(pl.BlockSpec(memory_space=pl.ANY))
        args.append(prev_out)
        aliases = {len(args) - 1: 0}
        inner = body

        def body(*refs):
            n_in = len(args)
            inner(*refs[:n_in - 1], *refs[n_in:])

    return pl.pallas_call(
        body,
        out_shape=jax.ShapeDtypeStruct((N_TOKENS, D_MODEL), F32),
        grid=(n_rows // tq,),
        in_specs=in_specs,
        out_specs=row_spec,
        scratch_shapes=[pltpu.VMEM((tq, N_HEADS * V_DIM), BF16)],
        input_output_aliases=aliases,
        compiler_params=_params(),
        name="mla_attn_latent" if latent else "mla_attn_prompt",
    )(*args)


def _rope_swap_perm():
    idx = np.arange(QK_ROPE)
    return np.where((idx % 32) < ROPE_AXIS_PAIRS, idx + ROPE_AXIS_PAIRS, idx - ROPE_AXIS_PAIRS)


def _rope_tables():
    n = DEC_SEQ
    row = jnp.repeat(jnp.arange(n // GRID_W), GRID_W)
    col = jnp.tile(jnp.arange(GRID_W), n // GRID_W)
    inv = ROPE_BASE ** (-jnp.arange(ROPE_AXIS_PAIRS, dtype=F32) / ROPE_AXIS_PAIRS)
    ar = row[:, None] * inv
    ac = col[:, None] * inv
    zeros = jnp.zeros((n, 128 - QK_ROPE), F32)
    cos = jnp.concatenate([jnp.cos(ar), jnp.cos(ar), jnp.cos(ac), jnp.cos(ac), zeros], axis=-1)
    sin = jnp.concatenate([-jnp.sin(ar), jnp.sin(ar), -jnp.sin(ac), jnp.sin(ac), zeros], axis=-1)
    return cos, sin


def _mla_weights(w_dq, q_norm_g, w_uq, w_dkv, kv_norm_g, w_ukv, w_o):
    swap = _rope_swap_perm()
    z64 = jnp.zeros((D_MODEL, 128 - QK_ROPE), F32)
    kr_cols = w_dkv[:, KV_LORA:]
    dkv = jnp.concatenate([w_dkv[:, :KV_LORA], kr_cols, z64, kr_cols[:, swap], z64], axis=-1)
    uq = w_uq.reshape(Q_LORA, N_HEADS, QK_NOPE + QK_ROPE)
    zq = jnp.zeros((Q_LORA, N_HEADS, 128 - QK_ROPE), F32)
    uq_main = jnp.concatenate([uq, zq], axis=-1).reshape(Q_LORA, N_HEADS * HEAD_W)
    uq_swapped = jnp.concatenate([uq[:, :, QK_NOPE:][:, :, swap], zq], axis=-1).reshape(Q_LORA, N_HEADS * 128)
    ukv = w_ukv.reshape(KV_LORA, N_HEADS, QK_NOPE + V_DIM)
    cos, sin = _rope_tables()
    return {
        "dkv": dkv.astype(BF16),
        "kv_g": kv_norm_g.reshape(1, KV_LORA),
        "dq": w_dq.astype(BF16),
        "q_g": q_norm_g.reshape(1, Q_LORA),
        "uq": uq_main.astype(BF16),
        "uq_swapped": uq_swapped.astype(BF16),
        "uk": ukv[:, :, :QK_NOPE].reshape(KV_LORA, N_HEADS * QK_NOPE).astype(BF16),
        "uv": ukv[:, :, QK_NOPE:].reshape(KV_LORA, N_HEADS * V_DIM).astype(BF16),
        "o": w_o.astype(BF16),
        "cos": cos,
        "sin": sin,
    }


def kernel(x_prompt, x_sample, cache_mla_ckv, cache_mla_krope, c, c_ctx, w_ada, b_ada, ln_g, ln_b, ffn_w1, ffn_w3, ffn_w2, cp_w_in, conv_w, conv_b, conv_norm_g, conv_norm_b, pool_w, pool_scale, cp_w_out, mla_w_dq, mla_q_norm_g, mla_w_uq, mla_w_dkv, mla_kv_norm_g, mla_w_ukv, mla_w_o):
    c_all = jnp.concatenate(
        [c_ctx[None, :], c, jnp.zeros((N_GROUPS - 1 - DEC_BATCH, D_MODEL), F32)], axis=0)
    ada = _ada(c_all, w_ada, b_ada).reshape(DEPTH, N_GROUPS, N_MOD, D_MODEL)

    x = (x_prompt.reshape(N_PROMPT, D_MODEL), x_sample.reshape(DEC_BATCH * DEC_SEQ, D_MODEL))
    ffn_f32 = (ffn_w1, ffn_w3, ffn_w2)
    w_bf16 = tuple(wf[0, 0].astype(BF16) for wf in ffn_f32)
    new_ckv, new_krope = [], []
    for i in range(DEPTH):
        mod = ada[i]
        j = i // 2
        x, w_bf16 = _ffn(x, mod, ln_g[i, 0], ln_b[i, 0], *w_bf16, ffn_f32, (i, 1), k0=0)
        if i % 2 == 0:
            x = _conv_pool_mixer(x, mod, ln_g[i, 1], ln_b[i, 1], cp_w_in[j].astype(BF16), conv_w[j],
                                 conv_b[j], conv_norm_g[j], conv_norm_b[j], pool_w[j].astype(BF16),
                                 pool_scale[j], cp_w_out[j].astype(BF16))
        else:
            w = _mla_weights(mla_w_dq[j], mla_q_norm_g[j], mla_w_uq[j], mla_w_dkv[j],
                             mla_kv_norm_g[j], mla_w_ukv[j], mla_w_o[j])
            q_p, k_p, v_p, ckv_p, kr_p = _mla_proj(x, mod, w, latent=False)
            q_s, k_s, v_s = _mla_proj(x, mod, w, latent=True)
            new_ckv.append(ckv_p.reshape(BATCH, SEQ, KV_LORA))
            new_krope.append(kr_p.reshape(BATCH, SEQ, QK_ROPE))
            kr_cache = jnp.pad(cache_mla_krope[:, j].reshape(DEC_BATCH * PAST_LEN, QK_ROPE),
                               ((0, 0), (0, 128 - QK_ROPE)))
            cache_kv = _cache_expand(cache_mla_ckv[:, j].reshape(DEC_BATCH * PAST_LEN, KV_LORA), kr_cache, w)
            attn_args = (x, mod, ln_g[i, 1], ln_b[i, 1], w["o"])
            y = _attention(*attn_args, q_p, k_p, v_p, None, None, latent=False)
            x = _attention(*attn_args, q_s, k_s, v_s, cache_kv, y, latent=True)
        last = i == DEPTH - 1
        x, w_bf16 = _ffn(x, mod, ln_g[i, 2], ln_b[i, 2], *w_bf16,
                         None if last else ffn_f32, None if last else (i + 1, 0),
                         k0=6, split_out=last)

    y_prompt = x[0].reshape(BATCH, SEQ, D_MODEL)
    y_sample = x[1].reshape(DEC_BATCH, DEC_SEQ, D_MODEL)
    return (y_prompt, y_sample, jnp.stack(new_ckv, axis=1), jnp.stack(new_krope, axis=1))
```

```python
import functools

import jax
import jax.numpy as jnp
import numpy as np
from jax import lax
from jax.experimental import pallas as pl
from jax.experimental.pallas import tpu as pltpu

F32 = jnp.float32
BF16 = jnp.bfloat16

D_MODEL = 1024
BATCH = 32
SEQ = 256
DEPTH = 2
DEC_BATCH = 4
DEC_SEQ = 1024
PAST_LEN = 512
GRID_W = 64
N_MOD = 9
D_FF = 2816
D_CONV = 512
CONV_WIDTH = 31
D_POOL = 512
POOL_WINDOWS = (2, 4, 8, 16)
POOL_GROUP = 128
N_HEADS = 8
QK_NOPE = 128
QK_ROPE = 64
V_DIM = 128
KV_LORA = 256
Q_LORA = 384
ROPE_AXIS_PAIRS = 16
ROPE_BASE = 10000.0
ALPHA = (2 * DEPTH) ** 0.25
LN_EPS = 1e-5
RMS_EPS = 1e-6

N_PROMPT = BATCH * SEQ
N_TOKENS = N_PROMPT + DEC_BATCH * DEC_SEQ
N_GROUPS = 8
HEAD_W = 256
CONV_PAD = 16
CONV_STRIDE = 4
CONV_ROWS = 8 * CONV_STRIDE
CONV_UNROLL = 4

VMEM_LIMIT_BYTES = 60 * 1024 * 1024

FFN_TILE = 1024
FFN_SUB = 256
MIX_TILE = 1024
MIX_SEG = 256
SEG_ROWS = MIX_SEG + 2 * CONV_PAD
PROJ_TILE = 512
ATTN_Q_TILE = 256


def _layer_norm(z, g, b):
    mu = jnp.mean(z, axis=-1, keepdims=True)
    zc = z - mu
    var = jnp.mean(zc * zc, axis=-1, keepdims=True)
    return zc * lax.rsqrt(var + LN_EPS) * g + b


def _rms_norm(z, g):
    return z * lax.rsqrt(jnp.mean(z * z, axis=-1, keepdims=True) + RMS_EPS) * g


def _silu(z):
    return z * jax.nn.sigmoid(z)


def _dot(a, b):
    return jnp.dot(a, b, preferred_element_type=F32)


def _dot_nt(a, b):
    return lax.dot_general(a, b, (((1,), (1,)), ((), ())), preferred_element_type=F32)


def _const_spec(shape):
    nd = len(shape)
    return pl.BlockSpec(shape, lambda *_: (0,) * nd, pipeline_mode=pl.Buffered(1))


def _group_of_tile(t, tile):
    n_p = N_PROMPT // tile
    per_seq = DEC_SEQ // tile
    return jnp.where(t < n_p, 0, 1 + (t - n_p) // per_seq)


def _params(n_axes=1):
    return pltpu.CompilerParams(
        dimension_semantics=("arbitrary",) * n_axes,
        vmem_limit_bytes=VMEM_LIMIT_BYTES,
    )


def _ada_body(c_ref, w_ref, b_ref, o_ref):
    s = _silu(c_ref[...]).astype(BF16)
    o_ref[...] = _dot(s, w_ref[...].astype(BF16)) + b_ref[...]


def _ada(c_all, w_ada, b_ada):
    tn = D_MODEL
    return pl.pallas_call(
        _ada_body,
        out_shape=jax.ShapeDtypeStruct((DEPTH, N_GROUPS, N_MOD * D_MODEL), F32),
        grid=(DEPTH, N_MOD),
        in_specs=[
            pl.BlockSpec((N_GROUPS, D_MODEL), lambda l, k: (0, 0)),
            pl.BlockSpec((None, D_MODEL, tn), lambda l, k: (l, 0, k)),
            pl.BlockSpec((None, 1, tn), lambda l, k: (l, 0, k)),
        ],
        out_specs=pl.BlockSpec((None, N_GROUPS, tn), lambda l, k: (l, 0, k)),
        compiler_params=_params(2),
        name="ada",
    )(c_all, w_ada, b_ada.reshape(DEPTH, 1, N_MOD * D_MODEL))


def _ffn_body(*refs, k0, split_in, split_out, cast_next):
    refs = list(refs)
    n_x = 2 if split_in else 1
    x_refs, refs = refs[:n_x], refs[n_x:]
    (mod_ref, g_ref, b_ref, w1_ref, w3_ref, w2_ref), refs = refs[:6], refs[6:]
    if cast_next:
        next_f32, refs = refs[:3], refs[3:]
    n_o = 2 if split_out else 1
    o_refs, refs = refs[:n_o], refs[n_o:]
    t = pl.program_id(0)
    is_prompt = t < N_PROMPT // FFN_TILE
    if cast_next:
        for src, dst in zip(next_f32, refs):
            dst[...] = src[...].astype(BF16)
    shift = mod_ref[k0:k0 + 1, :]
    scale = mod_ref[k0 + 1:k0 + 2, :]
    gate = mod_ref[k0 + 2:k0 + 3, :]
    outs = []
    for s in range(FFN_TILE // FFN_SUB):
        rows = slice(s * FFN_SUB, (s + 1) * FFN_SUB)
        if split_in:
            x = jnp.where(is_prompt, x_refs[0][rows, :], x_refs[1][rows, :])
        else:
            x = x_refs[0][rows, :]
        h = (x * (1.0 + scale) + shift).astype(BF16)
        a = (_silu(_dot(h, w1_ref[...])) * _dot(h, w3_ref[...])).astype(BF16)
        y = _dot(a, w2_ref[...])
        out = _layer_norm(ALPHA * x + (0.5 * gate) * y, g_ref[...], b_ref[...])
        if split_out:
            outs.append((rows, out))
        else:
            o_refs[0][rows, :] = out
    if split_out:
        @pl.when(is_prompt)
        def _():
            for rows, out in outs:
                o_refs[0][rows, :] = out

        @pl.when(jnp.logical_not(is_prompt))
        def _():
            for rows, out in outs:
                o_refs[1][rows, :] = out


def _slab_rows(n_rows, n_steps):
    n_slabs = max(d for d in range(1, n_steps + 1) if n_rows % d == 0 and (n_rows // d) % 16 == 0)
    return n_rows // n_slabs, n_slabs


def _ffn(xs, mod, ln_g, ln_b, w1, w3, w2, next_w=None, next_idx=None, *, k0, split_out=False):
    tm = FFN_TILE
    n_steps = N_TOKENS // tm
    n_p = N_PROMPT // tm
    split_in = isinstance(xs, (tuple, list))
    xs = list(xs) if split_in else [xs]
    prompt_spec = pl.BlockSpec((tm, D_MODEL), lambda t: (jnp.minimum(t, n_p - 1), 0))
    latent_spec = pl.BlockSpec((tm, D_MODEL), lambda t: (jnp.maximum(t - n_p, 0), 0))
    merged_spec = pl.BlockSpec((tm, D_MODEL), lambda t: (t, 0))
    in_specs = ([prompt_spec, latent_spec] if split_in else [merged_spec]) + [
        pl.BlockSpec((None, N_MOD, D_MODEL), lambda t: (_group_of_tile(t, tm), 0, 0)),
        _const_spec((1, D_MODEL)),
        _const_spec((1, D_MODEL)),
        _const_spec((D_MODEL, D_FF)),
        _const_spec((D_MODEL, D_FF)),
        _const_spec((D_FF, D_MODEL)),
    ]
    args = xs + [mod, ln_g.reshape(1, D_MODEL), ln_b.reshape(1, D_MODEL), w1, w3, w2]
    if split_out:
        out_shape = [jax.ShapeDtypeStruct((N_PROMPT, D_MODEL), F32),
                     jax.ShapeDtypeStruct((N_TOKENS - N_PROMPT, D_MODEL), F32)]
        out_specs = [prompt_spec, latent_spec]
    else:
        out_shape = [jax.ShapeDtypeStruct((N_TOKENS, D_MODEL), F32)]
        out_specs = [merged_spec]
    if next_w is not None:
        li, lj = next_idx
        for wn in next_w:
            n_rows, n_cols = wn.shape[2:]
            rows, n_slabs = _slab_rows(n_rows, n_steps)
            in_specs.append(pl.BlockSpec(
                (None, None, rows, n_cols),
                lambda t, n_slabs=n_slabs: (li, lj, jnp.minimum(t, n_slabs - 1), 0)))
            args.append(wn)
            out_shape.append(jax.ShapeDtypeStruct((n_rows, n_cols), BF16))
            out_specs.append(pl.BlockSpec(
                (rows, n_cols), lambda t, n_slabs=n_slabs: (jnp.minimum(t, n_slabs - 1), 0)))
    outs = pl.pallas_call(
        functools.partial(_ffn_body, k0=k0, split_in=split_in, split_out=split_out,
                          cast_next=next_w is not None),
        out_shape=out_shape,
        grid=(n_steps,),
        in_specs=in_specs,
        out_specs=out_specs,
        compiler_params=_params(),
        name="ffn",
    )(*args)
    n_tok = 2 if split_out else 1
    tok = tuple(outs[:n_tok]) if split_out else outs[0]
    return tok, tuple(outs[n_tok:])


def _strided_rows(start):
    return pl.ds(start, 8, stride=CONV_STRIDE)


def _chunk_rows(j):
    chunks_per_seg = MIX_SEG // CONV_ROWS
    r0 = (j % chunks_per_seg) * CONV_ROWS
    return (j // chunks_per_seg) * SEG_ROWS + r0 + CONV_PAD, j * CONV_ROWS, r0


def _conv_chunk(j, carry, *, slab, taps_w, bias, pad_ref, conv_ref):
    p0, o0, _ = _chunk_rows(j)
    half = CONV_WIDTH // 2
    n_part = 2
    acc = [[None] * n_part for _ in range(CONV_STRIDE)]
    for off in range(-half, half + CONV_STRIDE):
        tap = pad_ref[slab, _strided_rows(p0 + off), :]
        for rho in range(CONV_STRIDE):
            k = off - rho + half
            if 0 <= k < CONV_WIDTH:
                term = taps_w[k] * tap
                part = acc[rho][k % n_part]
                acc[rho][k % n_part] = term if part is None else part + term
    for rho in range(CONV_STRIDE):
        conv_ref[slab, _strided_rows(o0 + rho), :] = (acc[rho][0] + acc[rho][1]) + bias
    return carry


def _pool_chunk(j, carry, *, pad_ref, pool_ref, is_prompt):
    p0, o0, r0 = _chunk_rows(j)
    seq_len = jnp.where(is_prompt, SEQ, DEC_SEQ)
    t0 = jnp.where(is_prompt, r0, o0) + CONV_STRIDE * lax.broadcasted_iota(jnp.int32, (8, 128), 0)
    for gi, w in enumerate(POOL_WINDOWS):
        left = w // 2
        right = w - 1 - left
        slab = D_CONV // 128 + gi
        total = [None] * CONV_STRIDE
        for off in range(-left, right + CONV_STRIDE):
            tap = pad_ref[slab, _strided_rows(p0 + off), :]
            for rho in range(CONV_STRIDE):
                if -left <= off - rho <= right:
                    total[rho] = tap if total[rho] is None else total[rho] + tap
        for rho in range(CONV_STRIDE):
            t = t0 + rho
            cnt = (jnp.minimum(t + right, seq_len - 1) - jnp.maximum(t - left, 0) + 1).astype(F32)
            tok = pad_ref[slab, _strided_rows(p0 + rho), :]
            pool_ref[gi, _strided_rows(o0 + rho), :] = total[rho] / cnt - tok
    return carry


def _cp_body(x_ref, mod_ref, g_ref, b_ref, win_ref, cw_ref, cb_ref, cng_ref, cnb_ref,
             pw_ref, ps_ref, wout_ref, o_ref, pad_ref, conv_ref, pool_ref):
    is_prompt = pl.program_id(0) < N_PROMPT // MIX_TILE
    x = x_ref[...]
    shift = mod_ref[3:4, :]
    scale = mod_ref[4:5, :]
    gate = mod_ref[5:6, :]
    h = (x * (1.0 + scale) + shift).astype(BF16)
    proj = _dot(h, win_ref[...])
    glu = proj[:, :D_CONV] * jax.nn.sigmoid(proj[:, D_CONV:2 * D_CONV])
    hp = proj[:, 2 * D_CONV:]
    n_seg = MIX_TILE // MIX_SEG
    zeros = jnp.zeros((CONV_PAD, 128), F32)
    for slab in range((D_CONV + D_POOL) // 128):
        part = (glu if slab < D_CONV // 128 else hp)[:, (slab % (D_CONV // 128)) * 128:][:, :128]
        for s in range(n_seg):
            lo = s * MIX_SEG
            hi = lo + MIX_SEG
            base = s * SEG_ROWS
            pad_ref[slab, base:base + CONV_PAD, :] = (
                zeros if s == 0 else jnp.where(is_prompt, zeros, part[lo - CONV_PAD:lo]))
            pad_ref[slab, base + CONV_PAD:base + CONV_PAD + MIX_SEG, :] = part[lo:hi]
            pad_ref[slab, base + CONV_PAD + MIX_SEG:base + SEG_ROWS, :] = (
                zeros if s == n_seg - 1 else jnp.where(is_prompt, zeros, part[hi:hi + CONV_PAD]))
    n_chunks = MIX_TILE // CONV_ROWS
    for slab in range(D_CONV // 128):
        lanes = slice(slab * 128, (slab + 1) * 128)
        taps_w = [jnp.broadcast_to(cw_ref[k:k + 1, lanes], (8, 128)) for k in range(CONV_WIDTH)]
        bias = jnp.broadcast_to(cb_ref[:, lanes], (8, 128))
        lax.fori_loop(0, n_chunks, functools.partial(
            _conv_chunk, slab=slab, taps_w=taps_w, bias=bias, pad_ref=pad_ref, conv_ref=conv_ref), 0,
            unroll=CONV_UNROLL)
    lax.fori_loop(0, n_chunks, functools.partial(
        _pool_chunk, pad_ref=pad_ref, pool_ref=pool_ref, is_prompt=is_prompt), 0,
        unroll=CONV_UNROLL)
    conv = jnp.concatenate([conv_ref[c] for c in range(D_CONV // 128)], axis=-1)
    a = _silu(_layer_norm(conv, cng_ref[...], cnb_ref[...]))
    pooled = [_dot(pool_ref[gi].astype(BF16), pw_ref[gi]) for gi in range(len(POOL_WINDOWS))]
    bmix = jnp.concatenate(pooled, axis=-1) * ps_ref[...]
    cat = jnp.concatenate([a, bmix], axis=-1).astype(BF16)
    y = _dot(cat, wout_ref[...])
    o_ref[...] = _layer_norm(ALPHA * x + gate * y, g_ref[...], b_ref[...])


def _conv_pool_mixer(x, mod, ln_g, ln_b, w_in, conv_w, conv_b, cn_g, cn_b, pool_w, pool_scale, w_out):
    tm = MIX_TILE
    assert tm == DEC_SEQ and MIX_SEG == SEQ and N_PROMPT % tm == 0
    row_spec = pl.BlockSpec((tm, D_MODEL), lambda t: (t, 0))
    in_specs = [
        row_spec,
        pl.BlockSpec((None, N_MOD, D_MODEL), lambda t: (_group_of_tile(t, tm), 0, 0)),
        _const_spec((1, D_MODEL)),
        _const_spec((1, D_MODEL)),
        _const_spec((D_MODEL, 2 * D_CONV + D_POOL)),
        _const_spec((CONV_WIDTH, D_CONV)),
        _const_spec((1, D_CONV)),
        _const_spec((1, D_CONV)),
        _const_spec((1, D_CONV)),
        _const_spec((len(POOL_WINDOWS), POOL_GROUP, POOL_GROUP)),
        _const_spec((1, D_POOL)),
        _const_spec((D_CONV + D_POOL, D_MODEL)),
    ]
    args = [x, mod, ln_g.reshape(1, D_MODEL), ln_b.reshape(1, D_MODEL), w_in, conv_w,
            conv_b.reshape(1, D_CONV), cn_g.reshape(1, D_CONV), cn_b.reshape(1, D_CONV),
            pool_w, pool_scale.reshape(1, D_POOL), w_out]
    return pl.pallas_call(
        _cp_body,
        out_shape=jax.ShapeDtypeStruct((N_TOKENS, D_MODEL), F32),
        grid=(N_TOKENS // tm,),
        in_specs=in_specs,
        out_specs=row_spec,
        scratch_shapes=[
            pltpu.VMEM(((D_CONV + D_POOL) // 128, tm // MIX_SEG * SEG_ROWS, 128), F32),
            pltpu.VMEM((D_CONV // 128, tm, 128), F32),
            pltpu.VMEM((D_POOL // 128, tm, 128), F32),
        ],
        compiler_params=_params(),
        name="conv_pool",
    )(*args)


def _mla_proj_body(*refs, rope, emit_cache):
    (x_ref, mod_ref, wdkv_ref, kvg_ref, wdq_ref, qg_ref, wuq_ref, wuk_ref, wuv_ref), rest = refs[:9], refs[9:]
    if rope:
        (wuqs_ref, cos_ref, sin_ref), rest = rest[:3], rest[3:]
    q_ref, k_ref, v_ref = rest[:3]
    x = x_ref[...]
    h = (x * (1.0 + mod_ref[4:5, :]) + mod_ref[3:4, :]).astype(BF16)
    kv = _dot(h, wdkv_ref[...])
    ckv = _rms_norm(kv[:, :KV_LORA], kvg_ref[...])
    kr = kv[:, KV_LORA:KV_LORA + 128]
    qd = _rms_norm(_dot(h, wdq_ref[...]), qg_ref[...]).astype(BF16)
    q = _dot(qd, wuq_ref[...])
    if emit_cache:
        ckv_ref, kr_ref = rest[3:5]
        ckv_ref[...] = ckv
        kr_ref[...] = kr[:, :QK_ROPE]
    if rope:
        cos = cos_ref[...]
        sin = sin_ref[...]
        kr = kr * cos + kv[:, KV_LORA + 128:KV_LORA + 256] * sin
        q_swapped = _dot(qd, wuqs_ref[...])
    ckv16 = ckv.astype(BF16)
    kn = _dot(ckv16, wuk_ref[...])
    v_ref[...] = _dot(ckv16, wuv_ref[...]).astype(BF16)
    kr16 = kr.astype(BF16)
    for hd in range(N_HEADS):
        lo = hd * HEAD_W
        q_ref[:, lo:lo + 128] = q[:, lo:lo + 128].astype(BF16)
        q_rope = q[:, lo + 128:lo + 256]
        if rope:
            q_rope = q_rope * cos + q_swapped[:, hd * 128:(hd + 1) * 128] * sin
        q_ref[:, lo + 128:lo + 256] = q_rope.astype(BF16)
        k_ref[:, lo:lo + 128] = kn[:, hd * 128:(hd + 1) * 128].astype(BF16)
        k_ref[:, lo + 128:lo + 256] = kr16


def _mla_proj(x, mod, w, *, latent):
    tm = PROJ_TILE
    first = N_PROMPT // tm if latent else 0
    n_rows = DEC_BATCH * DEC_SEQ if latent else N_PROMPT
    in_specs = [
        pl.BlockSpec((tm, D_MODEL), lambda t: (first + t, 0)),
        pl.BlockSpec((None, N_MOD, D_MODEL), lambda t: (_group_of_tile(first + t, tm), 0, 0)),
        _const_spec((D_MODEL, KV_LORA + 256)),
        _const_spec((1, KV_LORA)),
        _const_spec((D_MODEL, Q_LORA)),
        _const_spec((1, Q_LORA)),
        _const_spec((Q_LORA, N_HEADS * HEAD_W)),
        _const_spec((KV_LORA, N_HEADS * QK_NOPE)),
        _const_spec((KV_LORA, N_HEADS * V_DIM)),
    ]
    args = [x, mod, w["dkv"], w["kv_g"], w["dq"], w["q_g"], w["uq"], w["uk"], w["uv"]]
    out_shape = [
        jax.ShapeDtypeStruct((n_rows, N_HEADS * HEAD_W), BF16),
        jax.ShapeDtypeStruct((n_rows, N_HEADS * HEAD_W), BF16),
        jax.ShapeDtypeStruct((n_rows, N_HEADS * V_DIM), BF16),
    ]
    out_specs = [
        pl.BlockSpec((tm, N_HEADS * HEAD_W), lambda t: (t, 0)),
        pl.BlockSpec((tm, N_HEADS * HEAD_W), lambda t: (t, 0)),
        pl.BlockSpec((tm, N_HEADS * V_DIM), lambda t: (t, 0)),
    ]
    if latent:
        per_seq = DEC_SEQ // tm
        in_specs += [
            _const_spec((Q_LORA, N_HEADS * 128)),
            pl.BlockSpec((tm, 128), lambda t: (t % per_seq, 0)),
            pl.BlockSpec((tm, 128), lambda t: (t % per_seq, 0)),
        ]
        args += [w["uq_swapped"], w["cos"], w["sin"]]
    else:
        out_shape += [
            jax.ShapeDtypeStruct((n_rows, KV_LORA), F32),
            jax.ShapeDtypeStruct((n_rows, QK_ROPE), F32),
        ]
        out_specs += [
            pl.BlockSpec((tm, KV_LORA), lambda t: (t, 0)),
            pl.BlockSpec((tm, QK_ROPE), lambda t: (t, 0)),
        ]
    return pl.pallas_call(
        functools.partial(_mla_proj_body, rope=latent, emit_cache=not latent),
        out_shape=out_shape,
        grid=(n_rows // tm,),
        in_specs=in_specs,
        out_specs=out_specs,
        compiler_params=_params(),
        name="mla_proj_latent" if latent else "mla_proj_prompt",
    )(*args)


def _cache_expand_body(ckv_ref, kr_ref, wuk_ref, wuv_ref, k_ref, v_ref):
    ckv16 = ckv_ref[...].astype(BF16)
    kn = _dot(ckv16, wuk_ref[...])
    v_ref[...] = _dot(ckv16, wuv_ref[...]).astype(BF16)
    kr16 = kr_ref[...].astype(BF16)
    for hd in range(N_HEADS):
        lo = hd * HEAD_W
        k_ref[:, lo:lo + 128] = kn[:, hd * 128:(hd + 1) * 128].astype(BF16)
        k_ref[:, lo + 128:lo + 256] = kr16


def _cache_expand(ckv, kr_padded, w):
    n_rows = DEC_BATCH * PAST_LEN
    tm = PAST_LEN
    return pl.pallas_call(
        _cache_expand_body,
        out_shape=[
            jax.ShapeDtypeStruct((n_rows, N_HEADS * HEAD_W), BF16),
            jax.ShapeDtypeStruct((n_rows, N_HEADS * V_DIM), BF16),
        ],
        grid=(n_rows // tm,),
        in_specs=[
            pl.BlockSpec((tm, KV_LORA), lambda t: (t, 0)),
            pl.BlockSpec((tm, 128), lambda t: (t, 0)),
            _const_spec((KV_LORA, N_HEADS * QK_NOPE)),
            _const_spec((KV_LORA, N_HEADS * V_DIM)),
        ],
        out_specs=[
            pl.BlockSpec((tm, N_HEADS * HEAD_W), lambda t: (t, 0)),
            pl.BlockSpec((tm, N_HEADS * V_DIM), lambda t: (t, 0)),
        ],
        compiler_params=_params(),
        name="mla_cache_expand",
    )(ckv, kr_padded, w["uk"], w["uv"])


def _attn_body(*refs, with_cache):
    x_ref, mod_ref, g_ref, b_ref, wo_ref, q_ref, k_ref, v_ref = refs[:8]
    if with_cache:
        kc_ref, vc_ref = refs[8:10]
    o_ref, attn_ref = refs[-2:]
    sm_scale = (QK_NOPE + QK_ROPE) ** -0.5
    for hd in range(N_HEADS):
        ql = slice(hd * HEAD_W, (hd + 1) * HEAD_W)
        vl = slice(hd * V_DIM, (hd + 1) * V_DIM)
        qh = q_ref[:, ql]
        s1 = _dot_nt(qh, k_ref[:, ql]) * sm_scale
        m = jnp.max(s1, axis=-1, keepdims=True)
        if with_cache:
            s2 = _dot_nt(qh, kc_ref[:, ql]) * sm_scale
            m = jnp.maximum(m, jnp.max(s2, axis=-1, keepdims=True))
        e1 = jnp.exp(s1 - m)
        denom = jnp.sum(e1, axis=-1, keepdims=True)
        if with_cache:
            e2 = jnp.exp(s2 - m)
            denom = denom + jnp.sum(e2, axis=-1, keepdims=True)
        inv = 1.0 / denom
        out = _dot((e1 * inv).astype(BF16), v_ref[:, vl])
        if with_cache:
            out = out + _dot((e2 * inv).astype(BF16), vc_ref[:, vl])
        attn_ref[:, vl] = out.astype(BF16)
    x = x_ref[...]
    y = _dot(attn_ref[...], wo_ref[...])
    o_ref[...] = _layer_norm(ALPHA * x + mod_ref[5:6, :] * y, g_ref[...], b_ref[...])


def _attention(x, mod, ln_g, ln_b, w_o, q, k, v, cache_kv, prev_out, *, latent):
    tq = ATTN_Q_TILE
    seq_len = DEC_SEQ if latent else SEQ
    q_per_seq = seq_len // tq
    first = N_PROMPT // tq if latent else 0
    n_rows = DEC_BATCH * DEC_SEQ if latent else N_PROMPT
    row_spec = pl.BlockSpec((tq, D_MODEL), lambda t: (first + t, 0))
    in_specs = [
        row_spec,
        pl.BlockSpec((None, N_MOD, D_MODEL), lambda t: (_group_of_tile(first + t, tq), 0, 0)),
        _const_spec((1, D_MODEL)),
        _const_spec((1, D_MODEL)),
        _const_spec((N_HEADS * V_DIM, D_MODEL)),
        pl.BlockSpec((tq, N_HEADS * HEAD_W), lambda t: (t, 0)),
        pl.BlockSpec((seq_len, N_HEADS * HEAD_W), lambda t: (t // q_per_seq, 0)),
        pl.BlockSpec((seq_len, N_HEADS * V_DIM), lambda t: (t // q_per_seq, 0)),
    ]
    args = [x, mod, ln_g.reshape(1, D_MODEL), ln_b.reshape(1, D_MODEL), w_o, q, k, v]
    if latent:
        in_specs += [
            pl.BlockSpec((PAST_LEN, N_HEADS * HEAD_W), lambda t: (t // q_per_seq, 0)),
            pl.BlockSpec((PAST_LEN, N_HEADS * V_DIM), lambda t: (t // q_per_seq, 0)),
        ]
        args += list(cache_kv)
    body = functools.partial(_attn_body, with_cache=latent)
    aliases = {}
    if prev_out is not None:
        in_specs.append(pl.BlockSpec(memory_space=pl.ANY))
        args.append(prev_out)
        aliases = {len(args) - 1: 0}
        inner = body

        def body(*refs):
            n_in = len(args)
            inner(*refs[:n_in - 1], *refs[n_in:])

    return pl.pallas_call(
        body,
        out_shape=jax.ShapeDtypeStruct((N_TOKENS, D_MODEL), F32),
        grid=(n_rows // tq,),
        in_specs=in_specs,
        out_specs=row_spec,
        scratch_shapes=[pltpu.VMEM((tq, N_HEADS * V_DIM), BF16)],
        input_output_aliases=aliases,
        compiler_params=_params(),
        name="mla_attn_latent" if latent else "mla_attn_prompt",
    )(*args)


def _rope_swap_perm():
    idx = np.arange(QK_ROPE)
    return np.where((idx % 32) < ROPE_AXIS_PAIRS, idx + ROPE_AXIS_PAIRS, idx - ROPE_AXIS_PAIRS)


def _rope_tables():
    n = DEC_SEQ
    row = jnp.repeat(jnp.arange(n // GRID_W), GRID_W)
    col = jnp.tile(jnp.arange(GRID_W), n // GRID_W)
    inv = ROPE_BASE ** (-jnp.arange(ROPE_AXIS_PAIRS, dtype=F32) / ROPE_AXIS_PAIRS)
    ar = row[:, None] * inv
    ac = col[:, None] * inv
    zeros = jnp.zeros((n, 128 - QK_ROPE), F32)
    cos = jnp.concatenate([jnp.cos(ar), jnp.cos(ar), jnp.cos(ac), jnp.cos(ac), zeros], axis=-1)
    sin = jnp.concatenate([-jnp.sin(ar), jnp.sin(ar), -jnp.sin(ac), jnp.sin(ac), zeros], axis=-1)
    return cos, sin


def _mla_weights(w_dq, q_norm_g, w_uq, w_dkv, kv_norm_g, w_ukv, w_o):
    swap = _rope_swap_perm()
    z64 = jnp.zeros((D_MODEL, 128 - QK_ROPE), F32)
    kr_cols = w_dkv[:, KV_LORA:]
    dkv = jnp.concatenate([w_dkv[:, :KV_LORA], kr_cols, z64, kr_cols[:, swap], z64], axis=-1)
    uq = w_uq.reshape(Q_LORA, N_HEADS, QK_NOPE + QK_ROPE)
    zq = jnp.zeros((Q_LORA, N_HEADS, 128 - QK_ROPE), F32)
    uq_main = jnp.concatenate([uq, zq], axis=-1).reshape(Q_LORA, N_HEADS * HEAD_W)
    uq_swapped = jnp.concatenate([uq[:, :, QK_NOPE:][:, :, swap], zq], axis=-1).reshape(Q_LORA, N_HEADS * 128)
    ukv = w_ukv.reshape(KV_LORA, N_HEADS, QK_NOPE + V_DIM)
    cos, sin = _rope_tables()
    return {
        "dkv": dkv.astype(BF16),
        "kv_g": kv_norm_g.reshape(1, KV_LORA),
        "dq": w_dq.astype(BF16),
        "q_g": q_norm_g.reshape(1, Q_LORA),
        "uq": uq_main.astype(BF16),
        "uq_swapped": uq_swapped.astype(BF16),
        "uk": ukv[:, :, :QK_NOPE].reshape(KV_LORA, N_HEADS * QK_NOPE).astype(BF16),
        "uv": ukv[:, :, QK_NOPE:].reshape(KV_LORA, N_HEADS * V_DIM).astype(BF16),
        "o": w_o.astype(BF16),
        "cos": cos,
        "sin": sin,
    }


def kernel(x_prompt, x_sample, cache_mla_ckv, cache_mla_krope, c, c_ctx, w_ada, b_ada, ln_g, ln_b, ffn_w1, ffn_w3, ffn_w2, cp_w_in, conv_w, conv_b, conv_norm_g, conv_norm_b, pool_w, pool_scale, cp_w_out, mla_w_dq, mla_q_norm_g, mla_w_uq, mla_w_dkv, mla_kv_norm_g, mla_w_ukv, mla_w_o):
    c_all = jnp.concatenate(
        [c_ctx[None, :], c, jnp.zeros((N_GROUPS - 1 - DEC_BATCH, D_MODEL), F32)], axis=0)
    ada = _ada(c_all, w_ada, b_ada).reshape(DEPTH, N_GROUPS, N_MOD, D_MODEL)

    x = (x_prompt.reshape(N_PROMPT, D_MODEL), x_sample.reshape(DEC_BATCH * DEC_SEQ, D_MODEL))
    ffn_f32 = (ffn_w1, ffn_w3, ffn_w2)
    w_bf16 = tuple(wf[0, 0].astype(BF16) for wf in ffn_f32)
    new_ckv, new_krope = [], []
    for i in range(DEPTH):
        mod = ada[i]
        j = i // 2
        x, w_bf16 = _ffn(x, mod, ln_g[i, 0], ln_b[i, 0], *w_bf16, ffn_f32, (i, 1), k0=0)
        if i % 2 == 0:
            x = _conv_pool_mixer(x, mod, ln_g[i, 1], ln_b[i, 1], cp_w_in[j].astype(BF16), conv_w[j],
                                 conv_b[j], conv_norm_g[j], conv_norm_b[j], pool_w[j].astype(BF16),
                                 pool_scale[j], cp_w_out[j].astype(BF16))
        else:
            w = _mla_weights(mla_w_dq[j], mla_q_norm_g[j], mla_w_uq[j], mla_w_dkv[j],
                             mla_kv_norm_g[j], mla_w_ukv[j], mla_w_o[j])
            q_p, k_p, v_p, ckv_p, kr_p = _mla_proj(x, mod, w, latent=False)
            q_s, k_s, v_s = _mla_proj(x, mod, w, latent=True)
            new_ckv.append(ckv_p.reshape(BATCH, SEQ, KV_LORA))
            new_krope.append(kr_p.reshape(BATCH, SEQ, QK_ROPE))
            kr_cache = jnp.pad(cache_mla_krope[:, j].reshape(DEC_BATCH * PAST_LEN, QK_ROPE),
                               ((0, 0), (0, 128 - QK_ROPE)))
            cache_kv = _cache_expand(cache_mla_ckv[:, j].reshape(DEC_BATCH * PAST_LEN, KV_LORA), kr_cache, w)
            attn_args = (x, mod, ln_g[i, 1], ln_b[i, 1], w["o"])
            y = _attention(*attn_args, q_p, k_p, v_p, None, None, latent=False)
            x = _attention(*attn_args, q_s, k_s, v_s, cache_kv, y, latent=True)
        last = i == DEPTH - 1
        x, w_bf16 = _ffn(x, mod, ln_g[i, 2], ln_b[i, 2], *w_bf16,
                         None if last else ffn_f32, None if last else (i + 1, 0),
                         k0=6, split_out=last)

    y_prompt = x[0].reshape(BATCH, SEQ, D_MODEL)
    y_sample = x[1].reshape(DEC_BATCH, DEC_SEQ, D_MODEL)
    return (y_prompt, y_sample, jnp.stack(new_ckv, axis=1), jnp.stack(new_krope, axis=1))
```

```python
import functools
import math

import jax
import jax.numpy as jnp
import numpy as np
from jax import lax
from jax.experimental import pallas as pl
from jax.experimental.pallas import tpu as pltpu

F32 = jnp.float32
BF16 = jnp.bfloat16

D_MODEL = 1024
BATCH = 32
SEQ = 256
DEPTH = 2
DEC_BATCH = 4
DEC_SEQ = 1024
PAST_LEN = 512
GRID_W = 64
N_MOD = 9
D_FF = 2816
D_CONV = 512
CONV_WIDTH = 31
D_POOL = 512
POOL_WINDOWS = (2, 4, 8, 16)
POOL_GROUP = 128
N_HEADS = 8
QK_NOPE = 128
QK_ROPE = 64
V_DIM = 128
KV_LORA = 256
Q_LORA = 384
ROPE_AXIS_PAIRS = 16
ROPE_BASE = 10000.0
ALPHA = (2 * DEPTH) ** 0.25
LN_EPS = 1e-5
RMS_EPS = 1e-6

N_PROMPT = BATCH * SEQ
N_TOKENS = N_PROMPT + DEC_BATCH * DEC_SEQ
N_GROUPS = 8
HEAD_W = 256
CONV_PAD = 16
CONV_STRIDE = 4
CONV_ROWS = 8 * CONV_STRIDE
CONV_UNROLL = 4

VMEM_LIMIT_BYTES = 60 * 1024 * 1024

FFN_TILES = (1024, 512)
FFN_WORK_BYTES = 4 * 1024 * 1024
FFN_SUB = 256
MIX_TILE = 1024
MIX_SEG = 256
SEG_ROWS = MIX_SEG + 2 * CONV_PAD
PROJ_TILE = 512
ATTN_Q_TILE = 256
ATTN_PROMPT_TILE = 1024


def _layer_norm(z, g, b):
    mu = jnp.mean(z, axis=-1, keepdims=True)
    zc = z - mu
    var = jnp.mean(zc * zc, axis=-1, keepdims=True)
    return zc * lax.rsqrt(var + LN_EPS) * g + b


def _rms_norm(z, g):
    return z * lax.rsqrt(jnp.mean(z * z, axis=-1, keepdims=True) + RMS_EPS) * g


def _silu(z):
    return z * jax.nn.sigmoid(z)


def _dot(a, b):
    return jnp.dot(a, b, preferred_element_type=F32)


def _dot_nt(a, b):
    return lax.dot_general(a, b, (((1,), (1,)), ((), ())), preferred_element_type=F32)


def _const_spec(shape):
    nd = len(shape)
    return pl.BlockSpec(shape, lambda *_: (0,) * nd, pipeline_mode=pl.Buffered(1))


def _group_of_tile(t, tile):
    n_p = N_PROMPT // tile
    per_seq = DEC_SEQ // tile
    return jnp.where(t < n_p, 0, 1 + (t - n_p) // per_seq)


def _params(n_axes=1):
    return pltpu.CompilerParams(
        dimension_semantics=("arbitrary",) * n_axes,
        vmem_limit_bytes=VMEM_LIMIT_BYTES,
    )


def _ada_body(c_ref, w_ref, b_ref, o_ref):
    s = _silu(c_ref[...]).astype(BF16)
    o_ref[...] = _dot(s, w_ref[...].astype(BF16)) + b_ref[...]


def _ada(c_all, w_ada, b_ada):
    tn = D_MODEL
    return pl.pallas_call(
        _ada_body,
        out_shape=jax.ShapeDtypeStruct((DEPTH, N_GROUPS, N_MOD * D_MODEL), F32),
        grid=(DEPTH, N_MOD),
        in_specs=[
            pl.BlockSpec((N_GROUPS, D_MODEL), lambda l, k: (0, 0)),
            pl.BlockSpec((None, D_MODEL, tn), lambda l, k: (l, 0, k)),
            pl.BlockSpec((None, 1, tn), lambda l, k: (l, 0, k)),
        ],
        out_specs=pl.BlockSpec((None, N_GROUPS, tn), lambda l, k: (l, 0, k)),
        compiler_params=_params(2),
        name="ada",
    )(c_all, w_ada, b_ada.reshape(DEPTH, 1, N_MOD * D_MODEL))


def _ffn_body(*refs, k0, split_in, split_out, cast_next):
    refs = list(refs)
    n_x = 2 if split_in else 1
    x_refs, refs = refs[:n_x], refs[n_x:]
    (mod_ref, g_ref, b_ref, w1_ref, w3_ref, w2_ref), refs = refs[:6], refs[6:]
    if cast_next:
        next_f32, refs = refs[:3], refs[3:]
    n_o = 2 if split_out else 1
    o_refs, refs = refs[:n_o], refs[n_o:]
    tile = x_refs[0].shape[0]
    is_prompt = pl.program_id(0) < N_PROMPT // tile
    if cast_next:
        for src, dst in zip(next_f32, refs):
            dst[...] = src[...].astype(BF16)
    shift = mod_ref[k0:k0 + 1, :]
    scale = mod_ref[k0 + 1:k0 + 2, :]
    gate = mod_ref[k0 + 2:k0 + 3, :]
    outs = []
    for s in range(tile // FFN_SUB):
        rows = slice(s * FFN_SUB, (s + 1) * FFN_SUB)
        if split_in:
            x = jnp.where(is_prompt, x_refs[0][rows, :], x_refs[1][rows, :])
        else:
            x = x_refs[0][rows, :]
        h = (x * (1.0 + scale) + shift).astype(BF16)
        a = (_silu(_dot(h, w1_ref[...])) * _dot(h, w3_ref[...])).astype(BF16)
        y = _dot(a, w2_ref[...])
        out = _layer_norm(ALPHA * x + (0.5 * gate) * y, g_ref[...], b_ref[...])
        if split_out:
            outs.append((rows, out))
        else:
            o_refs[0][rows, :] = out
    if split_out:
        @pl.when(is_prompt)
        def _():
            for rows, out in outs:
                o_refs[0][rows, :] = out

        @pl.when(jnp.logical_not(is_prompt))
        def _():
            for rows, out in outs:
                o_refs[1][rows, :] = out


def _slab_rows(n_rows, n_steps):
    n_slabs = max(d for d in range(1, n_steps + 1) if n_rows % d == 0 and (n_rows // d) % 16 == 0)
    return n_rows // n_slabs, n_slabs


def _ffn_tile(n_in, split_out, next_w):
    for tm in FFN_TILES:
        n_steps = N_TOKENS // tm
        tile_bytes = tm * D_MODEL * 4
        est = 3 * D_MODEL * D_FF * 2
        est += (n_in + (2 if split_out else 1)) * 2 * tile_bytes
        est += FFN_WORK_BYTES + (2 * tile_bytes if split_out else 0)
        if next_w is not None:
            for wn in next_w:
                rows, _ = _slab_rows(wn.shape[2], n_steps)
                est += 2 * rows * wn.shape[3] * (4 + 2)
        if est <= VMEM_LIMIT_BYTES:
            return tm
    raise ValueError("no FFN tile fits in VMEM")


def _ffn(xs, mod, ln_g, ln_b, w1, w3, w2, next_w=None, next_idx=None, *, k0, split_out=False):
    split_in = isinstance(xs, (tuple, list))
    xs = list(xs) if split_in else [xs]
    tm = _ffn_tile(len(xs), split_out, next_w)
    n_steps = N_TOKENS // tm
    n_p = N_PROMPT // tm
    prompt_spec = pl.BlockSpec((tm, D_MODEL), lambda t: (jnp.minimum(t, n_p - 1), 0))
    latent_spec = pl.BlockSpec((tm, D_MODEL), lambda t: (jnp.maximum(t - n_p, 0), 0))
    merged_spec = pl.BlockSpec((tm, D_MODEL), lambda t: (t, 0))
    in_specs = ([prompt_spec, latent_spec] if split_in else [merged_spec]) + [
        pl.BlockSpec((None, N_MOD, D_MODEL), lambda t: (_group_of_tile(t, tm), 0, 0)),
        _const_spec((1, D_MODEL)),
        _const_spec((1, D_MODEL)),
        _const_spec((D_MODEL, D_FF)),
        _const_spec((D_MODEL, D_FF)),
        _const_spec((D_FF, D_MODEL)),
    ]
    args = xs + [mod, ln_g.reshape(1, D_MODEL), ln_b.reshape(1, D_MODEL), w1, w3, w2]
    if split_out:
        out_shape = [jax.ShapeDtypeStruct((N_PROMPT, D_MODEL), F32),
                     jax.ShapeDtypeStruct((N_TOKENS - N_PROMPT, D_MODEL), F32)]
        out_specs = [prompt_spec, latent_spec]
    else:
        out_shape = [jax.ShapeDtypeStruct((N_TOKENS, D_MODEL), F32)]
        out_specs = [merged_spec]
    if next_w is not None:
        li, lj = next_idx
        for wn in next_w:
            n_rows, n_cols = wn.shape[2:]
            rows, n_slabs = _slab_rows(n_rows, n_steps)
            in_specs.append(pl.BlockSpec(
                (None, None, rows, n_cols),
                lambda t, n_slabs=n_slabs: (li, lj, jnp.minimum(t, n_slabs - 1), 0)))
            args.append(wn)
            out_shape.append(jax.ShapeDtypeStruct((n_rows, n_cols), BF16))
            out_specs.append(pl.BlockSpec(
                (rows, n_cols), lambda t, n_slabs=n_slabs: (jnp.minimum(t, n_slabs - 1), 0)))
    outs = pl.pallas_call(
        functools.partial(_ffn_body, k0=k0, split_in=split_in, split_out=split_out,
                          cast_next=next_w is not None),
        out_shape=out_shape,
        grid=(n_steps,),
        in_specs=in_specs,
        out_specs=out_specs,
        compiler_params=_params(),
        name="ffn",
    )(*args)
    n_tok = 2 if split_out else 1
    tok = tuple(outs[:n_tok]) if split_out else outs[0]
    return tok, tuple(outs[n_tok:])


def _strided_rows(start):
    return pl.ds(start, 8, stride=CONV_STRIDE)


def _chunk_rows(j):
    chunks_per_seg = MIX_SEG // CONV_ROWS
    r0 = (j % chunks_per_seg) * CONV_ROWS
    return (j // chunks_per_seg) * SEG_ROWS + r0 + CONV_PAD, j * CONV_ROWS, r0


def _conv_chunk(j, carry, *, slab, taps_w, bias, pad_ref, conv_ref):
    p0, o0, _ = _chunk_rows(j)
    half = CONV_WIDTH // 2
    n_part = 2
    acc = [[None] * n_part for _ in range(CONV_STRIDE)]
    for off in range(-half, half + CONV_STRIDE):
        tap = pad_ref[slab, _strided_rows(p0 + off), :]
        for rho in range(CONV_STRIDE):
            k = off - rho + half
            if 0 <= k < CONV_WIDTH:
                term = taps_w[k] * tap
                part = acc[rho][k % n_part]
                acc[rho][k % n_part] = term if part is None else part + term
    for rho in range(CONV_STRIDE):
        conv_ref[slab, _strided_rows(o0 + rho), :] = (acc[rho][0] + acc[rho][1]) + bias
    return carry


def _pool_chunk(j, carry, *, pad_ref, pool_ref, is_prompt):
    p0, o0, r0 = _chunk_rows(j)
    seq_len = jnp.where(is_prompt, SEQ, DEC_SEQ)
    t0 = jnp.where(is_prompt, r0, o0) + CONV_STRIDE * lax.broadcasted_iota(jnp.int32, (8, 128), 0)
    for gi, w in enumerate(POOL_WINDOWS):
        left = w // 2
        right = w - 1 - left
        slab = D_CONV // 128 + gi
        total = [None] * CONV_STRIDE
        for off in range(-left, right + CONV_STRIDE):
            tap = pad_ref[slab, _strided_rows(p0 + off), :]
            for rho in range(CONV_STRIDE):
                if -left <= off - rho <= right:
                    total[rho] = tap if total[rho] is None else total[rho] + tap
        for rho in range(CONV_STRIDE):
            t = t0 + rho
            cnt = (jnp.minimum(t + right, seq_len - 1) - jnp.maximum(t - left, 0) + 1).astype(F32)
            tok = pad_ref[slab, _strided_rows(p0 + rho), :]
            pool_ref[gi, _strided_rows(o0 + rho), :] = total[rho] / cnt - tok
    return carry


def _cp_body(x_ref, mod_ref, g_ref, b_ref, win_ref, cw_ref, cb_ref, cng_ref, cnb_ref,
             pw_ref, ps_ref, wout_ref, o_ref, pad_ref, conv_ref, pool_ref):
    is_prompt = pl.program_id(0) < N_PROMPT // MIX_TILE
    x = x_ref[...]
    shift = mod_ref[3:4, :]
    scale = mod_ref[4:5, :]
    gate = mod_ref[5:6, :]
    h = (x * (1.0 + scale) + shift).astype(BF16)
    proj = _dot(h, win_ref[...])
    glu = proj[:, :D_CONV] * jax.nn.sigmoid(proj[:, D_CONV:2 * D_CONV])
    hp = proj[:, 2 * D_CONV:]
    n_seg = MIX_TILE // MIX_SEG
    zeros = jnp.zeros((CONV_PAD, 128), F32)
    for slab in range((D_CONV + D_POOL) // 128):
        part = (glu if slab < D_CONV // 128 else hp)[:, (slab % (D_CONV // 128)) * 128:][:, :128]
        for s in range(n_seg):
            lo = s * MIX_SEG
            hi = lo + MIX_SEG
            base = s * SEG_ROWS
            pad_ref[slab, base:base + CONV_PAD, :] = (
                zeros if s == 0 else jnp.where(is_prompt, zeros, part[lo - CONV_PAD:lo]))
            pad_ref[slab, base + CONV_PAD:base + CONV_PAD + MIX_SEG, :] = part[lo:hi]
            pad_ref[slab, base + CONV_PAD + MIX_SEG:base + SEG_ROWS, :] = (
                zeros if s == n_seg - 1 else jnp.where(is_prompt, zeros, part[hi:hi + CONV_PAD]))
    n_chunks = MIX_TILE // CONV_ROWS
    for slab in range(D_CONV // 128):
        lanes = slice(slab * 128, (slab + 1) * 128)
        taps_w = [jnp.broadcast_to(cw_ref[k:k + 1, lanes], (8, 128)) for k in range(CONV_WIDTH)]
        bias = jnp.broadcast_to(cb_ref[:, lanes], (8, 128))
        lax.fori_loop(0, n_chunks, functools.partial(
            _conv_chunk, slab=slab, taps_w=taps_w, bias=bias, pad_ref=pad_ref, conv_ref=conv_ref), 0,
            unroll=CONV_UNROLL)
    lax.fori_loop(0, n_chunks, functools.partial(
        _pool_chunk, pad_ref=pad_ref, pool_ref=pool_ref, is_prompt=is_prompt), 0,
        unroll=CONV_UNROLL)
    conv = jnp.concatenate([conv_ref[c] for c in range(D_CONV // 128)], axis=-1)
    a = _silu(_layer_norm(conv, cng_ref[...], cnb_ref[...]))
    pooled = [_dot(pool_ref[gi].astype(BF16), pw_ref[gi]) for gi in range(len(POOL_WINDOWS))]
    bmix = jnp.concatenate(pooled, axis=-1) * ps_ref[...]
    cat = jnp.concatenate([a, bmix], axis=-1).astype(BF16)
    y = _dot(cat, wout_ref[...])
    o_ref[...] = _layer_norm(ALPHA * x + gate * y, g_ref[...], b_ref[...])


def _conv_pool_mixer(x, mod, ln_g, ln_b, w_in, conv_w, conv_b, cn_g, cn_b, pool_w, pool_scale, w_out):
    tm = MIX_TILE
    assert tm == DEC_SEQ and MIX_SEG == SEQ and N_PROMPT % tm == 0
    row_spec = pl.BlockSpec((tm, D_MODEL), lambda t: (t, 0))
    in_specs = [
        row_spec,
        pl.BlockSpec((None, N_MOD, D_MODEL), lambda t: (_group_of_tile(t, tm), 0, 0)),
        _const_spec((1, D_MODEL)),
        _const_spec((1, D_MODEL)),
        _const_spec((D_MODEL, 2 * D_CONV + D_POOL)),
        _const_spec((CONV_WIDTH, D_CONV)),
        _const_spec((1, D_CONV)),
        _const_spec((1, D_CONV)),
        _const_spec((1, D_CONV)),
        _const_spec((len(POOL_WINDOWS), POOL_GROUP, POOL_GROUP)),
        _const_spec((1, D_POOL)),
        _const_spec((D_CONV + D_POOL, D_MODEL)),
    ]
    args = [x, mod, ln_g.reshape(1, D_MODEL), ln_b.reshape(1, D_MODEL), w_in, conv_w,
            conv_b.reshape(1, D_CONV), cn_g.reshape(1, D_CONV), cn_b.reshape(1, D_CONV),
            pool_w, pool_scale.reshape(1, D_POOL), w_out]
    return pl.pallas_call(
        _cp_body,
        out_shape=jax.ShapeDtypeStruct((N_TOKENS, D_MODEL), F32),
        grid=(N_TOKENS // tm,),
        in_specs=in_specs,
        out_specs=row_spec,
        scratch_shapes=[
            pltpu.VMEM(((D_CONV + D_POOL) // 128, tm // MIX_SEG * SEG_ROWS, 128), F32),
            pltpu.VMEM((D_CONV // 128, tm, 128), F32),
            pltpu.VMEM((D_POOL // 128, tm, 128), F32),
        ],
        compiler_params=_params(),
        name="conv_pool",
    )(*args)


def _mla_proj_body(*refs, rope, emit_cache):
    (x_ref, mod_ref, wdkv_ref, kvg_ref, wdq_ref, qg_ref, wuq_ref, wuk_ref, wuv_ref), rest = refs[:9], refs[9:]
    if rope:
        (wuqs_ref, cos_ref, sin_ref), rest = rest[:3], rest[3:]
    q_ref, k_ref, v_ref = rest[:3]
    x = x_ref[...]
    h = (x * (1.0 + mod_ref[4:5, :]) + mod_ref[3:4, :]).astype(BF16)
    kv = _dot(h, wdkv_ref[...])
    ckv = _rms_norm(kv[:, :KV_LORA], kvg_ref[...])
    kr = kv[:, KV_LORA:KV_LORA + 128]
    qd = _rms_norm(_dot(h, wdq_ref[...]), qg_ref[...]).astype(BF16)
    q = _dot(qd, wuq_ref[...])
    if emit_cache:
        ckv_ref, kr_ref = rest[3:5]
        ckv_ref[...] = ckv
        kr_ref[...] = kr[:, :QK_ROPE]
    if rope:
        cos = cos_ref[...]
        sin = sin_ref[...]
        kr = kr * cos + kv[:, KV_LORA + 128:KV_LORA + 256] * sin
        q_swapped = _dot(qd, wuqs_ref[...])
    ckv16 = ckv.astype(BF16)
    kn = _dot(ckv16, wuk_ref[...])
    v_ref[...] = _dot(ckv16, wuv_ref[...]).astype(BF16)
    kr16 = kr.astype(BF16)
    for hd in range(N_HEADS):
        lo = hd * HEAD_W
        q_ref[:, lo:lo + 128] = q[:, lo:lo + 128].astype(BF16)
        q_rope = q[:, lo + 128:lo + 256]
        if rope:
            q_rope = q_rope * cos + q_swapped[:, hd * 128:(hd + 1) * 128] * sin
        q_ref[:, lo + 128:lo + 256] = q_rope.astype(BF16)
        k_ref[:, lo:lo + 128] = kn[:, hd * 128:(hd + 1) * 128].astype(BF16)
        k_ref[:, lo + 128:lo + 256] = kr16


def _mla_proj(x, mod, w, *, latent):
    tm = PROJ_TILE
    first = N_PROMPT // tm if latent else 0
    n_rows = DEC_BATCH * DEC_SEQ if latent else N_PROMPT
    in_specs = [
        pl.BlockSpec((tm, D_MODEL), lambda t: (first + t, 0)),
        pl.BlockSpec((None, N_MOD, D_MODEL), lambda t: (_group_of_tile(first + t, tm), 0, 0)),
        _const_spec((D_MODEL, KV_LORA + 256)),
        _const_spec((1, KV_LORA)),
        _const_spec((D_MODEL, Q_LORA)),
        _const_spec((1, Q_LORA)),
        _const_spec((Q_LORA, N_HEADS * HEAD_W)),
        _const_spec((KV_LORA, N_HEADS * QK_NOPE)),
        _const_spec((KV_LORA, N_HEADS * V_DIM)),
    ]
    args = [x, mod, w["dkv"], w["kv_g"], w["dq"], w["q_g"], w["uq"], w["uk"], w["uv"]]
    out_shape = [
        jax.ShapeDtypeStruct((n_rows, N_HEADS * HEAD_W), BF16),
        jax.ShapeDtypeStruct((n_rows, N_HEADS * HEAD_W), BF16),
        jax.ShapeDtypeStruct((n_rows, N_HEADS * V_DIM), BF16),
    ]
    out_specs = [
        pl.BlockSpec((tm, N_HEADS * HEAD_W), lambda t: (t, 0)),
        pl.BlockSpec((tm, N_HEADS * HEAD_W), lambda t: (t, 0)),
        pl.BlockSpec((tm, N_HEADS * V_DIM), lambda t: (t, 0)),
    ]
    if latent:
        per_seq = DEC_SEQ // tm
        in_specs += [
            _const_spec((Q_LORA, N_HEADS * 128)),
            pl.BlockSpec((tm, 128), lambda t: (t % per_seq, 0)),
            pl.BlockSpec((tm, 128), lambda t: (t % per_seq, 0)),
        ]
        args += [w["uq_swapped"], w["cos"], w["sin"]]
    else:
        out_shape += [
            jax.ShapeDtypeStruct((n_rows, KV_LORA), F32),
            jax.ShapeDtypeStruct((n_rows, QK_ROPE), F32),
        ]
        out_specs += [
            pl.BlockSpec((tm, KV_LORA), lambda t: (t, 0)),
            pl.BlockSpec((tm, QK_ROPE), lambda t: (t, 0)),
        ]
    return pl.pallas_call(
        functools.partial(_mla_proj_body, rope=latent, emit_cache=not latent),
        out_shape=out_shape,
        grid=(n_rows // tm,),
        in_specs=in_specs,
        out_specs=out_specs,
        compiler_params=_params(),
        name="mla_proj_latent" if latent else "mla_proj_prompt",
    )(*args)


def _cache_expand_body(ckv_ref, kr_ref, wuk_ref, wuv_ref, k_ref, v_ref):
    ckv16 = ckv_ref[...].astype(BF16)
    kn = _dot(ckv16, wuk_ref[...])
    v_ref[...] = _dot(ckv16, wuv_ref[...]).astype(BF16)
    kr16 = kr_ref[...].astype(BF16)
    for hd in range(N_HEADS):
        lo = hd * HEAD_W
        k_ref[:, lo:lo + 128] = kn[:, hd * 128:(hd + 1) * 128].astype(BF16)
        k_ref[:, lo + 128:lo + 256] = kr16


def _cache_expand(ckv, kr_padded, w):
    n_rows = DEC_BATCH * PAST_LEN
    tm = PAST_LEN
    return pl.pallas_call(
        _cache_expand_body,
        out_shape=[
            jax.ShapeDtypeStruct((n_rows, N_HEADS * HEAD_W), BF16),
            jax.ShapeDtypeStruct((n_rows, N_HEADS * V_DIM), BF16),
        ],
        grid=(n_rows // tm,),
        in_specs=[
            pl.BlockSpec((tm, KV_LORA), lambda t: (t, 0)),
            pl.BlockSpec((tm, 128), lambda t: (t, 0)),
            _const_spec((KV_LORA, N_HEADS * QK_NOPE)),
            _const_spec((KV_LORA, N_HEADS * V_DIM)),
        ],
        out_specs=[
            pl.BlockSpec((tm, N_HEADS * HEAD_W), lambda t: (t, 0)),
            pl.BlockSpec((tm, N_HEADS * V_DIM), lambda t: (t, 0)),
        ],
        compiler_params=_params(),
        name="mla_cache_expand",
    )(ckv, kr_padded, w["uk"], w["uv"])


def _attend(q, keys, values):
    scores = [_dot_nt(q, k) for k in keys]
    m = functools.reduce(jnp.maximum, [jnp.max(s, axis=-1, keepdims=True) for s in scores])
    c = (QK_NOPE + QK_ROPE) ** -0.5 * math.log2(math.e)
    e = [jnp.exp2((s - m) * c) for s in scores]
    denom = functools.reduce(jnp.add, [jnp.sum(ei, axis=-1, keepdims=True) for ei in e])
    out = functools.reduce(jnp.add, [_dot(ei.astype(BF16), v) for ei, v in zip(e, values)])
    return out * (1.0 / denom)


def _attn_body(*refs, seqs_per_tile, with_cache):
    x_ref, mod_ref, g_ref, b_ref, wo_ref, q_ref, k_ref, v_ref = refs[:8]
    if with_cache:
        kc_ref, vc_ref = refs[8:10]
    o_ref, attn_ref = refs[-2:]
    n_keys = k_ref.shape[0] // seqs_per_tile
    n_q = q_ref.shape[0] // seqs_per_tile
    for sq in range(seqs_per_tile):
        q_rows = slice(sq * n_q, (sq + 1) * n_q)
        k_rows = slice(sq * n_keys, (sq + 1) * n_keys)
        for hd in range(N_HEADS):
            ql = slice(hd * HEAD_W, (hd + 1) * HEAD_W)
            vl = slice(hd * V_DIM, (hd + 1) * V_DIM)
            keys = [k_ref[k_rows, ql]]
            values = [v_ref[k_rows, vl]]
            if with_cache:
                keys.append(kc_ref[:, ql])
                values.append(vc_ref[:, vl])
            attn_ref[q_rows, vl] = _attend(q_ref[q_rows, ql], keys, values).astype(BF16)
    y = _dot(attn_ref[...], wo_ref[...])
    o_ref[...] = _layer_norm(ALPHA * x_ref[...] + mod_ref[5:6, :] * y, g_ref[...], b_ref[...])


def _attention(x, mod, ln_g, ln_b, w_o, q, k, v, cache_kv, *, latent):
    if latent:
        tq, seqs_per_tile, key_rows = ATTN_Q_TILE, 1, DEC_SEQ
        q_per_seq = DEC_SEQ // tq
        n_rows = DEC_BATCH * DEC_SEQ
    else:
        tq, seqs_per_tile, key_rows = ATTN_PROMPT_TILE, ATTN_PROMPT_TILE // SEQ, ATTN_PROMPT_TILE
        q_per_seq = 1
        n_rows = N_PROMPT
    first = N_PROMPT // tq if latent else 0
    in_specs = [
        pl.BlockSpec((tq, D_MODEL), lambda t: (first + t, 0)),
        pl.BlockSpec((None, N_MOD, D_MODEL), lambda t: (_group_of_tile(first + t, tq), 0, 0)),
        _const_spec((1, D_MODEL)),
        _const_spec((1, D_MODEL)),
        _const_spec((N_HEADS * V_DIM, D_MODEL)),
        pl.BlockSpec((tq, N_HEADS * HEAD_W), lambda t: (t, 0)),
        pl.BlockSpec((key_rows, N_HEADS * HEAD_W), lambda t: (t // q_per_seq, 0)),
        pl.BlockSpec((key_rows, N_HEADS * V_DIM), lambda t: (t // q_per_seq, 0)),
    ]
    args = [x, mod, ln_g.reshape(1, D_MODEL), ln_b.reshape(1, D_MODEL), w_o, q, k, v]
    if latent:
        in_specs += [
            pl.BlockSpec((PAST_LEN, N_HEADS * HEAD_W), lambda t: (t // q_per_seq, 0)),
            pl.BlockSpec((PAST_LEN, N_HEADS * V_DIM), lambda t: (t // q_per_seq, 0)),
        ]
        args += list(cache_kv)
    return pl.pallas_call(
        functools.partial(_attn_body, seqs_per_tile=seqs_per_tile, with_cache=latent),
        out_shape=jax.ShapeDtypeStruct((n_rows, D_MODEL), F32),
        grid=(n_rows // tq,),
        in_specs=in_specs,
        out_specs=pl.BlockSpec((tq, D_MODEL), lambda t: (t, 0)),
        scratch_shapes=[pltpu.VMEM((tq, N_HEADS * V_DIM), BF16)],
        compiler_params=_params(),
        name="mla_attn_latent" if latent else "mla_attn_prompt",
    )(*args)


def _rope_swap_perm():
    idx = np.arange(QK_ROPE)
    return np.where((idx % 32) < ROPE_AXIS_PAIRS, idx + ROPE_AXIS_PAIRS, idx - ROPE_AXIS_PAIRS)


def _rope_tables():
    n = DEC_SEQ
    row = jnp.repeat(jnp.arange(n // GRID_W), GRID_W)
    col = jnp.tile(jnp.arange(GRID_W), n // GRID_W)
    inv = ROPE_BASE ** (-jnp.arange(ROPE_AXIS_PAIRS, dtype=F32) / ROPE_AXIS_PAIRS)
    ar = row[:, None] * inv
    ac = col[:, None] * inv
    zeros = jnp.zeros((n, 128 - QK_ROPE), F32)
    cos = jnp.concatenate([jnp.cos(ar), jnp.cos(ar), jnp.cos(ac), jnp.cos(ac), zeros], axis=-1)
    sin = jnp.concatenate([-jnp.sin(ar), jnp.sin(ar), -jnp.sin(ac), jnp.sin(ac), zeros], axis=-1)
    return cos, sin


def _mla_weights(w_dq, q_norm_g, w_uq, w_dkv, kv_norm_g, w_ukv, w_o):
    swap = _rope_swap_perm()
    z64 = jnp.zeros((D_MODEL, 128 - QK_ROPE), F32)
    kr_cols = w_dkv[:, KV_LORA:]
    dkv = jnp.concatenate([w_dkv[:, :KV_LORA], kr_cols, z64, kr_cols[:, swap], z64], axis=-1)
    uq = w_uq.reshape(Q_LORA, N_HEADS, QK_NOPE + QK_ROPE)
    zq = jnp.zeros((Q_LORA, N_HEADS, 128 - QK_ROPE), F32)
    uq_main = jnp.concatenate([uq, zq], axis=-1).reshape(Q_LORA, N_HEADS * HEAD_W)
    uq_swapped = jnp.concatenate([uq[:, :, QK_NOPE:][:, :, swap], zq], axis=-1).reshape(Q_LORA, N_HEADS * 128)
    ukv = w_ukv.reshape(KV_LORA, N_HEADS, QK_NOPE + V_DIM)
    cos, sin = _rope_tables()
    return {
        "dkv": dkv.astype(BF16),
        "kv_g": kv_norm_g.reshape(1, KV_LORA),
        "dq": w_dq.astype(BF16),
        "q_g": q_norm_g.reshape(1, Q_LORA),
        "uq": uq_main.astype(BF16),
        "uq_swapped": uq_swapped.astype(BF16),
        "uk": ukv[:, :, :QK_NOPE].reshape(KV_LORA, N_HEADS * QK_NOPE).astype(BF16),
        "uv": ukv[:, :, QK_NOPE:].reshape(KV_LORA, N_HEADS * V_DIM).astype(BF16),
        "o": w_o.astype(BF16),
        "cos": cos,
        "sin": sin,
    }


def kernel(x_prompt, x_sample, cache_mla_ckv, cache_mla_krope, c, c_ctx, w_ada, b_ada, ln_g, ln_b, ffn_w1, ffn_w3, ffn_w2, cp_w_in, conv_w, conv_b, conv_norm_g, conv_norm_b, pool_w, pool_scale, cp_w_out, mla_w_dq, mla_q_norm_g, mla_w_uq, mla_w_dkv, mla_kv_norm_g, mla_w_ukv, mla_w_o):
    c_all = jnp.concatenate(
        [c_ctx[None, :], c, jnp.zeros((N_GROUPS - 1 - DEC_BATCH, D_MODEL), F32)], axis=0)
    ada = _ada(c_all, w_ada, b_ada).reshape(DEPTH, N_GROUPS, N_MOD, D_MODEL)

    x = (x_prompt.reshape(N_PROMPT, D_MODEL), x_sample.reshape(DEC_BATCH * DEC_SEQ, D_MODEL))
    ffn_f32 = (ffn_w1, ffn_w3, ffn_w2)
    w_bf16 = tuple(wf[0, 0].astype(BF16) for wf in ffn_f32)
    new_ckv, new_krope = [], []
    for i in range(DEPTH):
        mod = ada[i]
        j = i // 2
        x, w_bf16 = _ffn(x, mod, ln_g[i, 0], ln_b[i, 0], *w_bf16, ffn_f32, (i, 1), k0=0)
        if i % 2 == 0:
            x = _conv_pool_mixer(x, mod, ln_g[i, 1], ln_b[i, 1], cp_w_in[j].astype(BF16), conv_w[j],
                                 conv_b[j], conv_norm_g[j], conv_norm_b[j], pool_w[j].astype(BF16),
                                 pool_scale[j], cp_w_out[j].astype(BF16))
        else:
            w = _mla_weights(mla_w_dq[j], mla_q_norm_g[j], mla_w_uq[j], mla_w_dkv[j],
                             mla_kv_norm_g[j], mla_w_ukv[j], mla_w_o[j])
            q_p, k_p, v_p, ckv_p, kr_p = _mla_proj(x, mod, w, latent=False)
            q_s, k_s, v_s = _mla_proj(x, mod, w, latent=True)
            new_ckv.append(ckv_p.reshape(BATCH, SEQ, KV_LORA))
            new_krope.append(kr_p.reshape(BATCH, SEQ, QK_ROPE))
            kr_cache = jnp.pad(cache_mla_krope[:, j].reshape(DEC_BATCH * PAST_LEN, QK_ROPE),
                               ((0, 0), (0, 128 - QK_ROPE)))
            cache_kv = _cache_expand(cache_mla_ckv[:, j].reshape(DEC_BATCH * PAST_LEN, KV_LORA), kr_cache, w)
            attn_args = (x, mod, ln_g[i, 1], ln_b[i, 1], w["o"])
            x = (_attention(*attn_args, q_p, k_p, v_p, None, latent=False),
                 _attention(*attn_args, q_s, k_s, v_s, cache_kv, latent=True))
        last = i == DEPTH - 1
        x, w_bf16 = _ffn(x, mod, ln_g[i, 2], ln_b[i, 2], *w_bf16,
                         None if last else ffn_f32, None if last else (i + 1, 0),
                         k0=6, split_out=last)

    y_prompt = x[0].reshape(BATCH, SEQ, D_MODEL)
    y_sample = x[1].reshape(DEC_BATCH, DEC_SEQ, D_MODEL)
    return (y_prompt, y_sample, jnp.stack(new_ckv, axis=1), jnp.stack(new_krope, axis=1))
```

```python
import functools
import math

import jax
import jax.numpy as jnp
import numpy as np
from jax import lax
from jax.experimental import pallas as pl
from jax.experimental.pallas import tpu as pltpu

F32 = jnp.float32
BF16 = jnp.bfloat16

D_MODEL = 1024
BATCH = 32
SEQ = 256
DEPTH = 2
DEC_BATCH = 4
DEC_SEQ = 1024
PAST_LEN = 512
GRID_W = 64
N_MOD = 9
D_FF = 2816
D_CONV = 512
CONV_WIDTH = 31
D_POOL = 512
POOL_WINDOWS = (2, 4, 8, 16)
POOL_GROUP = 128
N_HEADS = 8
QK_NOPE = 128
QK_ROPE = 64
V_DIM = 128
KV_LORA = 256
Q_LORA = 384
ROPE_AXIS_PAIRS = 16
ROPE_BASE = 10000.0
ALPHA = (2 * DEPTH) ** 0.25
LN_EPS = 1e-5
RMS_EPS = 1e-6

N_PROMPT = BATCH * SEQ
N_TOKENS = N_PROMPT + DEC_BATCH * DEC_SEQ
N_GROUPS = 8
HEAD_W = 256
CONV_PAD = 16
CONV_STRIDE = 4
CONV_ROWS = 8 * CONV_STRIDE
CONV_UNROLL = 4

VMEM_LIMIT_BYTES = 60 * 1024 * 1024

FFN_TILES = (1024, 512)
FFN_WORK_BYTES = 4 * 1024 * 1024
FFN_SUB = 256
MIX_TILE = 1024
MIX_SEG = 256
SEG_ROWS = MIX_SEG + 2 * CONV_PAD
MLA_TILE = 1024
ATTN_Q_TILE = 256


def _layer_norm(z, g, b):
    mu = jnp.mean(z, axis=-1, keepdims=True)
    zc = z - mu
    var = jnp.mean(zc * zc, axis=-1, keepdims=True)
    return zc * lax.rsqrt(var + LN_EPS) * g + b


def _rms_norm(z, g):
    return z * lax.rsqrt(jnp.mean(z * z, axis=-1, keepdims=True) + RMS_EPS) * g


def _silu(z):
    return z * jax.nn.sigmoid(z)


def _dot(a, b):
    return jnp.dot(a, b, preferred_element_type=F32)


def _dot_nt(a, b):
    return lax.dot_general(a, b, (((1,), (1,)), ((), ())), preferred_element_type=F32)


def _const_spec(shape):
    nd = len(shape)
    return pl.BlockSpec(shape, lambda *_: (0,) * nd, pipeline_mode=pl.Buffered(1))


def _group_of_tile(t, tile):
    n_p = N_PROMPT // tile
    per_seq = DEC_SEQ // tile
    return jnp.where(t < n_p, 0, 1 + (t - n_p) // per_seq)


def _params(n_axes=1):
    return pltpu.CompilerParams(
        dimension_semantics=("arbitrary",) * n_axes,
        vmem_limit_bytes=VMEM_LIMIT_BYTES,
    )


def _ada_body(c_ref, w_ref, b_ref, o_ref):
    s = _silu(c_ref[...]).astype(BF16)
    o_ref[...] = _dot(s, w_ref[...].astype(BF16)) + b_ref[...]


def _ada(c_all, w_ada, b_ada):
    tn = D_MODEL
    return pl.pallas_call(
        _ada_body,
        out_shape=jax.ShapeDtypeStruct((DEPTH, N_GROUPS, N_MOD * D_MODEL), F32),
        grid=(DEPTH, N_MOD),
        in_specs=[
            pl.BlockSpec((N_GROUPS, D_MODEL), lambda l, k: (0, 0)),
            pl.BlockSpec((None, D_MODEL, tn), lambda l, k: (l, 0, k)),
            pl.BlockSpec((None, 1, tn), lambda l, k: (l, 0, k)),
        ],
        out_specs=pl.BlockSpec((None, N_GROUPS, tn), lambda l, k: (l, 0, k)),
        compiler_params=_params(2),
        name="ada",
    )(c_all, w_ada, b_ada.reshape(DEPTH, 1, N_MOD * D_MODEL))


def _ffn_body(*refs, k0, split_in, split_out, cast_next):
    refs = list(refs)
    n_x = 2 if split_in else 1
    x_refs, refs = refs[:n_x], refs[n_x:]
    (mod_ref, g_ref, b_ref, w1_ref, w3_ref, w2_ref), refs = refs[:6], refs[6:]
    if cast_next:
        next_f32, refs = refs[:3], refs[3:]
    n_o = 2 if split_out else 1
    o_refs, refs = refs[:n_o], refs[n_o:]
    tile = x_refs[0].shape[0]
    is_prompt = pl.program_id(0) < N_PROMPT // tile
    if cast_next:
        for src, dst in zip(next_f32, refs):
            dst[...] = src[...].astype(BF16)
    shift = mod_ref[k0:k0 + 1, :]
    scale = mod_ref[k0 + 1:k0 + 2, :]
    gate = mod_ref[k0 + 2:k0 + 3, :]
    outs = []
    for s in range(tile // FFN_SUB):
        rows = slice(s * FFN_SUB, (s + 1) * FFN_SUB)
        if split_in:
            x = jnp.where(is_prompt, x_refs[0][rows, :], x_refs[1][rows, :])
        else:
            x = x_refs[0][rows, :]
        h = (x * (1.0 + scale) + shift).astype(BF16)
        a = (_silu(_dot(h, w1_ref[...])) * _dot(h, w3_ref[...])).astype(BF16)
        y = _dot(a, w2_ref[...])
        out = _layer_norm(ALPHA * x + (0.5 * gate) * y, g_ref[...], b_ref[...])
        if split_out:
            outs.append((rows, out))
        else:
            o_refs[0][rows, :] = out
    if split_out:
        @pl.when(is_prompt)
        def _():
            for rows, out in outs:
                o_refs[0][rows, :] = out

        @pl.when(jnp.logical_not(is_prompt))
        def _():
            for rows, out in outs:
                o_refs[1][rows, :] = out


def _slab_rows(n_rows, n_steps):
    n_slabs = max(d for d in range(1, n_steps + 1) if n_rows % d == 0 and (n_rows // d) % 16 == 0)
    return n_rows // n_slabs, n_slabs


def _ffn_tile(n_in, split_out, next_w):
    for tm in FFN_TILES:
        n_steps = N_TOKENS // tm
        tile_bytes = tm * D_MODEL * 4
        est = 3 * D_MODEL * D_FF * 2
        est += (n_in + (2 if split_out else 1)) * 2 * tile_bytes
        est += FFN_WORK_BYTES + (2 * tile_bytes if split_out else 0)
        if next_w is not None:
            for wn in next_w:
                rows, _ = _slab_rows(wn.shape[2], n_steps)
                est += 2 * rows * wn.shape[3] * (4 + 2)
        if est <= VMEM_LIMIT_BYTES:
            return tm
    raise ValueError("no FFN tile fits in VMEM")


def _ffn(xs, mod, ln_g, ln_b, w1, w3, w2, next_w=None, next_idx=None, *, k0, split_out=False):
    split_in = isinstance(xs, (tuple, list))
    xs = list(xs) if split_in else [xs]
    tm = _ffn_tile(len(xs), split_out, next_w)
    n_steps = N_TOKENS // tm
    n_p = N_PROMPT // tm
    prompt_spec = pl.BlockSpec((tm, D_MODEL), lambda t: (jnp.minimum(t, n_p - 1), 0))
    latent_spec = pl.BlockSpec((tm, D_MODEL), lambda t: (jnp.maximum(t - n_p, 0), 0))
    merged_spec = pl.BlockSpec((tm, D_MODEL), lambda t: (t, 0))
    in_specs = ([prompt_spec, latent_spec] if split_in else [merged_spec]) + [
        pl.BlockSpec((None, N_MOD, D_MODEL), lambda t: (_group_of_tile(t, tm), 0, 0)),
        _const_spec((1, D_MODEL)),
        _const_spec((1, D_MODEL)),
        _const_spec((D_MODEL, D_FF)),
        _const_spec((D_MODEL, D_FF)),
        _const_spec((D_FF, D_MODEL)),
    ]
    args = xs + [mod, ln_g.reshape(1, D_MODEL), ln_b.reshape(1, D_MODEL), w1, w3, w2]
    if split_out:
        out_shape = [jax.ShapeDtypeStruct((N_PROMPT, D_MODEL), F32),
                     jax.ShapeDtypeStruct((N_TOKENS - N_PROMPT, D_MODEL), F32)]
        out_specs = [prompt_spec, latent_spec]
    else:
        out_shape = [jax.ShapeDtypeStruct((N_TOKENS, D_MODEL), F32)]
        out_specs = [merged_spec]
    if next_w is not None:
        li, lj = next_idx
        for wn in next_w:
            n_rows, n_cols = wn.shape[2:]
            rows, n_slabs = _slab_rows(n_rows, n_steps)
            in_specs.append(pl.BlockSpec(
                (None, None, rows, n_cols),
                lambda t, n_slabs=n_slabs: (li, lj, jnp.minimum(t, n_slabs - 1), 0)))
            args.append(wn)
            out_shape.append(jax.ShapeDtypeStruct((n_rows, n_cols), BF16))
            out_specs.append(pl.BlockSpec(
                (rows, n_cols), lambda t, n_slabs=n_slabs: (jnp.minimum(t, n_slabs - 1), 0)))
    outs = pl.pallas_call(
        functools.partial(_ffn_body, k0=k0, split_in=split_in, split_out=split_out,
                          cast_next=next_w is not None),
        out_shape=out_shape,
        grid=(n_steps,),
        in_specs=in_specs,
        out_specs=out_specs,
        compiler_params=_params(),
        name="ffn",
    )(*args)
    n_tok = 2 if split_out else 1
    tok = tuple(outs[:n_tok]) if split_out else outs[0]
    return tok, tuple(outs[n_tok:])


def _strided_rows(start):
    return pl.ds(start, 8, stride=CONV_STRIDE)


def _chunk_rows(j):
    chunks_per_seg = MIX_SEG // CONV_ROWS
    r0 = (j % chunks_per_seg) * CONV_ROWS
    return (j // chunks_per_seg) * SEG_ROWS + r0 + CONV_PAD, j * CONV_ROWS, r0


def _conv_chunk(j, carry, *, slab, taps_w, bias, pad_ref, conv_ref):
    p0, o0, _ = _chunk_rows(j)
    half = CONV_WIDTH // 2
    n_part = 2
    acc = [[None] * n_part for _ in range(CONV_STRIDE)]
    for off in range(-half, half + CONV_STRIDE):
        tap = pad_ref[slab, _strided_rows(p0 + off), :]
        for rho in range(CONV_STRIDE):
            k = off - rho + half
            if 0 <= k < CONV_WIDTH:
                term = taps_w[k] * tap
                part = acc[rho][k % n_part]
                acc[rho][k % n_part] = term if part is None else part + term
    for rho in range(CONV_STRIDE):
        conv_ref[slab, _strided_rows(o0 + rho), :] = (acc[rho][0] + acc[rho][1]) + bias
    return carry


def _pool_chunk(j, carry, *, pad_ref, pool_ref, is_prompt):
    p0, o0, r0 = _chunk_rows(j)
    seq_len = jnp.where(is_prompt, SEQ, DEC_SEQ)
    t0 = jnp.where(is_prompt, r0, o0) + CONV_STRIDE * lax.broadcasted_iota(jnp.int32, (8, 128), 0)
    for gi, w in enumerate(POOL_WINDOWS):
        left = w // 2
        right = w - 1 - left
        slab = D_CONV // 128 + gi
        total = [None] * CONV_STRIDE
        for off in range(-left, right + CONV_STRIDE):
            tap = pad_ref[slab, _strided_rows(p0 + off), :]
            for rho in range(CONV_STRIDE):
                if -left <= off - rho <= right:
                    total[rho] = tap if total[rho] is None else total[rho] + tap
        for rho in range(CONV_STRIDE):
            t = t0 + rho
            cnt = (jnp.minimum(t + right, seq_len - 1) - jnp.maximum(t - left, 0) + 1).astype(F32)
            tok = pad_ref[slab, _strided_rows(p0 + rho), :]
            pool_ref[gi, _strided_rows(o0 + rho), :] = total[rho] / cnt - tok
    return carry


def _cp_body(x_ref, mod_ref, g_ref, b_ref, win_ref, cw_ref, cb_ref, cng_ref, cnb_ref,
             pw_ref, ps_ref, wout_ref, o_ref, pad_ref, conv_ref, pool_ref):
    is_prompt = pl.program_id(0) < N_PROMPT // MIX_TILE
    x = x_ref[...]
    shift = mod_ref[3:4, :]
    scale = mod_ref[4:5, :]
    gate = mod_ref[5:6, :]
    h = (x * (1.0 + scale) + shift).astype(BF16)
    proj = _dot(h, win_ref[...])
    glu = proj[:, :D_CONV] * jax.nn.sigmoid(proj[:, D_CONV:2 * D_CONV])
    hp = proj[:, 2 * D_CONV:]
    n_seg = MIX_TILE // MIX_SEG
    zeros = jnp.zeros((CONV_PAD, 128), F32)
    for slab in range((D_CONV + D_POOL) // 128):
        part = (glu if slab < D_CONV // 128 else hp)[:, (slab % (D_CONV // 128)) * 128:][:, :128]
        for s in range(n_seg):
            lo = s * MIX_SEG
            hi = lo + MIX_SEG
            base = s * SEG_ROWS
            pad_ref[slab, base:base + CONV_PAD, :] = (
                zeros if s == 0 else jnp.where(is_prompt, zeros, part[lo - CONV_PAD:lo]))
            pad_ref[slab, base + CONV_PAD:base + CONV_PAD + MIX_SEG, :] = part[lo:hi]
            pad_ref[slab, base + CONV_PAD + MIX_SEG:base + SEG_ROWS, :] = (
                zeros if s == n_seg - 1 else jnp.where(is_prompt, zeros, part[hi:hi + CONV_PAD]))
    n_chunks = MIX_TILE // CONV_ROWS
    for slab in range(D_CONV // 128):
        lanes = slice(slab * 128, (slab + 1) * 128)
        taps_w = [jnp.broadcast_to(cw_ref[k:k + 1, lanes], (8, 128)) for k in range(CONV_WIDTH)]
        bias = jnp.broadcast_to(cb_ref[:, lanes], (8, 128))
        lax.fori_loop(0, n_chunks, functools.partial(
            _conv_chunk, slab=slab, taps_w=taps_w, bias=bias, pad_ref=pad_ref, conv_ref=conv_ref), 0,
            unroll=CONV_UNROLL)
    lax.fori_loop(0, n_chunks, functools.partial(
        _pool_chunk, pad_ref=pad_ref, pool_ref=pool_ref, is_prompt=is_prompt), 0,
        unroll=CONV_UNROLL)
    conv = jnp.concatenate([conv_ref[c] for c in range(D_CONV // 128)], axis=-1)
    a = _silu(_layer_norm(conv, cng_ref[...], cnb_ref[...]))
    pooled = [_dot(pool_ref[gi].astype(BF16), pw_ref[gi]) for gi in range(len(POOL_WINDOWS))]
    bmix = jnp.concatenate(pooled, axis=-1) * ps_ref[...]
    cat = jnp.concatenate([a, bmix], axis=-1).astype(BF16)
    y = _dot(cat, wout_ref[...])
    o_ref[...] = _layer_norm(ALPHA * x + gate * y, g_ref[...], b_ref[...])


def _conv_pool_mixer(x, mod, ln_g, ln_b, w_in, conv_w, conv_b, cn_g, cn_b, pool_w, pool_scale, w_out):
    tm = MIX_TILE
    assert tm == DEC_SEQ and MIX_SEG == SEQ and N_PROMPT % tm == 0
    row_spec = pl.BlockSpec((tm, D_MODEL), lambda t: (t, 0))
    in_specs = [
        row_spec,
        pl.BlockSpec((None, N_MOD, D_MODEL), lambda t: (_group_of_tile(t, tm), 0, 0)),
        _const_spec((1, D_MODEL)),
        _const_spec((1, D_MODEL)),
        _const_spec((D_MODEL, 2 * D_CONV + D_POOL)),
        _const_spec((CONV_WIDTH, D_CONV)),
        _const_spec((1, D_CONV)),
        _const_spec((1, D_CONV)),
        _const_spec((1, D_CONV)),
        _const_spec((len(POOL_WINDOWS), POOL_GROUP, POOL_GROUP)),
        _const_spec((1, D_POOL)),
        _const_spec((D_CONV + D_POOL, D_MODEL)),
    ]
    args = [x, mod, ln_g.reshape(1, D_MODEL), ln_b.reshape(1, D_MODEL), w_in, conv_w,
            conv_b.reshape(1, D_CONV), cn_g.reshape(1, D_CONV), cn_b.reshape(1, D_CONV),
            pool_w, pool_scale.reshape(1, D_POOL), w_out]
    return pl.pallas_call(
        _cp_body,
        out_shape=jax.ShapeDtypeStruct((N_TOKENS, D_MODEL), F32),
        grid=(N_TOKENS // tm,),
        in_specs=in_specs,
        out_specs=row_spec,
        scratch_shapes=[
            pltpu.VMEM(((D_CONV + D_POOL) // 128, tm // MIX_SEG * SEG_ROWS, 128), F32),
            pltpu.VMEM((D_CONV // 128, tm, 128), F32),
            pltpu.VMEM((D_POOL // 128, tm, 128), F32),
        ],
        compiler_params=_params(),
        name="conv_pool",
    )(*args)


def _attend(q, keys, values):
    scores = [_dot_nt(q, k) for k in keys]
    m = functools.reduce(jnp.maximum, [jnp.max(s, axis=-1, keepdims=True) for s in scores])
    c = (QK_NOPE + QK_ROPE) ** -0.5 * math.log2(math.e)
    e = [jnp.exp2((s - m) * c) for s in scores]
    denom = functools.reduce(jnp.add, [jnp.sum(ei, axis=-1, keepdims=True) for ei in e])
    out = functools.reduce(jnp.add, [_dot(ei.astype(BF16), v) for ei, v in zip(e, values)])
    return out * (1.0 / denom)


def _mla_project(x, mod_ref, wdkv_ref, kvg_ref, wdq_ref, qg_ref, wuq_ref):
    h = (x * (1.0 + mod_ref[4:5, :]) + mod_ref[3:4, :]).astype(BF16)
    kv = _dot(h, wdkv_ref[...])
    ckv = _rms_norm(kv[:, :KV_LORA], kvg_ref[...])
    qd = _rms_norm(_dot(h, wdq_ref[...]), qg_ref[...]).astype(BF16)
    return kv, ckv, qd, _dot(qd, wuq_ref[...])


def _mla_out(x, attn, mod_ref, g_ref, b_ref, wo_ref):
    y = _dot(attn, wo_ref[...])
    return _layer_norm(ALPHA * x + mod_ref[5:6, :] * y, g_ref[...], b_ref[...])


def _mla_prompt_body(x_ref, mod_ref, g_ref, b_ref, wdkv_ref, kvg_ref, wdq_ref, qg_ref, wuq_ref,
                     wuk_ref, wuv_ref, wo_ref, o_ref, ckv_ref, kr_ref):
    for s in range(x_ref.shape[0] // SEQ):
        rows = slice(s * SEQ, (s + 1) * SEQ)
        x = x_ref[rows, :]
        kv, ckv, _, q = _mla_project(x, mod_ref, wdkv_ref, kvg_ref, wdq_ref, qg_ref, wuq_ref)
        ckv_ref[rows, :] = ckv
        kr_ref[rows, :] = kv[:, KV_LORA:KV_LORA + QK_ROPE]
        q16 = q.astype(BF16)
        ckv16 = ckv.astype(BF16)
        kn16 = _dot(ckv16, wuk_ref[...]).astype(BF16)
        v16 = _dot(ckv16, wuv_ref[...]).astype(BF16)
        kr16 = kv[:, KV_LORA:KV_LORA + 128].astype(BF16)
        heads = []
        for hd in range(N_HEADS):
            vl = slice(hd * V_DIM, (hd + 1) * V_DIM)
            k_h = jnp.concatenate([kn16[:, hd * QK_NOPE:(hd + 1) * QK_NOPE], kr16], axis=-1)
            heads.append(_attend(q16[:, hd * HEAD_W:(hd + 1) * HEAD_W], [k_h], [v16[:, vl]]).astype(BF16))
        attn = jnp.concatenate(heads, axis=-1)
        o_ref[rows, :] = _mla_out(x, attn, mod_ref, g_ref, b_ref, wo_ref)


def _store_keys(k_ref, rows, kn16, kr16):
    for hd in range(N_HEADS):
        k_ref[rows, hd * HEAD_W:hd * HEAD_W + QK_NOPE] = kn16[:, hd * QK_NOPE:(hd + 1) * QK_NOPE]
        k_ref[rows, hd * HEAD_W + QK_NOPE:(hd + 1) * HEAD_W] = kr16


def _mla_latent_body(x_ref, mod_ref, g_ref, b_ref, wdkv_ref, kvg_ref, wdq_ref, qg_ref, wuq_ref,
                     wuk_ref, wuv_ref, wo_ref, wuqs_ref, cos_ref, sin_ref, cckv_ref, ckr_ref,
                     o_ref, q_s, k_s, v_s, kc_s, vc_s):
    c16 = cckv_ref[...].astype(BF16)
    vc_s[...] = _dot(c16, wuv_ref[...]).astype(BF16)
    _store_keys(kc_s, slice(None), _dot(c16, wuk_ref[...]).astype(BF16), ckr_ref[...].astype(BF16))
    for s in range(DEC_SEQ // ATTN_Q_TILE):
        rows = slice(s * ATTN_Q_TILE, (s + 1) * ATTN_Q_TILE)
        cos = cos_ref[rows, :]
        sin = sin_ref[rows, :]
        kv, ckv, qd, q = _mla_project(x_ref[rows, :], mod_ref, wdkv_ref, kvg_ref, wdq_ref, qg_ref, wuq_ref)
        q_swapped = _dot(qd, wuqs_ref[...])
        kr = kv[:, KV_LORA:KV_LORA + 128] * cos + kv[:, KV_LORA + 128:KV_LORA + 256] * sin
        ckv16 = ckv.astype(BF16)
        v_s[rows, :] = _dot(ckv16, wuv_ref[...]).astype(BF16)
        _store_keys(k_s, rows, _dot(ckv16, wuk_ref[...]).astype(BF16), kr.astype(BF16))
        for hd in range(N_HEADS):
            lo = hd * HEAD_W
            q_s[rows, lo:lo + QK_NOPE] = q[:, lo:lo + QK_NOPE].astype(BF16)
            q_rope = q[:, lo + QK_NOPE:lo + HEAD_W] * cos + q_swapped[:, hd * 128:(hd + 1) * 128] * sin
            q_s[rows, lo + QK_NOPE:lo + HEAD_W] = q_rope.astype(BF16)

    def q_block(i, carry):
        rows = pl.ds(pl.multiple_of(i * ATTN_Q_TILE, ATTN_Q_TILE), ATTN_Q_TILE)
        heads = []
        for hd in range(N_HEADS):
            ql = slice(hd * HEAD_W, (hd + 1) * HEAD_W)
            vl = slice(hd * V_DIM, (hd + 1) * V_DIM)
            heads.append(_attend(q_s[rows, ql], [k_s[:, ql], kc_s[:, ql]],
                                 [v_s[:, vl], vc_s[:, vl]]).astype(BF16))
        attn = jnp.concatenate(heads, axis=-1)
        o_ref[rows, :] = _mla_out(x_ref[rows, :], attn, mod_ref, g_ref, b_ref, wo_ref)
        return carry

    lax.fori_loop(0, DEC_SEQ // ATTN_Q_TILE, q_block, 0)


def _mla_mixer(x, mod, ln_g, ln_b, w, cache_ckv, cache_kr, *, latent):
    tile = MLA_TILE
    first = N_PROMPT // tile if latent else 0
    n_rows = DEC_BATCH * DEC_SEQ if latent else N_PROMPT
    dkv = w["dkv"] if latent else w["dkv"][:, :KV_LORA + 128]
    in_specs = [
        pl.BlockSpec((tile, D_MODEL), lambda t: (first + t, 0)),
        pl.BlockSpec((None, N_MOD, D_MODEL), lambda t: (_group_of_tile(first + t, tile), 0, 0)),
        _const_spec((1, D_MODEL)),
        _const_spec((1, D_MODEL)),
        _const_spec(dkv.shape),
        _const_spec((1, KV_LORA)),
        _const_spec((D_MODEL, Q_LORA)),
        _const_spec((1, Q_LORA)),
        _const_spec((Q_LORA, N_HEADS * HEAD_W)),
        _const_spec((KV_LORA, N_HEADS * QK_NOPE)),
        _const_spec((KV_LORA, N_HEADS * V_DIM)),
        _const_spec((N_HEADS * V_DIM, D_MODEL)),
    ]
    args = [x, mod, ln_g.reshape(1, D_MODEL), ln_b.reshape(1, D_MODEL), dkv, w["kv_g"], w["dq"],
            w["q_g"], w["uq"], w["uk"], w["uv"], w["o"]]
    out_shape = [jax.ShapeDtypeStruct((n_rows, D_MODEL), F32)]
    out_specs = [pl.BlockSpec((tile, D_MODEL), lambda t: (t, 0))]
    scratch = []
    if latent:
        assert tile == DEC_SEQ
        in_specs += [
            _const_spec((Q_LORA, N_HEADS * 128)),
            _const_spec((DEC_SEQ, 128)),
            _const_spec((DEC_SEQ, 128)),
            pl.BlockSpec((None, PAST_LEN, KV_LORA), lambda t: (t, 0, 0)),
            pl.BlockSpec((None, PAST_LEN, 128), lambda t: (t, 0, 0)),
        ]
        args += [w["uq_swapped"], w["cos"], w["sin"], cache_ckv, cache_kr]
        scratch = [
            pltpu.VMEM((DEC_SEQ, N_HEADS * HEAD_W), BF16),
            pltpu.VMEM((DEC_SEQ, N_HEADS * HEAD_W), BF16),
            pltpu.VMEM((DEC_SEQ, N_HEADS * V_DIM), BF16),
            pltpu.VMEM((PAST_LEN, N_HEADS * HEAD_W), BF16),
            pltpu.VMEM((PAST_LEN, N_HEADS * V_DIM), BF16),
        ]
    else:
        out_shape += [jax.ShapeDtypeStruct((n_rows, KV_LORA), F32),
                      jax.ShapeDtypeStruct((n_rows, QK_ROPE), F32)]
        out_specs += [pl.BlockSpec((tile, KV_LORA), lambda t: (t, 0)),
                      pl.BlockSpec((tile, QK_ROPE), lambda t: (t, 0))]
    outs = pl.pallas_call(
        _mla_latent_body if latent else _mla_prompt_body,
        out_shape=out_shape,
        grid=(n_rows // tile,),
        in_specs=in_specs,
        out_specs=out_specs,
        scratch_shapes=scratch,
        compiler_params=_params(),
        name="mla_latent" if latent else "mla_prompt",
    )(*args)
    return outs[0] if latent else outs


def _rope_swap_perm():
    idx = np.arange(QK_ROPE)
    return np.where((idx % 32) < ROPE_AXIS_PAIRS, idx + ROPE_AXIS_PAIRS, idx - ROPE_AXIS_PAIRS)


def _rope_tables():
    n = DEC_SEQ
    row = jnp.repeat(jnp.arange(n // GRID_W), GRID_W)
    col = jnp.tile(jnp.arange(GRID_W), n // GRID_W)
    inv = ROPE_BASE ** (-jnp.arange(ROPE_AXIS_PAIRS, dtype=F32) / ROPE_AXIS_PAIRS)
    ar = row[:, None] * inv
    ac = col[:, None] * inv
    zeros = jnp.zeros((n, 128 - QK_ROPE), F32)
    cos = jnp.concatenate([jnp.cos(ar), jnp.cos(ar), jnp.cos(ac), jnp.cos(ac), zeros], axis=-1)
    sin = jnp.concatenate([-jnp.sin(ar), jnp.sin(ar), -jnp.sin(ac), jnp.sin(ac), zeros], axis=-1)
    return cos, sin


def _mla_weights(w_dq, q_norm_g, w_uq, w_dkv, kv_norm_g, w_ukv, w_o):
    swap = _rope_swap_perm()
    z64 = jnp.zeros((D_MODEL, 128 - QK_ROPE), F32)
    kr_cols = w_dkv[:, KV_LORA:]
    dkv = jnp.concatenate([w_dkv[:, :KV_LORA], kr_cols, z64, kr_cols[:, swap], z64], axis=-1)
    uq = w_uq.reshape(Q_LORA, N_HEADS, QK_NOPE + QK_ROPE)
    zq = jnp.zeros((Q_LORA, N_HEADS, 128 - QK_ROPE), F32)
    uq_main = jnp.concatenate([uq, zq], axis=-1).reshape(Q_LORA, N_HEADS * HEAD_W)
    uq_swapped = jnp.concatenate([uq[:, :, QK_NOPE:][:, :, swap], zq], axis=-1).reshape(Q_LORA, N_HEADS * 128)
    ukv = w_ukv.reshape(KV_LORA, N_HEADS, QK_NOPE + V_DIM)
    cos, sin = _rope_tables()
    return {
        "dkv": dkv.astype(BF16),
        "kv_g": kv_norm_g.reshape(1, KV_LORA),
        "dq": w_dq.astype(BF16),
        "q_g": q_norm_g.reshape(1, Q_LORA),
        "uq": uq_main.astype(BF16),
        "uq_swapped": uq_swapped.astype(BF16),
        "uk": ukv[:, :, :QK_NOPE].reshape(KV_LORA, N_HEADS * QK_NOPE).astype(BF16),
        "uv": ukv[:, :, QK_NOPE:].reshape(KV_LORA, N_HEADS * V_DIM).astype(BF16),
        "o": w_o.astype(BF16),
        "cos": cos,
        "sin": sin,
    }


def kernel(x_prompt, x_sample, cache_mla_ckv, cache_mla_krope, c, c_ctx, w_ada, b_ada, ln_g, ln_b, ffn_w1, ffn_w3, ffn_w2, cp_w_in, conv_w, conv_b, conv_norm_g, conv_norm_b, pool_w, pool_scale, cp_w_out, mla_w_dq, mla_q_norm_g, mla_w_uq, mla_w_dkv, mla_kv_norm_g, mla_w_ukv, mla_w_o):
    c_all = jnp.concatenate(
        [c_ctx[None, :], c, jnp.zeros((N_GROUPS - 1 - DEC_BATCH, D_MODEL), F32)], axis=0)
    ada = _ada(c_all, w_ada, b_ada).reshape(DEPTH, N_GROUPS, N_MOD, D_MODEL)

    x = (x_prompt.reshape(N_PROMPT, D_MODEL), x_sample.reshape(DEC_BATCH * DEC_SEQ, D_MODEL))
    ffn_f32 = (ffn_w1, ffn_w3, ffn_w2)
    w_bf16 = tuple(wf[0, 0].astype(BF16) for wf in ffn_f32)
    new_ckv, new_krope = [], []
    for i in range(DEPTH):
        mod = ada[i]
        j = i // 2
        x, w_bf16 = _ffn(x, mod, ln_g[i, 0], ln_b[i, 0], *w_bf16, ffn_f32, (i, 1), k0=0)
        if i % 2 == 0:
            x = _conv_pool_mixer(x, mod, ln_g[i, 1], ln_b[i, 1], cp_w_in[j].astype(BF16), conv_w[j],
                                 conv_b[j], conv_norm_g[j], conv_norm_b[j], pool_w[j].astype(BF16),
                                 pool_scale[j], cp_w_out[j].astype(BF16))
        else:
            w = _mla_weights(mla_w_dq[j], mla_q_norm_g[j], mla_w_uq[j], mla_w_dkv[j],
                             mla_kv_norm_g[j], mla_w_ukv[j], mla_w_o[j])
            attn_args = (x, mod, ln_g[i, 1], ln_b[i, 1], w)
            y_p, ckv_p, kr_p = _mla_mixer(*attn_args, None, None, latent=False)
            new_ckv.append(ckv_p.reshape(BATCH, SEQ, KV_LORA))
            new_krope.append(kr_p.reshape(BATCH, SEQ, QK_ROPE))
            kr_cache = jnp.pad(cache_mla_krope[:, j], ((0, 0), (0, 0), (0, 128 - QK_ROPE)))
            x = (y_p, _mla_mixer(*attn_args, cache_mla_ckv[:, j], kr_cache, latent=True))
        last = i == DEPTH - 1
        x, w_bf16 = _ffn(x, mod, ln_g[i, 2], ln_b[i, 2], *w_bf16,
                         None if last else ffn_f32, None if last else (i + 1, 0),
                         k0=6, split_out=last)

    y_prompt = x[0].reshape(BATCH, SEQ, D_MODEL)
    y_sample = x[1].reshape(DEC_BATCH, DEC_SEQ, D_MODEL)
    return (y_prompt, y_sample, jnp.stack(new_ckv, axis=1), jnp.stack(new_krope, axis=1))
```

```python
import functools
import math

import jax
import jax.numpy as jnp
import numpy as np
from jax import lax
from jax.experimental import pallas as pl
from jax.experimental.pallas import tpu as pltpu

F32 = jnp.float32
BF16 = jnp.bfloat16

D_MODEL = 1024
BATCH = 32
SEQ = 256
DEPTH = 2
DEC_BATCH = 4
DEC_SEQ = 1024
PAST_LEN = 512
GRID_W = 64
N_MOD = 9
D_FF = 2816
D_CONV = 512
CONV_WIDTH = 31
D_POOL = 512
POOL_WINDOWS = (2, 4, 8, 16)
POOL_GROUP = 128
N_HEADS = 8
QK_NOPE = 128
QK_ROPE = 64
V_DIM = 128
KV_LORA = 256
Q_LORA = 384
ROPE_AXIS_PAIRS = 16
ROPE_BASE = 10000.0
ALPHA = (2 * DEPTH) ** 0.25
LN_EPS = 1e-5
RMS_EPS = 1e-6

N_PROMPT = BATCH * SEQ
N_TOKENS = N_PROMPT + DEC_BATCH * DEC_SEQ
N_GROUPS = 8
HEAD_W = 256
CONV_PAD = 16
CONV_STRIDE = 4
CONV_ROWS = 8 * CONV_STRIDE
CONV_UNROLL = 4

VMEM_LIMIT_BYTES = 60 * 1024 * 1024

FFN_TILES = (1024, 512)
FFN_WORK_BYTES = 4 * 1024 * 1024
FFN_SUB = 256
FFN_UNROLL = 1
MIX_TILE = 1024
MIX_SEG = 256
SEG_ROWS = MIX_SEG + 2 * CONV_PAD
MLA_TILE = 1024
ATTN_Q_TILE = 256


def _layer_norm(z, g, b):
    mu = jnp.mean(z, axis=-1, keepdims=True)
    zc = z - mu
    var = jnp.mean(zc * zc, axis=-1, keepdims=True)
    return zc * lax.rsqrt(var + LN_EPS) * g + b


def _rms_norm(z, g):
    return z * lax.rsqrt(jnp.mean(z * z, axis=-1, keepdims=True) + RMS_EPS) * g


def _silu(z):
    return z * jax.nn.sigmoid(z)


def _dot(a, b):
    return jnp.dot(a, b, preferred_element_type=F32)


def _dot_nt(a, b):
    return lax.dot_general(a, b, (((1,), (1,)), ((), ())), preferred_element_type=F32)


def _const_spec(shape):
    nd = len(shape)
    return pl.BlockSpec(shape, lambda *_: (0,) * nd, pipeline_mode=pl.Buffered(1))


def _group_of_tile(t, tile):
    n_p = N_PROMPT // tile
    per_seq = DEC_SEQ // tile
    return jnp.where(t < n_p, 0, 1 + (t - n_p) // per_seq)


def _params(n_axes=1):
    return pltpu.CompilerParams(
        dimension_semantics=("arbitrary",) * n_axes,
        vmem_limit_bytes=VMEM_LIMIT_BYTES,
    )


def _ada_body(c_ref, w_ref, b_ref, o_ref):
    s = _silu(c_ref[...]).astype(BF16)
    o_ref[...] = _dot(s, w_ref[...].astype(BF16)) + b_ref[...]


def _ada(c_all, w_ada, b_ada):
    tn = D_MODEL
    return pl.pallas_call(
        _ada_body,
        out_shape=jax.ShapeDtypeStruct((DEPTH, N_GROUPS, N_MOD * D_MODEL), F32),
        grid=(DEPTH, N_MOD),
        in_specs=[
            pl.BlockSpec((N_GROUPS, D_MODEL), lambda l, k: (0, 0)),
            pl.BlockSpec((None, D_MODEL, tn), lambda l, k: (l, 0, k)),
            pl.BlockSpec((None, 1, tn), lambda l, k: (l, 0, k)),
        ],
        out_specs=pl.BlockSpec((None, N_GROUPS, tn), lambda l, k: (l, 0, k)),
        compiler_params=_params(2),
        name="ada",
    )(c_all, w_ada, b_ada.reshape(DEPTH, 1, N_MOD * D_MODEL))


def _ffn_body(*refs, k0, split_in, split_out, cast_next):
    refs = list(refs)
    n_x = 2 if split_in else 1
    x_refs, refs = refs[:n_x], refs[n_x:]
    (mod_ref, g_ref, b_ref, w1_ref, w3_ref, w2_ref), refs = refs[:6], refs[6:]
    if cast_next:
        next_f32, refs = refs[:3], refs[3:]
    n_o = 2 if split_out else 1
    o_refs, refs = refs[:n_o], refs[n_o:]
    tile = x_refs[0].shape[0]
    is_prompt = pl.program_id(0) < N_PROMPT // tile
    if cast_next:
        for src, dst in zip(next_f32, refs):
            dst[...] = src[...].astype(BF16)
    shift = mod_ref[k0:k0 + 1, :]
    scale = mod_ref[k0 + 1:k0 + 2, :]
    gate = mod_ref[k0 + 2:k0 + 3, :]
    def sub_tiles(trip, carry):
        outs = []
        for u in range(FFN_UNROLL):
            rows = pl.ds(pl.multiple_of((trip * FFN_UNROLL + u) * FFN_SUB, FFN_SUB), FFN_SUB)
            if split_in:
                x = jnp.where(is_prompt, x_refs[0][rows, :], x_refs[1][rows, :])
            else:
                x = x_refs[0][rows, :]
            h = (x * (1.0 + scale) + shift).astype(BF16)
            a = (_silu(_dot(h, w1_ref[...])) * _dot(h, w3_ref[...])).astype(BF16)
            y = _dot(a, w2_ref[...])
            out = _layer_norm(ALPHA * x + (0.5 * gate) * y, g_ref[...], b_ref[...])
            if split_out:
                outs.append((rows, out))
            else:
                o_refs[0][rows, :] = out
        if split_out:
            @pl.when(is_prompt)
            def _():
                for rows, out in outs:
                    o_refs[0][rows, :] = out

            @pl.when(jnp.logical_not(is_prompt))
            def _():
                for rows, out in outs:
                    o_refs[1][rows, :] = out
        return carry

    lax.fori_loop(0, tile // (FFN_SUB * FFN_UNROLL), sub_tiles, 0)


def _slab_rows(n_rows, n_steps):
    n_slabs = max(d for d in range(1, n_steps + 1) if n_rows % d == 0 and (n_rows // d) % 16 == 0)
    return n_rows // n_slabs, n_slabs


def _ffn_tile(n_in, split_out, next_w):
    for tm in FFN_TILES:
        n_steps = N_TOKENS // tm
        tile_bytes = tm * D_MODEL * 4
        est = 3 * D_MODEL * D_FF * 2
        est += (n_in + (2 if split_out else 1)) * 2 * tile_bytes
        est += FFN_WORK_BYTES + (2 * tile_bytes if split_out else 0)
        if next_w is not None:
            for wn in next_w:
                rows, _ = _slab_rows(wn.shape[2], n_steps)
                est += 2 * rows * wn.shape[3] * (4 + 2)
        if est <= VMEM_LIMIT_BYTES:
            return tm
    raise ValueError("no FFN tile fits in VMEM")


def _ffn(xs, mod, ln_g, ln_b, w1, w3, w2, next_w=None, next_idx=None, *, k0, split_out=False):
    split_in = isinstance(xs, (tuple, list))
    xs = list(xs) if split_in else [xs]
    tm = _ffn_tile(len(xs), split_out, next_w)
    n_steps = N_TOKENS // tm
    n_p = N_PROMPT // tm
    prompt_spec = pl.BlockSpec((tm, D_MODEL), lambda t: (jnp.minimum(t, n_p - 1), 0))
    latent_spec = pl.BlockSpec((tm, D_MODEL), lambda t: (jnp.maximum(t - n_p, 0), 0))
    merged_spec = pl.BlockSpec((tm, D_MODEL), lambda t: (t, 0))
    in_specs = ([prompt_spec, latent_spec] if split_in else [merged_spec]) + [
        pl.BlockSpec((None, N_MOD, D_MODEL), lambda t: (_group_of_tile(t, tm), 0, 0)),
        _const_spec((1, D_MODEL)),
        _const_spec((1, D_MODEL)),
        _const_spec((D_MODEL, D_FF)),
        _const_spec((D_MODEL, D_FF)),
        _const_spec((D_FF, D_MODEL)),
    ]
    args = xs + [mod, ln_g.reshape(1, D_MODEL), ln_b.reshape(1, D_MODEL), w1, w3, w2]
    if split_out:
        out_shape = [jax.ShapeDtypeStruct((N_PROMPT, D_MODEL), F32),
                     jax.ShapeDtypeStruct((N_TOKENS - N_PROMPT, D_MODEL), F32)]
        out_specs = [prompt_spec, latent_spec]
    else:
        out_shape = [jax.ShapeDtypeStruct((N_TOKENS, D_MODEL), F32)]
        out_specs = [merged_spec]
    if next_w is not None:
        li, lj = next_idx
        for wn in next_w:
            n_rows, n_cols = wn.shape[2:]
            rows, n_slabs = _slab_rows(n_rows, n_steps)
            in_specs.append(pl.BlockSpec(
                (None, None, rows, n_cols),
                lambda t, n_slabs=n_slabs: (li, lj, jnp.minimum(t, n_slabs - 1), 0)))
            args.append(wn)
            out_shape.append(jax.ShapeDtypeStruct((n_rows, n_cols), BF16))
            out_specs.append(pl.BlockSpec(
                (rows, n_cols), lambda t, n_slabs=n_slabs: (jnp.minimum(t, n_slabs - 1), 0)))
    outs = pl.pallas_call(
        functools.partial(_ffn_body, k0=k0, split_in=split_in, split_out=split_out,
                          cast_next=next_w is not None),
        out_shape=out_shape,
        grid=(n_steps,),
        in_specs=in_specs,
        out_specs=out_specs,
        compiler_params=_params(),
        name="ffn",
    )(*args)
    n_tok = 2 if split_out else 1
    tok = tuple(outs[:n_tok]) if split_out else outs[0]
    return tok, tuple(outs[n_tok:])


def _strided_rows(start):
    return pl.ds(start, 8, stride=CONV_STRIDE)


def _chunk_rows(j):
    chunks_per_seg = MIX_SEG // CONV_ROWS
    r0 = (j % chunks_per_seg) * CONV_ROWS
    return (j // chunks_per_seg) * SEG_ROWS + r0 + CONV_PAD, j * CONV_ROWS, r0


def _conv_chunk(j, carry, *, slab, taps_w, bias, pad_ref, conv_ref):
    p0, o0, _ = _chunk_rows(j)
    half = CONV_WIDTH // 2
    n_part = 2
    acc = [[None] * n_part for _ in range(CONV_STRIDE)]
    for off in range(-half, half + CONV_STRIDE):
        tap = pad_ref[slab, _strided_rows(p0 + off), :]
        for rho in range(CONV_STRIDE):
            k = off - rho + half
            if 0 <= k < CONV_WIDTH:
                term = taps_w[k] * tap
                part = acc[rho][k % n_part]
                acc[rho][k % n_part] = term if part is None else part + term
    for rho in range(CONV_STRIDE):
        conv_ref[slab, _strided_rows(o0 + rho), :] = (acc[rho][0] + acc[rho][1]) + bias
    return carry


def _pool_chunk(j, carry, *, pad_ref, pool_ref, is_prompt):
    p0, o0, r0 = _chunk_rows(j)
    seq_len = jnp.where(is_prompt, SEQ, DEC_SEQ)
    t0 = jnp.where(is_prompt, r0, o0) + CONV_STRIDE * lax.broadcasted_iota(jnp.int32, (8, 128), 0)
    for gi, w in enumerate(POOL_WINDOWS):
        left = w // 2
        right = w - 1 - left
        slab = D_CONV // 128 + gi
        total = [None] * CONV_STRIDE
        for off in range(-left, right + CONV_STRIDE):
            tap = pad_ref[slab, _strided_rows(p0 + off), :]
            for rho in range(CONV_STRIDE):
                if -left <= off - rho <= right:
                    total[rho] = tap if total[rho] is None else total[rho] + tap
        for rho in range(CONV_STRIDE):
            t = t0 + rho
            cnt = (jnp.minimum(t + right, seq_len - 1) - jnp.maximum(t - left, 0) + 1).astype(F32)
            tok = pad_ref[slab, _strided_rows(p0 + rho), :]
            pool_ref[gi, _strided_rows(o0 + rho), :] = total[rho] / cnt - tok
    return carry


def _cp_body(x_ref, mod_ref, g_ref, b_ref, win_ref, cw_ref, cb_ref, cng_ref, cnb_ref,
             pw_ref, ps_ref, wout_ref, o_ref, pad_ref, conv_ref, pool_ref):
    is_prompt = pl.program_id(0) < N_PROMPT // MIX_TILE
    x = x_ref[...]
    shift = mod_ref[3:4, :]
    scale = mod_ref[4:5, :]
    gate = mod_ref[5:6, :]
    h = (x * (1.0 + scale) + shift).astype(BF16)
    proj = _dot(h, win_ref[...])
    glu = proj[:, :D_CONV] * jax.nn.sigmoid(proj[:, D_CONV:2 * D_CONV])
    hp = proj[:, 2 * D_CONV:]
    n_seg = MIX_TILE // MIX_SEG
    zeros = jnp.zeros((CONV_PAD, 128), F32)
    for slab in range((D_CONV + D_POOL) // 128):
        part = (glu if slab < D_CONV // 128 else hp)[:, (slab % (D_CONV // 128)) * 128:][:, :128]
        for s in range(n_seg):
            lo = s * MIX_SEG
            hi = lo + MIX_SEG
            base = s * SEG_ROWS
            pad_ref[slab, base:base + CONV_PAD, :] = (
                zeros if s == 0 else jnp.where(is_prompt, zeros, part[lo - CONV_PAD:lo]))
            pad_ref[slab, base + CONV_PAD:base + CONV_PAD + MIX_SEG, :] = part[lo:hi]
            pad_ref[slab, base + CONV_PAD + MIX_SEG:base + SEG_ROWS, :] = (
                zeros if s == n_seg - 1 else jnp.where(is_prompt, zeros, part[hi:hi + CONV_PAD]))
    n_chunks = MIX_TILE // CONV_ROWS
    for slab in range(D_CONV // 128):
        lanes = slice(slab * 128, (slab + 1) * 128)
        taps_w = [jnp.broadcast_to(cw_ref[k:k + 1, lanes], (8, 128)) for k in range(CONV_WIDTH)]
        bias = jnp.broadcast_to(cb_ref[:, lanes], (8, 128))
        lax.fori_loop(0, n_chunks, functools.partial(
            _conv_chunk, slab=slab, taps_w=taps_w, bias=bias, pad_ref=pad_ref, conv_ref=conv_ref), 0,
            unroll=CONV_UNROLL)
    lax.fori_loop(0, n_chunks, functools.partial(
        _pool_chunk, pad_ref=pad_ref, pool_ref=pool_ref, is_prompt=is_prompt), 0,
        unroll=CONV_UNROLL)
    conv = jnp.concatenate([conv_ref[c] for c in range(D_CONV // 128)], axis=-1)
    a = _silu(_layer_norm(conv, cng_ref[...], cnb_ref[...]))
    pooled = [_dot(pool_ref[gi].astype(BF16), pw_ref[gi]) for gi in range(len(POOL_WINDOWS))]
    bmix = jnp.concatenate(pooled, axis=-1) * ps_ref[...]
    cat = jnp.concatenate([a, bmix], axis=-1).astype(BF16)
    y = _dot(cat, wout_ref[...])
    o_ref[...] = _layer_norm(ALPHA * x + gate * y, g_ref[...], b_ref[...])


def _conv_pool_mixer(x, mod, ln_g, ln_b, w_in, conv_w, conv_b, cn_g, cn_b, pool_w, pool_scale, w_out):
    tm = MIX_TILE
    assert tm == DEC_SEQ and MIX_SEG == SEQ and N_PROMPT % tm == 0
    row_spec = pl.BlockSpec((tm, D_MODEL), lambda t: (t, 0))
    in_specs = [
        row_spec,
        pl.BlockSpec((None, N_MOD, D_MODEL), lambda t: (_group_of_tile(t, tm), 0, 0)),
        _const_spec((1, D_MODEL)),
        _const_spec((1, D_MODEL)),
        _const_spec((D_MODEL, 2 * D_CONV + D_POOL)),
        _const_spec((CONV_WIDTH, D_CONV)),
        _const_spec((1, D_CONV)),
        _const_spec((1, D_CONV)),
        _const_spec((1, D_CONV)),
        _const_spec((len(POOL_WINDOWS), POOL_GROUP, POOL_GROUP)),
        _const_spec((1, D_POOL)),
        _const_spec((D_CONV + D_POOL, D_MODEL)),
    ]
    args = [x, mod, ln_g.reshape(1, D_MODEL), ln_b.reshape(1, D_MODEL), w_in, conv_w,
            conv_b.reshape(1, D_CONV), cn_g.reshape(1, D_CONV), cn_b.reshape(1, D_CONV),
            pool_w, pool_scale.reshape(1, D_POOL), w_out]
    return pl.pallas_call(
        _cp_body,
        out_shape=jax.ShapeDtypeStruct((N_TOKENS, D_MODEL), F32),
        grid=(N_TOKENS // tm,),
        in_specs=in_specs,
        out_specs=row_spec,
        scratch_shapes=[
            pltpu.VMEM(((D_CONV + D_POOL) // 128, tm // MIX_SEG * SEG_ROWS, 128), F32),
            pltpu.VMEM((D_CONV // 128, tm, 128), F32),
            pltpu.VMEM((D_POOL // 128, tm, 128), F32),
        ],
        compiler_params=_params(),
        name="conv_pool",
    )(*args)


def _attend(q, keys, values):
    scores = [_dot_nt(q, k) for k in keys]
    m = functools.reduce(jnp.maximum, [jnp.max(s, axis=-1, keepdims=True) for s in scores])
    c = (QK_NOPE + QK_ROPE) ** -0.5 * math.log2(math.e)
    e = [jnp.exp2((s - m) * c) for s in scores]
    denom = functools.reduce(jnp.add, [jnp.sum(ei, axis=-1, keepdims=True) for ei in e])
    out = functools.reduce(jnp.add, [_dot(ei.astype(BF16), v) for ei, v in zip(e, values)])
    return out * (1.0 / denom)


def _mla_project(x, mod_ref, wdkv_ref, kvg_ref, wdq_ref, qg_ref, wuq_ref):
    h = (x * (1.0 + mod_ref[4:5, :]) + mod_ref[3:4, :]).astype(BF16)
    kv = _dot(h, wdkv_ref[...])
    ckv = _rms_norm(kv[:, :KV_LORA], kvg_ref[...])
    qd = _rms_norm(_dot(h, wdq_ref[...]), qg_ref[...]).astype(BF16)
    return kv, ckv, qd, _dot(qd, wuq_ref[...])


def _mla_out(x, attn, mod_ref, g_ref, b_ref, wo_ref):
    y = _dot(attn, wo_ref[...])
    return _layer_norm(ALPHA * x + mod_ref[5:6, :] * y, g_ref[...], b_ref[...])


def _mla_prompt_body(x_ref, mod_ref, g_ref, b_ref, wdkv_ref, kvg_ref, wdq_ref, qg_ref, wuq_ref,
                     wuk_ref, wuv_ref, wo_ref, o_ref, ckv_ref, kr_ref):
    for s in range(x_ref.shape[0] // SEQ):
        rows = slice(s * SEQ, (s + 1) * SEQ)
        x = x_ref[rows, :]
        kv, ckv, _, q = _mla_project(x, mod_ref, wdkv_ref, kvg_ref, wdq_ref, qg_ref, wuq_ref)
        ckv_ref[rows, :] = ckv
        kr_ref[rows, :] = kv[:, KV_LORA:KV_LORA + QK_ROPE]
        q16 = q.astype(BF16)
        ckv16 = ckv.astype(BF16)
        kn16 = _dot(ckv16, wuk_ref[...]).astype(BF16)
        v16 = _dot(ckv16, wuv_ref[...]).astype(BF16)
        kr16 = kv[:, KV_LORA:KV_LORA + 128].astype(BF16)
        heads = []
        for hd in range(N_HEADS):
            vl = slice(hd * V_DIM, (hd + 1) * V_DIM)
            k_h = jnp.concatenate([kn16[:, hd * QK_NOPE:(hd + 1) * QK_NOPE], kr16], axis=-1)
            heads.append(_attend(q16[:, hd * HEAD_W:(hd + 1) * HEAD_W], [k_h], [v16[:, vl]]).astype(BF16))
        attn = jnp.concatenate(heads, axis=-1)
        o_ref[rows, :] = _mla_out(x, attn, mod_ref, g_ref, b_ref, wo_ref)


def _store_keys(k_ref, rows, kn16, kr16):
    for hd in range(N_HEADS):
        k_ref[rows, hd * HEAD_W:hd * HEAD_W + QK_NOPE] = kn16[:, hd * QK_NOPE:(hd + 1) * QK_NOPE]
        k_ref[rows, hd * HEAD_W + QK_NOPE:(hd + 1) * HEAD_W] = kr16


def _mla_latent_body(x_ref, mod_ref, g_ref, b_ref, wdkv_ref, kvg_ref, wdq_ref, qg_ref, wuq_ref,
                     wuk_ref, wuv_ref, wo_ref, wuqs_ref, cos_ref, sin_ref, cckv_ref, ckr_ref,
                     o_ref, q_s, k_s, v_s, kc_s, vc_s):
    c16 = cckv_ref[...].astype(BF16)
    vc_s[...] = _dot(c16, wuv_ref[...]).astype(BF16)
    _store_keys(kc_s, slice(None), _dot(c16, wuk_ref[...]).astype(BF16), ckr_ref[...].astype(BF16))
    for s in range(DEC_SEQ // ATTN_Q_TILE):
        rows = slice(s * ATTN_Q_TILE, (s + 1) * ATTN_Q_TILE)
        cos = cos_ref[rows, :]
        sin = sin_ref[rows, :]
        kv, ckv, qd, q = _mla_project(x_ref[rows, :], mod_ref, wdkv_ref, kvg_ref, wdq_ref, qg_ref, wuq_ref)
        q_swapped = _dot(qd, wuqs_ref[...])
        kr = kv[:, KV_LORA:KV_LORA + 128] * cos + kv[:, KV_LORA + 128:KV_LORA + 256] * sin
        ckv16 = ckv.astype(BF16)
        v_s[rows, :] = _dot(ckv16, wuv_ref[...]).astype(BF16)
        _store_keys(k_s, rows, _dot(ckv16, wuk_ref[...]).astype(BF16), kr.astype(BF16))
        for hd in range(N_HEADS):
            lo = hd * HEAD_W
            q_s[rows, lo:lo + QK_NOPE] = q[:, lo:lo + QK_NOPE].astype(BF16)
            q_rope = q[:, lo + QK_NOPE:lo + HEAD_W] * cos + q_swapped[:, hd * 128:(hd + 1) * 128] * sin
            q_s[rows, lo + QK_NOPE:lo + HEAD_W] = q_rope.astype(BF16)

    def q_block(i, carry):
        rows = pl.ds(pl.multiple_of(i * ATTN_Q_TILE, ATTN_Q_TILE), ATTN_Q_TILE)
        heads = []
        for hd in range(N_HEADS):
            ql = slice(hd * HEAD_W, (hd + 1) * HEAD_W)
            vl = slice(hd * V_DIM, (hd + 1) * V_DIM)
            heads.append(_attend(q_s[rows, ql], [k_s[:, ql], kc_s[:, ql]],
                                 [v_s[:, vl], vc_s[:, vl]]).astype(BF16))
        attn = jnp.concatenate(heads, axis=-1)
        o_ref[rows, :] = _mla_out(x_ref[rows, :], attn, mod_ref, g_ref, b_ref, wo_ref)
        return carry

    lax.fori_loop(0, DEC_SEQ // ATTN_Q_TILE, q_block, 0)


def _mla_mixer(x, mod, ln_g, ln_b, w, cache_ckv, cache_kr, *, latent):
    tile = MLA_TILE
    first = N_PROMPT // tile if latent else 0
    n_rows = DEC_BATCH * DEC_SEQ if latent else N_PROMPT
    dkv = w["dkv"] if latent else w["dkv"][:, :KV_LORA + 128]
    in_specs = [
        pl.BlockSpec((tile, D_MODEL), lambda t: (first + t, 0)),
        pl.BlockSpec((None, N_MOD, D_MODEL), lambda t: (_group_of_tile(first + t, tile), 0, 0)),
        _const_spec((1, D_MODEL)),
        _const_spec((1, D_MODEL)),
        _const_spec(dkv.shape),
        _const_spec((1, KV_LORA)),
        _const_spec((D_MODEL, Q_LORA)),
        _const_spec((1, Q_LORA)),
        _const_spec((Q_LORA, N_HEADS * HEAD_W)),
        _const_spec((KV_LORA, N_HEADS * QK_NOPE)),
        _const_spec((KV_LORA, N_HEADS * V_DIM)),
        _const_spec((N_HEADS * V_DIM, D_MODEL)),
    ]
    args = [x, mod, ln_g.reshape(1, D_MODEL), ln_b.reshape(1, D_MODEL), dkv, w["kv_g"], w["dq"],
            w["q_g"], w["uq"], w["uk"], w["uv"], w["o"]]
    out_shape = [jax.ShapeDtypeStruct((n_rows, D_MODEL), F32)]
    out_specs = [pl.BlockSpec((tile, D_MODEL), lambda t: (t, 0))]
    scratch = []
    if latent:
        assert tile == DEC_SEQ
        in_specs += [
            _const_spec((Q_LORA, N_HEADS * 128)),
            _const_spec((DEC_SEQ, 128)),
            _const_spec((DEC_SEQ, 128)),
            pl.BlockSpec((None, PAST_LEN, KV_LORA), lambda t: (t, 0, 0)),
            pl.BlockSpec((None, PAST_LEN, 128), lambda t: (t, 0, 0)),
        ]
        args += [w["uq_swapped"], w["cos"], w["sin"], cache_ckv, cache_kr]
        scratch = [
            pltpu.VMEM((DEC_SEQ, N_HEADS * HEAD_W), BF16),
            pltpu.VMEM((DEC_SEQ, N_HEADS * HEAD_W), BF16),
            pltpu.VMEM((DEC_SEQ, N_HEADS * V_DIM), BF16),
            pltpu.VMEM((PAST_LEN, N_HEADS * HEAD_W), BF16),
            pltpu.VMEM((PAST_LEN, N_HEADS * V_DIM), BF16),
        ]
    else:
        out_shape += [jax.ShapeDtypeStruct((n_rows, KV_LORA), F32),
                      jax.ShapeDtypeStruct((n_rows, QK_ROPE), F32)]
        out_specs += [pl.BlockSpec((tile, KV_LORA), lambda t: (t, 0)),
                      pl.BlockSpec((tile, QK_ROPE), lambda t: (t, 0))]
    outs = pl.pallas_call(
        _mla_latent_body if latent else _mla_prompt_body,
        out_shape=out_shape,
        grid=(n_rows // tile,),
        in_specs=in_specs,
        out_specs=out_specs,
        scratch_shapes=scratch,
        compiler_params=_params(),
        name="mla_latent" if latent else "mla_prompt",
    )(*args)
    return outs[0] if latent else outs


def _rope_swap_perm():
    idx = np.arange(QK_ROPE)
    return np.where((idx % 32) < ROPE_AXIS_PAIRS, idx + ROPE_AXIS_PAIRS, idx - ROPE_AXIS_PAIRS)


def _rope_tables():
    n = DEC_SEQ
    row = jnp.repeat(jnp.arange(n // GRID_W), GRID_W)
    col = jnp.tile(jnp.arange(GRID_W), n // GRID_W)
    inv = ROPE_BASE ** (-jnp.arange(ROPE_AXIS_PAIRS, dtype=F32) / ROPE_AXIS_PAIRS)
    ar = row[:, None] * inv
    ac = col[:, None] * inv
    zeros = jnp.zeros((n, 128 - QK_ROPE), F32)
    cos = jnp.concatenate([jnp.cos(ar), jnp.cos(ar), jnp.cos(ac), jnp.cos(ac), zeros], axis=-1)
    sin = jnp.concatenate([-jnp.sin(ar), jnp.sin(ar), -jnp.sin(ac), jnp.sin(ac), zeros], axis=-1)
    return cos, sin


def _mla_weights(w_dq, q_norm_g, w_uq, w_dkv, kv_norm_g, w_ukv, w_o):
    swap = _rope_swap_perm()
    z64 = jnp.zeros((D_MODEL, 128 - QK_ROPE), F32)
    kr_cols = w_dkv[:, KV_LORA:]
    dkv = jnp.concatenate([w_dkv[:, :KV_LORA], kr_cols, z64, kr_cols[:, swap], z64], axis=-1)
    uq = w_uq.reshape(Q_LORA, N_HEADS, QK_NOPE + QK_ROPE)
    zq = jnp.zeros((Q_LORA, N_HEADS, 128 - QK_ROPE), F32)
    uq_main = jnp.concatenate([uq, zq], axis=-1).reshape(Q_LORA, N_HEADS * HEAD_W)
    uq_swapped = jnp.concatenate([uq[:, :, QK_NOPE:][:, :, swap], zq], axis=-1).reshape(Q_LORA, N_HEADS * 128)
    ukv = w_ukv.reshape(KV_LORA, N_HEADS, QK_NOPE + V_DIM)
    cos, sin = _rope_tables()
    return {
        "dkv": dkv.astype(BF16),
        "kv_g": kv_norm_g.reshape(1, KV_LORA),
        "dq": w_dq.astype(BF16),
        "q_g": q_norm_g.reshape(1, Q_LORA),
        "uq": uq_main.astype(BF16),
        "uq_swapped": uq_swapped.astype(BF16),
        "uk": ukv[:, :, :QK_NOPE].reshape(KV_LORA, N_HEADS * QK_NOPE).astype(BF16),
        "uv": ukv[:, :, QK_NOPE:].reshape(KV_LORA, N_HEADS * V_DIM).astype(BF16),
        "o": w_o.astype(BF16),
        "cos": cos,
        "sin": sin,
    }


def kernel(x_prompt, x_sample, cache_mla_ckv, cache_mla_krope, c, c_ctx, w_ada, b_ada, ln_g, ln_b, ffn_w1, ffn_w3, ffn_w2, cp_w_in, conv_w, conv_b, conv_norm_g, conv_norm_b, pool_w, pool_scale, cp_w_out, mla_w_dq, mla_q_norm_g, mla_w_uq, mla_w_dkv, mla_kv_norm_g, mla_w_ukv, mla_w_o):
    c_all = jnp.concatenate(
        [c_ctx[None, :], c, jnp.zeros((N_GROUPS - 1 - DEC_BATCH, D_MODEL), F32)], axis=0)
    ada = _ada(c_all, w_ada, b_ada).reshape(DEPTH, N_GROUPS, N_MOD, D_MODEL)

    x = (x_prompt.reshape(N_PROMPT, D_MODEL), x_sample.reshape(DEC_BATCH * DEC_SEQ, D_MODEL))
    ffn_f32 = (ffn_w1, ffn_w3, ffn_w2)
    w_bf16 = tuple(wf[0, 0].astype(BF16) for wf in ffn_f32)
    new_ckv, new_krope = [], []
    for i in range(DEPTH):
        mod = ada[i]
        j = i // 2
        x, w_bf16 = _ffn(x, mod, ln_g[i, 0], ln_b[i, 0], *w_bf16, ffn_f32, (i, 1), k0=0)
        if i % 2 == 0:
            x = _conv_pool_mixer(x, mod, ln_g[i, 1], ln_b[i, 1], cp_w_in[j].astype(BF16), conv_w[j],
                                 conv_b[j], conv_norm_g[j], conv_norm_b[j], pool_w[j].astype(BF16),
                                 pool_scale[j], cp_w_out[j].astype(BF16))
        else:
            w = _mla_weights(mla_w_dq[j], mla_q_norm_g[j], mla_w_uq[j], mla_w_dkv[j],
                             mla_kv_norm_g[j], mla_w_ukv[j], mla_w_o[j])
            attn_args = (x, mod, ln_g[i, 1], ln_b[i, 1], w)
            y_p, ckv_p, kr_p = _mla_mixer(*attn_args, None, None, latent=False)
            new_ckv.append(ckv_p.reshape(BATCH, SEQ, KV_LORA))
            new_krope.append(kr_p.reshape(BATCH, SEQ, QK_ROPE))
            kr_cache = jnp.pad(cache_mla_krope[:, j], ((0, 0), (0, 0), (0, 128 - QK_ROPE)))
            x = (y_p, _mla_mixer(*attn_args, cache_mla_ckv[:, j], kr_cache, latent=True))
        last = i == DEPTH - 1
        x, w_bf16 = _ffn(x, mod, ln_g[i, 2], ln_b[i, 2], *w_bf16,
                         None if last else ffn_f32, None if last else (i + 1, 0),
                         k0=6, split_out=last)

    y_prompt = x[0].reshape(BATCH, SEQ, D_MODEL)
    y_sample = x[1].reshape(DEC_BATCH, DEC_SEQ, D_MODEL)
    return (y_prompt, y_sample, jnp.stack(new_ckv, axis=1), jnp.stack(new_krope, axis=1))
```

```python
import functools
import math

import jax
import jax.numpy as jnp
import numpy as np
from jax import lax
from jax.experimental import pallas as pl
from jax.experimental.pallas import tpu as pltpu

F32 = jnp.float32
BF16 = jnp.bfloat16

D_MODEL = 1024
BATCH = 32
SEQ = 256
DEPTH = 2
DEC_BATCH = 4
DEC_SEQ = 1024
PAST_LEN = 512
GRID_W = 64
N_MOD = 9
D_FF = 2816
D_CONV = 512
CONV_WIDTH = 31
D_POOL = 512
POOL_WINDOWS = (2, 4, 8, 16)
POOL_GROUP = 128
N_HEADS = 8
QK_NOPE = 128
QK_ROPE = 64
V_DIM = 128
KV_LORA = 256
Q_LORA = 384
ROPE_AXIS_PAIRS = 16
ROPE_BASE = 10000.0
ALPHA = (2 * DEPTH) ** 0.25
LN_EPS = 1e-5
RMS_EPS = 1e-6

N_PROMPT = BATCH * SEQ
N_TOKENS = N_PROMPT + DEC_BATCH * DEC_SEQ
N_GROUPS = 8
HEAD_W = 256
CONV_PAD = 16
CONV_STRIDE = 4
CONV_ROWS = 8 * CONV_STRIDE
CONV_UNROLL = 4

VMEM_LIMIT_BYTES = 60 * 1024 * 1024

FFN_TILES = (1024, 512)
FFN_WORK_BYTES = 4 * 1024 * 1024
FFN_SUB = 256
FFN_UNROLL = 4
MIX_TILE = 1024
MIX_SEG = 256
SEG_ROWS = MIX_SEG + 2 * CONV_PAD
MLA_TILE = 1024
ATTN_Q_TILE = 256


def _layer_norm(z, g, b):
    mu = jnp.mean(z, axis=-1, keepdims=True)
    zc = z - mu
    var = jnp.mean(zc * zc, axis=-1, keepdims=True)
    return zc * lax.rsqrt(var + LN_EPS) * g + b


def _rms_norm(z, g):
    return z * lax.rsqrt(jnp.mean(z * z, axis=-1, keepdims=True) + RMS_EPS) * g


def _silu(z):
    return z * jax.nn.sigmoid(z)


def _dot(a, b):
    return jnp.dot(a, b, preferred_element_type=F32)


def _dot_nt(a, b):
    return lax.dot_general(a, b, (((1,), (1,)), ((), ())), preferred_element_type=F32)


def _const_spec(shape):
    nd = len(shape)
    return pl.BlockSpec(shape, lambda *_: (0,) * nd, pipeline_mode=pl.Buffered(1))


def _group_of_tile(t, tile):
    n_p = N_PROMPT // tile
    per_seq = DEC_SEQ // tile
    return jnp.where(t < n_p, 0, 1 + (t - n_p) // per_seq)


def _params(n_axes=1):
    return pltpu.CompilerParams(
        dimension_semantics=("arbitrary",) * n_axes,
        vmem_limit_bytes=VMEM_LIMIT_BYTES,
    )


def _ada_body(c_ref, w_ref, b_ref, o_ref):
    s = _silu(c_ref[...]).astype(BF16)
    o_ref[...] = _dot(s, w_ref[...].astype(BF16)) + b_ref[...]


def _ada(c_all, w_ada, b_ada):
    tn = D_MODEL
    return pl.pallas_call(
        _ada_body,
        out_shape=jax.ShapeDtypeStruct((DEPTH, N_GROUPS, N_MOD * D_MODEL), F32),
        grid=(DEPTH, N_MOD),
        in_specs=[
            pl.BlockSpec((N_GROUPS, D_MODEL), lambda l, k: (0, 0)),
            pl.BlockSpec((None, D_MODEL, tn), lambda l, k: (l, 0, k)),
            pl.BlockSpec((None, 1, tn), lambda l, k: (l, 0, k)),
        ],
        out_specs=pl.BlockSpec((None, N_GROUPS, tn), lambda l, k: (l, 0, k)),
        compiler_params=_params(2),
        name="ada",
    )(c_all, w_ada, b_ada.reshape(DEPTH, 1, N_MOD * D_MODEL))


def _ffn_body(*refs, k0, split_in, split_out, cast_next):
    refs = list(refs)
    n_x = 2 if split_in else 1
    x_refs, refs = refs[:n_x], refs[n_x:]
    (mod_ref, g_ref, b_ref, w1_ref, w3_ref, w2_ref), refs = refs[:6], refs[6:]
    if cast_next:
        next_f32, refs = refs[:3], refs[3:]
    n_o = 2 if split_out else 1
    o_refs, refs = refs[:n_o], refs[n_o:]
    tile = x_refs[0].shape[0]
    is_prompt = pl.program_id(0) < N_PROMPT // tile
    if cast_next:
        for src, dst in zip(next_f32, refs):
            dst[...] = src[...].astype(BF16)
    shift = mod_ref[k0:k0 + 1, :]
    scale = mod_ref[k0 + 1:k0 + 2, :]
    gate = mod_ref[k0 + 2:k0 + 3, :]
    n_sub = tile // FFN_SUB
    unroll = min(FFN_UNROLL, n_sub)

    def sub_tiles(trip, carry):
        outs = []
        for u in range(unroll):
            start = (trip * unroll + u) * FFN_SUB
            rows = pl.ds(start if isinstance(start, int) else pl.multiple_of(start, FFN_SUB), FFN_SUB)
            if split_in:
                x = jnp.where(is_prompt, x_refs[0][rows, :], x_refs[1][rows, :])
            else:
                x = x_refs[0][rows, :]
            h = (x * (1.0 + scale) + shift).astype(BF16)
            a = (_silu(_dot(h, w1_ref[...])) * _dot(h, w3_ref[...])).astype(BF16)
            y = _dot(a, w2_ref[...])
            out = _layer_norm(ALPHA * x + (0.5 * gate) * y, g_ref[...], b_ref[...])
            if split_out:
                outs.append((rows, out))
            else:
                o_refs[0][rows, :] = out
        if split_out:
            @pl.when(is_prompt)
            def _():
                for rows, out in outs:
                    o_refs[0][rows, :] = out

            @pl.when(jnp.logical_not(is_prompt))
            def _():
                for rows, out in outs:
                    o_refs[1][rows, :] = out
        return carry

    if n_sub == unroll:
        sub_tiles(0, 0)
    else:
        lax.fori_loop(0, n_sub // unroll, sub_tiles, 0)


def _slab_rows(n_rows, n_steps):
    n_slabs = max(d for d in range(1, n_steps + 1) if n_rows % d == 0 and (n_rows // d) % 16 == 0)
    return n_rows // n_slabs, n_slabs


def _ffn_tile(n_in, split_out, next_w):
    for tm in FFN_TILES:
        n_steps = N_TOKENS // tm
        tile_bytes = tm * D_MODEL * 4
        est = 3 * D_MODEL * D_FF * 2
        est += (n_in + (2 if split_out else 1)) * 2 * tile_bytes
        est += FFN_WORK_BYTES + (2 * tile_bytes if split_out else 0)
        if next_w is not None:
            for wn in next_w:
                rows, _ = _slab_rows(wn.shape[2], n_steps)
                est += 2 * rows * wn.shape[3] * (4 + 2)
        if est <= VMEM_LIMIT_BYTES:
            return tm
    raise ValueError("no FFN tile fits in VMEM")


def _ffn(xs, mod, ln_g, ln_b, w1, w3, w2, next_w=None, next_idx=None, *, k0, split_out=False):
    split_in = isinstance(xs, (tuple, list))
    xs = list(xs) if split_in else [xs]
    tm = _ffn_tile(len(xs), split_out, next_w)
    n_steps = N_TOKENS // tm
    n_p = N_PROMPT // tm
    prompt_spec = pl.BlockSpec((tm, D_MODEL), lambda t: (jnp.minimum(t, n_p - 1), 0))
    latent_spec = pl.BlockSpec((tm, D_MODEL), lambda t: (jnp.maximum(t - n_p, 0), 0))
    merged_spec = pl.BlockSpec((tm, D_MODEL), lambda t: (t, 0))
    in_specs = ([prompt_spec, latent_spec] if split_in else [merged_spec]) + [
        pl.BlockSpec((None, N_MOD, D_MODEL), lambda t: (_group_of_tile(t, tm), 0, 0)),
        _const_spec((1, D_MODEL)),
        _const_spec((1, D_MODEL)),
        _const_spec((D_MODEL, D_FF)),
        _const_spec((D_MODEL, D_FF)),
        _const_spec((D_FF, D_MODEL)),
    ]
    args = xs + [mod, ln_g.reshape(1, D_MODEL), ln_b.reshape(1, D_MODEL), w1, w3, w2]
    if split_out:
        out_shape = [jax.ShapeDtypeStruct((N_PROMPT, D_MODEL), F32),
                     jax.ShapeDtypeStruct((N_TOKENS - N_PROMPT, D_MODEL), F32)]
        out_specs = [prompt_spec, latent_spec]
    else:
        out_shape = [jax.ShapeDtypeStruct((N_TOKENS, D_MODEL), F32)]
        out_specs = [merged_spec]
    if next_w is not None:
        li, lj = next_idx
        for wn in next_w:
            n_rows, n_cols = wn.shape[2:]
            rows, n_slabs = _slab_rows(n_rows, n_steps)
            in_specs.append(pl.BlockSpec(
                (None, None, rows, n_cols),
                lambda t, n_slabs=n_slabs: (li, lj, jnp.minimum(t, n_slabs - 1), 0)))
            args.append(wn)
            out_shape.append(jax.ShapeDtypeStruct((n_rows, n_cols), BF16))
            out_specs.append(pl.BlockSpec(
                (rows, n_cols), lambda t, n_slabs=n_slabs: (jnp.minimum(t, n_slabs - 1), 0)))
    outs = pl.pallas_call(
        functools.partial(_ffn_body, k0=k0, split_in=split_in, split_out=split_out,
                          cast_next=next_w is not None),
        out_shape=out_shape,
        grid=(n_steps,),
        in_specs=in_specs,
        out_specs=out_specs,
        compiler_params=_params(),
        name="ffn",
    )(*args)
    n_tok = 2 if split_out else 1
    tok = tuple(outs[:n_tok]) if split_out else outs[0]
    return tok, tuple(outs[n_tok:])


def _strided_rows(start):
    return pl.ds(start, 8, stride=CONV_STRIDE)


def _chunk_rows(j):
    chunks_per_seg = MIX_SEG // CONV_ROWS
    r0 = (j % chunks_per_seg) * CONV_ROWS
    return (j // chunks_per_seg) * SEG_ROWS + r0 + CONV_PAD, j * CONV_ROWS, r0


def _conv_chunk(j, carry, *, slab, taps_w, bias, pad_ref, conv_ref):
    p0, o0, _ = _chunk_rows(j)
    half = CONV_WIDTH // 2
    n_part = 2
    acc = [[None] * n_part for _ in range(CONV_STRIDE)]
    for off in range(-half, half + CONV_STRIDE):
        tap = pad_ref[slab, _strided_rows(p0 + off), :]
        for rho in range(CONV_STRIDE):
            k = off - rho + half
            if 0 <= k < CONV_WIDTH:
                term = taps_w[k] * tap
                part = acc[rho][k % n_part]
                acc[rho][k % n_part] = term if part is None else part + term
    for rho in range(CONV_STRIDE):
        conv_ref[slab, _strided_rows(o0 + rho), :] = (acc[rho][0] + acc[rho][1]) + bias
    return carry


def _pool_chunk(j, carry, *, pad_ref, pool_ref, is_prompt):
    p0, o0, r0 = _chunk_rows(j)
    seq_len = jnp.where(is_prompt, SEQ, DEC_SEQ)
    t0 = jnp.where(is_prompt, r0, o0) + CONV_STRIDE * lax.broadcasted_iota(jnp.int32, (8, 128), 0)
    for gi, w in enumerate(POOL_WINDOWS):
        left = w // 2
        right = w - 1 - left
        slab = D_CONV // 128 + gi
        total = [None] * CONV_STRIDE
        for off in range(-left, right + CONV_STRIDE):
            tap = pad_ref[slab, _strided_rows(p0 + off), :]
            for rho in range(CONV_STRIDE):
                if -left <= off - rho <= right:
                    total[rho] = tap if total[rho] is None else total[rho] + tap
        for rho in range(CONV_STRIDE):
            t = t0 + rho
            cnt = (jnp.minimum(t + right, seq_len - 1) - jnp.maximum(t - left, 0) + 1).astype(F32)
            tok = pad_ref[slab, _strided_rows(p0 + rho), :]
            pool_ref[gi, _strided_rows(o0 + rho), :] = total[rho] / cnt - tok
    return carry


def _cp_body(x_ref, mod_ref, g_ref, b_ref, win_ref, cw_ref, cb_ref, cng_ref, cnb_ref,
             pw_ref, ps_ref, wout_ref, o_ref, pad_ref, conv_ref, pool_ref):
    is_prompt = pl.program_id(0) < N_PROMPT // MIX_TILE
    shift = mod_ref[3:4, :]
    scale = mod_ref[4:5, :]
    gate = mod_ref[5:6, :]
    n_seg = MIX_TILE // MIX_SEG
    n_slab = (D_CONV + D_POOL) // 128
    feats = []
    for s in range(n_seg):
        x = x_ref[s * MIX_SEG:(s + 1) * MIX_SEG, :]
        h = (x * (1.0 + scale) + shift).astype(BF16)
        proj = _dot(h, win_ref[...])
        glu = proj[:, :D_CONV] * jax.nn.sigmoid(proj[:, D_CONV:2 * D_CONV])
        feats.append(jnp.concatenate([glu, proj[:, 2 * D_CONV:]], axis=-1))
    zeros = jnp.zeros((CONV_PAD, 128), F32)
    for s in range(n_seg):
        base = s * SEG_ROWS
        for slab in range(n_slab):
            lanes = slice(slab * 128, (slab + 1) * 128)
            pad_ref[slab, base:base + CONV_PAD, :] = (
                zeros if s == 0 else jnp.where(is_prompt, zeros, feats[s - 1][MIX_SEG - CONV_PAD:, lanes]))
            pad_ref[slab, base + CONV_PAD:base + CONV_PAD + MIX_SEG, :] = feats[s][:, lanes]
            pad_ref[slab, base + CONV_PAD + MIX_SEG:base + SEG_ROWS, :] = (
                zeros if s == n_seg - 1 else jnp.where(is_prompt, zeros, feats[s + 1][:CONV_PAD, lanes]))
    taps_w = [[jnp.broadcast_to(cw_ref[k:k + 1, slab * 128:(slab + 1) * 128], (8, 128))
               for k in range(CONV_WIDTH)] for slab in range(D_CONV // 128)]
    bias = [jnp.broadcast_to(cb_ref[:, slab * 128:(slab + 1) * 128], (8, 128))
            for slab in range(D_CONV // 128)]
    chunks_per_seg = MIX_SEG // CONV_ROWS
    for s in range(n_seg):
        rows = slice(s * MIX_SEG, (s + 1) * MIX_SEG)
        seg_chunks = range(s * chunks_per_seg, (s + 1) * chunks_per_seg)
        for slab in range(D_CONV // 128):
            for j in seg_chunks:
                _conv_chunk(j, 0, slab=slab, taps_w=taps_w[slab], bias=bias[slab],
                            pad_ref=pad_ref, conv_ref=conv_ref)
        for j in seg_chunks:
            _pool_chunk(j, 0, pad_ref=pad_ref, pool_ref=pool_ref, is_prompt=is_prompt)
        conv = jnp.concatenate([conv_ref[c, rows, :] for c in range(D_CONV // 128)], axis=-1)
        a = _silu(_layer_norm(conv, cng_ref[...], cnb_ref[...]))
        pooled = [_dot(pool_ref[gi, rows, :].astype(BF16), pw_ref[gi]) for gi in range(len(POOL_WINDOWS))]
        bmix = jnp.concatenate(pooled, axis=-1) * ps_ref[...]
        cat = jnp.concatenate([a, bmix], axis=-1).astype(BF16)
        y = _dot(cat, wout_ref[...])
        o_ref[rows, :] = _layer_norm(ALPHA * x_ref[rows, :] + gate * y, g_ref[...], b_ref[...])


def _conv_pool_mixer(x, mod, ln_g, ln_b, w_in, conv_w, conv_b, cn_g, cn_b, pool_w, pool_scale, w_out):
    tm = MIX_TILE
    assert tm == DEC_SEQ and MIX_SEG == SEQ and N_PROMPT % tm == 0
    row_spec = pl.BlockSpec((tm, D_MODEL), lambda t: (t, 0))
    in_specs = [
        row_spec,
        pl.BlockSpec((None, N_MOD, D_MODEL), lambda t: (_group_of_tile(t, tm), 0, 0)),
        _const_spec((1, D_MODEL)),
        _const_spec((1, D_MODEL)),
        _const_spec((D_MODEL, 2 * D_CONV + D_POOL)),
        _const_spec((CONV_WIDTH, D_CONV)),
        _const_spec((1, D_CONV)),
        _const_spec((1, D_CONV)),
        _const_spec((1, D_CONV)),
        _const_spec((len(POOL_WINDOWS), POOL_GROUP, POOL_GROUP)),
        _const_spec((1, D_POOL)),
        _const_spec((D_CONV + D_POOL, D_MODEL)),
    ]
    args = [x, mod, ln_g.reshape(1, D_MODEL), ln_b.reshape(1, D_MODEL), w_in, conv_w,
            conv_b.reshape(1, D_CONV), cn_g.reshape(1, D_CONV), cn_b.reshape(1, D_CONV),
            pool_w, pool_scale.reshape(1, D_POOL), w_out]
    return pl.pallas_call(
        _cp_body,
        out_shape=jax.ShapeDtypeStruct((N_TOKENS, D_MODEL), F32),
        grid=(N_TOKENS // tm,),
        in_specs=in_specs,
        out_specs=row_spec,
        scratch_shapes=[
            pltpu.VMEM(((D_CONV + D_POOL) // 128, tm // MIX_SEG * SEG_ROWS, 128), F32),
            pltpu.VMEM((D_CONV // 128, tm, 128), F32),
            pltpu.VMEM((D_POOL // 128, tm, 128), F32),
        ],
        compiler_params=_params(),
        name="conv_pool",
    )(*args)


def _attend(q, keys, values):
    scores = [_dot_nt(q, k) for k in keys]
    m = functools.reduce(jnp.maximum, [jnp.max(s, axis=-1, keepdims=True) for s in scores])
    c = (QK_NOPE + QK_ROPE) ** -0.5 * math.log2(math.e)
    e = [jnp.exp2((s - m) * c) for s in scores]
    denom = functools.reduce(jnp.add, [jnp.sum(ei, axis=-1, keepdims=True) for ei in e])
    out = functools.reduce(jnp.add, [_dot(ei.astype(BF16), v) for ei, v in zip(e, values)])
    return out * (1.0 / denom)


def _mla_project(x, mod_ref, wdkv_ref, kvg_ref, wdq_ref, qg_ref, wuq_ref):
    h = (x * (1.0 + mod_ref[4:5, :]) + mod_ref[3:4, :]).astype(BF16)
    kv = _dot(h, wdkv_ref[...])
    ckv = _rms_norm(kv[:, :KV_LORA], kvg_ref[...])
    qd = _rms_norm(_dot(h, wdq_ref[...]), qg_ref[...]).astype(BF16)
    return kv, ckv, qd, _dot(qd, wuq_ref[...])


def _mla_out(x, attn, mod_ref, g_ref, b_ref, wo_ref):
    y = _dot(attn, wo_ref[...])
    return _layer_norm(ALPHA * x + mod_ref[5:6, :] * y, g_ref[...], b_ref[...])


def _mla_prompt_body(x_ref, mod_ref, g_ref, b_ref, wdkv_ref, kvg_ref, wdq_ref, qg_ref, wuq_ref,
                     wuk_ref, wuv_ref, wo_ref, o_ref, ckv_ref, kr_ref):
    for s in range(x_ref.shape[0] // SEQ):
        rows = slice(s * SEQ, (s + 1) * SEQ)
        x = x_ref[rows, :]
        kv, ckv, _, q = _mla_project(x, mod_ref, wdkv_ref, kvg_ref, wdq_ref, qg_ref, wuq_ref)
        ckv_ref[rows, :] = ckv
        kr_ref[rows, :] = kv[:, KV_LORA:KV_LORA + QK_ROPE]
        q16 = q.astype(BF16)
        ckv16 = ckv.astype(BF16)
        kn16 = _dot(ckv16, wuk_ref[...]).astype(BF16)
        v16 = _dot(ckv16, wuv_ref[...]).astype(BF16)
        kr16 = kv[:, KV_LORA:KV_LORA + 128].astype(BF16)
        heads = []
        for hd in range(N_HEADS):
            vl = slice(hd * V_DIM, (hd + 1) * V_DIM)
            k_h = jnp.concatenate([kn16[:, hd * QK_NOPE:(hd + 1) * QK_NOPE], kr16], axis=-1)
            heads.append(_attend(q16[:, hd * HEAD_W:(hd + 1) * HEAD_W], [k_h], [v16[:, vl]]).astype(BF16))
        attn = jnp.concatenate(heads, axis=-1)
        o_ref[rows, :] = _mla_out(x, attn, mod_ref, g_ref, b_ref, wo_ref)


def _store_keys(k_ref, rows, kn16, kr16):
    for hd in range(N_HEADS):
        k_ref[rows, hd * HEAD_W:hd * HEAD_W + QK_NOPE] = kn16[:, hd * QK_NOPE:(hd + 1) * QK_NOPE]
        k_ref[rows, hd * HEAD_W + QK_NOPE:(hd + 1) * HEAD_W] = kr16


def _mla_latent_body(x_ref, mod_ref, g_ref, b_ref, wdkv_ref, kvg_ref, wdq_ref, qg_ref, wuq_ref,
                     wuk_ref, wuv_ref, wo_ref, wuqs_ref, cos_ref, sin_ref, cckv_ref, ckr_ref,
                     o_ref, q_s, k_s, v_s, kc_s, vc_s):
    c16 = cckv_ref[...].astype(BF16)
    vc_s[...] = _dot(c16, wuv_ref[...]).astype(BF16)
    _store_keys(kc_s, slice(None), _dot(c16, wuk_ref[...]).astype(BF16), ckr_ref[...].astype(BF16))
    for s in range(DEC_SEQ // ATTN_Q_TILE):
        rows = slice(s * ATTN_Q_TILE, (s + 1) * ATTN_Q_TILE)
        cos = cos_ref[rows, :]
        sin = sin_ref[rows, :]
        kv, ckv, qd, q = _mla_project(x_ref[rows, :], mod_ref, wdkv_ref, kvg_ref, wdq_ref, qg_ref, wuq_ref)
        q_swapped = _dot(qd, wuqs_ref[...])
        kr = kv[:, KV_LORA:KV_LORA + 128] * cos + kv[:, KV_LORA + 128:KV_LORA + 256] * sin
        ckv16 = ckv.astype(BF16)
        v_s[rows, :] = _dot(ckv16, wuv_ref[...]).astype(BF16)
        _store_keys(k_s, rows, _dot(ckv16, wuk_ref[...]).astype(BF16), kr.astype(BF16))
        for hd in range(N_HEADS):
            lo = hd * HEAD_W
            q_s[rows, lo:lo + QK_NOPE] = q[:, lo:lo + QK_NOPE].astype(BF16)
            q_rope = q[:, lo + QK_NOPE:lo + HEAD_W] * cos + q_swapped[:, hd * 128:(hd + 1) * 128] * sin
            q_s[rows, lo + QK_NOPE:lo + HEAD_W] = q_rope.astype(BF16)

    def q_block(i, carry):
        rows = pl.ds(pl.multiple_of(i * ATTN_Q_TILE, ATTN_Q_TILE), ATTN_Q_TILE)
        heads = []
        for hd in range(N_HEADS):
            ql = slice(hd * HEAD_W, (hd + 1) * HEAD_W)
            vl = slice(hd * V_DIM, (hd + 1) * V_DIM)
            heads.append(_attend(q_s[rows, ql], [k_s[:, ql], kc_s[:, ql]],
                                 [v_s[:, vl], vc_s[:, vl]]).astype(BF16))
        attn = jnp.concatenate(heads, axis=-1)
        o_ref[rows, :] = _mla_out(x_ref[rows, :], attn, mod_ref, g_ref, b_ref, wo_ref)
        return carry

    lax.fori_loop(0, DEC_SEQ // ATTN_Q_TILE, q_block, 0)


def _mla_mixer(x, mod, ln_g, ln_b, w, cache_ckv, cache_kr, *, latent):
    tile = MLA_TILE
    first = N_PROMPT // tile if latent else 0
    n_rows = DEC_BATCH * DEC_SEQ if latent else N_PROMPT
    dkv = w["dkv"] if latent else w["dkv"][:, :KV_LORA + 128]
    in_specs = [
        pl.BlockSpec((tile, D_MODEL), lambda t: (first + t, 0)),
        pl.BlockSpec((None, N_MOD, D_MODEL), lambda t: (_group_of_tile(first + t, tile), 0, 0)),
        _const_spec((1, D_MODEL)),
        _const_spec((1, D_MODEL)),
        _const_spec(dkv.shape),
        _const_spec((1, KV_LORA)),
        _const_spec((D_MODEL, Q_LORA)),
        _const_spec((1, Q_LORA)),
        _const_spec((Q_LORA, N_HEADS * HEAD_W)),
        _const_spec((KV_LORA, N_HEADS * QK_NOPE)),
        _const_spec((KV_LORA, N_HEADS * V_DIM)),
        _const_spec((N_HEADS * V_DIM, D_MODEL)),
    ]
    args = [x, mod, ln_g.reshape(1, D_MODEL), ln_b.reshape(1, D_MODEL), dkv, w["kv_g"], w["dq"],
            w["q_g"], w["uq"], w["uk"], w["uv"], w["o"]]
    out_shape = [jax.ShapeDtypeStruct((n_rows, D_MODEL), F32)]
    out_specs = [pl.BlockSpec((tile, D_MODEL), lambda t: (t, 0))]
    scratch = []
    if latent:
        assert tile == DEC_SEQ
        in_specs += [
            _const_spec((Q_LORA, N_HEADS * 128)),
            _const_spec((DEC_SEQ, 128)),
            _const_spec((DEC_SEQ, 128)),
            pl.BlockSpec((None, PAST_LEN, KV_LORA), lambda t: (t, 0, 0)),
            pl.BlockSpec((None, PAST_LEN, 128), lambda t: (t, 0, 0)),
        ]
        args += [w["uq_swapped"], w["cos"], w["sin"], cache_ckv, cache_kr]
        scratch = [
            pltpu.VMEM((DEC_SEQ, N_HEADS * HEAD_W), BF16),
            pltpu.VMEM((DEC_SEQ, N_HEADS * HEAD_W), BF16),
            pltpu.VMEM((DEC_SEQ, N_HEADS * V_DIM), BF16),
            pltpu.VMEM((PAST_LEN, N_HEADS * HEAD_W), BF16),
            pltpu.VMEM((PAST_LEN, N_HEADS * V_DIM), BF16),
        ]
    else:
        out_shape += [jax.ShapeDtypeStruct((n_rows, KV_LORA), F32),
                      jax.ShapeDtypeStruct((n_rows, QK_ROPE), F32)]
        out_specs += [pl.BlockSpec((tile, KV_LORA), lambda t: (t, 0)),
                      pl.BlockSpec((tile, QK_ROPE), lambda t: (t, 0))]
    outs = pl.pallas_call(
        _mla_latent_body if latent else _mla_prompt_body,
        out_shape=out_shape,
        grid=(n_rows // tile,),
        in_specs=in_specs,
        out_specs=out_specs,
        scratch_shapes=scratch,
        compiler_params=_params(),
        name="mla_latent" if latent else "mla_prompt",
    )(*args)
    return outs[0] if latent else outs


def _rope_swap_perm():
    idx = np.arange(QK_ROPE)
    return np.where((idx % 32) < ROPE_AXIS_PAIRS, idx + ROPE_AXIS_PAIRS, idx - ROPE_AXIS_PAIRS)


def _rope_tables():
    n = DEC_SEQ
    row = jnp.repeat(jnp.arange(n // GRID_W), GRID_W)
    col = jnp.tile(jnp.arange(GRID_W), n // GRID_W)
    inv = ROPE_BASE ** (-jnp.arange(ROPE_AXIS_PAIRS, dtype=F32) / ROPE_AXIS_PAIRS)
    ar = row[:, None] * inv
    ac = col[:, None] * inv
    zeros = jnp.zeros((n, 128 - QK_ROPE), F32)
    cos = jnp.concatenate([jnp.cos(ar), jnp.cos(ar), jnp.cos(ac), jnp.cos(ac), zeros], axis=-1)
    sin = jnp.concatenate([-jnp.sin(ar), jnp.sin(ar), -jnp.sin(ac), jnp.sin(ac), zeros], axis=-1)
    return cos, sin


def _mla_weights(w_dq, q_norm_g, w_uq, w_dkv, kv_norm_g, w_ukv, w_o):
    swap = _rope_swap_perm()
    z64 = jnp.zeros((D_MODEL, 128 - QK_ROPE), F32)
    kr_cols = w_dkv[:, KV_LORA:]
    dkv = jnp.concatenate([w_dkv[:, :KV_LORA], kr_cols, z64, kr_cols[:, swap], z64], axis=-1)
    uq = w_uq.reshape(Q_LORA, N_HEADS, QK_NOPE + QK_ROPE)
    zq = jnp.zeros((Q_LORA, N_HEADS, 128 - QK_ROPE), F32)
    uq_main = jnp.concatenate([uq, zq], axis=-1).reshape(Q_LORA, N_HEADS * HEAD_W)
    uq_swapped = jnp.concatenate([uq[:, :, QK_NOPE:][:, :, swap], zq], axis=-1).reshape(Q_LORA, N_HEADS * 128)
    ukv = w_ukv.reshape(KV_LORA, N_HEADS, QK_NOPE + V_DIM)
    cos, sin = _rope_tables()
    return {
        "dkv": dkv.astype(BF16),
        "kv_g": kv_norm_g.reshape(1, KV_LORA),
        "dq": w_dq.astype(BF16),
        "q_g": q_norm_g.reshape(1, Q_LORA),
        "uq": uq_main.astype(BF16),
        "uq_swapped": uq_swapped.astype(BF16),
        "uk": ukv[:, :, :QK_NOPE].reshape(KV_LORA, N_HEADS * QK_NOPE).astype(BF16),
        "uv": ukv[:, :, QK_NOPE:].reshape(KV_LORA, N_HEADS * V_DIM).astype(BF16),
        "o": w_o.astype(BF16),
        "cos": cos,
        "sin": sin,
    }


def kernel(x_prompt, x_sample, cache_mla_ckv, cache_mla_krope, c, c_ctx, w_ada, b_ada, ln_g, ln_b, ffn_w1, ffn_w3, ffn_w2, cp_w_in, conv_w, conv_b, conv_norm_g, conv_norm_b, pool_w, pool_scale, cp_w_out, mla_w_dq, mla_q_norm_g, mla_w_uq, mla_w_dkv, mla_kv_norm_g, mla_w_ukv, mla_w_o):
    c_all = jnp.concatenate(
        [c_ctx[None, :], c, jnp.zeros((N_GROUPS - 1 - DEC_BATCH, D_MODEL), F32)], axis=0)
    ada = _ada(c_all, w_ada, b_ada).reshape(DEPTH, N_GROUPS, N_MOD, D_MODEL)

    x = (x_prompt.reshape(N_PROMPT, D_MODEL), x_sample.reshape(DEC_BATCH * DEC_SEQ, D_MODEL))
    ffn_f32 = (ffn_w1, ffn_w3, ffn_w2)
    w_bf16 = tuple(wf[0, 0].astype(BF16) for wf in ffn_f32)
    new_ckv, new_krope = [], []
    for i in range(DEPTH):
        mod = ada[i]
        j = i // 2
        x, w_bf16 = _ffn(x, mod, ln_g[i, 0], ln_b[i, 0], *w_bf16, ffn_f32, (i, 1), k0=0)
        if i % 2 == 0:
            x = _conv_pool_mixer(x, mod, ln_g[i, 1], ln_b[i, 1], cp_w_in[j].astype(BF16), conv_w[j],
                                 conv_b[j], conv_norm_g[j], conv_norm_b[j], pool_w[j].astype(BF16),
                                 pool_scale[j], cp_w_out[j].astype(BF16))
        else:
            w = _mla_weights(mla_w_dq[j], mla_q_norm_g[j], mla_w_uq[j], mla_w_dkv[j],
                             mla_kv_norm_g[j], mla_w_ukv[j], mla_w_o[j])
            attn_args = (x, mod, ln_g[i, 1], ln_b[i, 1], w)
            y_p, ckv_p, kr_p = _mla_mixer(*attn_args, None, None, latent=False)
            new_ckv.append(ckv_p.reshape(BATCH, SEQ, KV_LORA))
            new_krope.append(kr_p.reshape(BATCH, SEQ, QK_ROPE))
            kr_cache = jnp.pad(cache_mla_krope[:, j], ((0, 0), (0, 0), (0, 128 - QK_ROPE)))
            x = (y_p, _mla_mixer(*attn_args, cache_mla_ckv[:, j], kr_cache, latent=True))
        last = i == DEPTH - 1
        x, w_bf16 = _ffn(x, mod, ln_g[i, 2], ln_b[i, 2], *w_bf16,
                         None if last else ffn_f32, None if last else (i + 1, 0),
                         k0=6, split_out=last)

    y_prompt = x[0].reshape(BATCH, SEQ, D_MODEL)
    y_sample = x[1].reshape(DEC_BATCH, DEC_SEQ, D_MODEL)
    return (y_prompt, y_sample, jnp.stack(new_ckv, axis=1), jnp.stack(new_krope, axis=1))
```

```python
import functools
import math

import jax
import jax.numpy as jnp
import numpy as np
from jax import lax
from jax.experimental import pallas as pl
from jax.experimental.pallas import tpu as pltpu

F32 = jnp.float32
BF16 = jnp.bfloat16

D_MODEL = 1024
BATCH = 32
SEQ = 256
DEPTH = 2
DEC_BATCH = 4
DEC_SEQ = 1024
PAST_LEN = 512
GRID_W = 64
N_MOD = 9
D_FF = 2816
D_CONV = 512
CONV_WIDTH = 31
D_POOL = 512
POOL_WINDOWS = (2, 4, 8, 16)
POOL_GROUP = 128
N_HEADS = 8
QK_NOPE = 128
QK_ROPE = 64
V_DIM = 128
KV_LORA = 256
Q_LORA = 384
ROPE_AXIS_PAIRS = 16
ROPE_BASE = 10000.0
ALPHA = (2 * DEPTH) ** 0.25
LN_EPS = 1e-5
RMS_EPS = 1e-6

N_PROMPT = BATCH * SEQ
N_TOKENS = N_PROMPT + DEC_BATCH * DEC_SEQ
N_GROUPS = 8
HEAD_W = 256
CONV_PAD = 16
CONV_STRIDE = 4
CONV_ROWS = 8 * CONV_STRIDE
CONV_UNROLL = 4

VMEM_LIMIT_BYTES = 60 * 1024 * 1024

FFN_TILES = (1024, 512)
FFN_WORK_BYTES = 4 * 1024 * 1024
FFN_SUB = 256
FFN_UNROLL = 4
MIX_TILE = 1024
MIX_SEG = 256
SEG_ROWS = MIX_SEG + 2 * CONV_PAD
MLA_TILE = 1024
ATTN_Q_TILE = 256


def _layer_norm(z, g, b):
    mu = jnp.mean(z, axis=-1, keepdims=True)
    zc = z - mu
    var = jnp.mean(zc * zc, axis=-1, keepdims=True)
    return zc * lax.rsqrt(var + LN_EPS) * g + b


def _rms_norm(z, g):
    return z * lax.rsqrt(jnp.mean(z * z, axis=-1, keepdims=True) + RMS_EPS) * g


def _silu(z):
    return z * jax.nn.sigmoid(z)


def _dot(a, b):
    return jnp.dot(a, b, preferred_element_type=F32)


def _dot_nt(a, b):
    return lax.dot_general(a, b, (((1,), (1,)), ((), ())), preferred_element_type=F32)


def _const_spec(shape):
    nd = len(shape)
    return pl.BlockSpec(shape, lambda *_: (0,) * nd, pipeline_mode=pl.Buffered(1))


def _group_of_tile(t, tile):
    n_p = N_PROMPT // tile
    per_seq = DEC_SEQ // tile
    return jnp.where(t < n_p, 0, 1 + (t - n_p) // per_seq)


def _params(n_axes=1):
    return pltpu.CompilerParams(
        dimension_semantics=("arbitrary",) * n_axes,
        vmem_limit_bytes=VMEM_LIMIT_BYTES,
    )


def _ada_body(c_ref, w_ref, b_ref, o_ref):
    s = _silu(c_ref[...]).astype(BF16)
    o_ref[...] = _dot(s, w_ref[...].astype(BF16)) + b_ref[...]


def _ada(c_all, w_ada, b_ada, layer):
    tn = D_MODEL
    return pl.pallas_call(
        _ada_body,
        out_shape=jax.ShapeDtypeStruct((N_GROUPS, N_MOD * D_MODEL), F32),
        grid=(N_MOD,),
        in_specs=[
            pl.BlockSpec((N_GROUPS, D_MODEL), lambda k: (0, 0)),
            pl.BlockSpec((None, D_MODEL, tn), lambda k: (layer, 0, k)),
            pl.BlockSpec((None, 1, tn), lambda k: (layer, 0, k)),
        ],
        out_specs=pl.BlockSpec((N_GROUPS, tn), lambda k: (0, k)),
        compiler_params=_params(),
        name="ada",
    )(c_all, w_ada, b_ada.reshape(DEPTH, 1, N_MOD * D_MODEL))


def _ffn_body(*refs, k0, split_in, split_out, cast_next):
    refs = list(refs)
    n_x = 2 if split_in else 1
    x_refs, refs = refs[:n_x], refs[n_x:]
    (mod_ref, g_ref, b_ref, w1_ref, w3_ref, w2_ref), refs = refs[:6], refs[6:]
    if cast_next:
        next_f32, refs = refs[:3], refs[3:]
    n_o = 2 if split_out else 1
    o_refs, refs = refs[:n_o], refs[n_o:]
    tile = x_refs[0].shape[0]
    is_prompt = pl.program_id(0) < N_PROMPT // tile
    if cast_next:
        for src, dst in zip(next_f32, refs):
            dst[...] = src[...].astype(BF16)
    shift = mod_ref[k0:k0 + 1, :]
    scale = mod_ref[k0 + 1:k0 + 2, :]
    gate = mod_ref[k0 + 2:k0 + 3, :]
    n_sub = tile // FFN_SUB
    unroll = min(FFN_UNROLL, n_sub)

    def sub_tiles(trip, carry):
        outs = []
        for u in range(unroll):
            start = (trip * unroll + u) * FFN_SUB
            rows = pl.ds(start if isinstance(start, int) else pl.multiple_of(start, FFN_SUB), FFN_SUB)
            if split_in:
                x = jnp.where(is_prompt, x_refs[0][rows, :], x_refs[1][rows, :])
            else:
                x = x_refs[0][rows, :]
            h = (x * (1.0 + scale) + shift).astype(BF16)
            a = (_silu(_dot(h, w1_ref[...])) * _dot(h, w3_ref[...])).astype(BF16)
            y = _dot(a, w2_ref[...])
            out = _layer_norm(ALPHA * x + (0.5 * gate) * y, g_ref[...], b_ref[...])
            if split_out:
                outs.append((rows, out))
            else:
                o_refs[0][rows, :] = out
        if split_out:
            @pl.when(is_prompt)
            def _():
                for rows, out in outs:
                    o_refs[0][rows, :] = out

            @pl.when(jnp.logical_not(is_prompt))
            def _():
                for rows, out in outs:
                    o_refs[1][rows, :] = out
        return carry

    if n_sub == unroll:
        sub_tiles(0, 0)
    else:
        lax.fori_loop(0, n_sub // unroll, sub_tiles, 0)


def _slab_rows(n_rows, n_steps):
    n_slabs = max(d for d in range(1, n_steps + 1) if n_rows % d == 0 and (n_rows // d) % 16 == 0)
    return n_rows // n_slabs, n_slabs


def _ffn_tile(n_in, split_out, next_w):
    for tm in FFN_TILES:
        n_steps = N_TOKENS // tm
        tile_bytes = tm * D_MODEL * 4
        est = 3 * D_MODEL * D_FF * 2
        est += (n_in + (2 if split_out else 1)) * 2 * tile_bytes
        est += FFN_WORK_BYTES + (2 * tile_bytes if split_out else 0)
        if next_w is not None:
            for wn in next_w:
                rows, _ = _slab_rows(wn.shape[2], n_steps)
                est += 2 * rows * wn.shape[3] * (4 + 2)
        if est <= VMEM_LIMIT_BYTES:
            return tm
    raise ValueError("no FFN tile fits in VMEM")


def _ffn(xs, mod, ln_g, ln_b, w1, w3, w2, next_w=None, next_idx=None, *, k0, split_out=False):
    split_in = isinstance(xs, (tuple, list))
    xs = list(xs) if split_in else [xs]
    tm = _ffn_tile(len(xs), split_out, next_w)
    n_steps = N_TOKENS // tm
    n_p = N_PROMPT // tm
    prompt_spec = pl.BlockSpec((tm, D_MODEL), lambda t: (jnp.minimum(t, n_p - 1), 0))
    latent_spec = pl.BlockSpec((tm, D_MODEL), lambda t: (jnp.maximum(t - n_p, 0), 0))
    merged_spec = pl.BlockSpec((tm, D_MODEL), lambda t: (t, 0))
    in_specs = ([prompt_spec, latent_spec] if split_in else [merged_spec]) + [
        pl.BlockSpec((None, N_MOD, D_MODEL), lambda t: (_group_of_tile(t, tm), 0, 0)),
        _const_spec((1, D_MODEL)),
        _const_spec((1, D_MODEL)),
        _const_spec((D_MODEL, D_FF)),
        _const_spec((D_MODEL, D_FF)),
        _const_spec((D_FF, D_MODEL)),
    ]
    args = xs + [mod, ln_g.reshape(1, D_MODEL), ln_b.reshape(1, D_MODEL), w1, w3, w2]
    if split_out:
        out_shape = [jax.ShapeDtypeStruct((N_PROMPT, D_MODEL), F32),
                     jax.ShapeDtypeStruct((N_TOKENS - N_PROMPT, D_MODEL), F32)]
        out_specs = [prompt_spec, latent_spec]
    else:
        out_shape = [jax.ShapeDtypeStruct((N_TOKENS, D_MODEL), F32)]
        out_specs = [merged_spec]
    if next_w is not None:
        li, lj = next_idx
        for wn in next_w:
            n_rows, n_cols = wn.shape[2:]
            rows, n_slabs = _slab_rows(n_rows, n_steps)
            in_specs.append(pl.BlockSpec(
                (None, None, rows, n_cols),
                lambda t, n_slabs=n_slabs: (li, lj, jnp.minimum(t, n_slabs - 1), 0)))
            args.append(wn)
            out_shape.append(jax.ShapeDtypeStruct((n_rows, n_cols), BF16))
            out_specs.append(pl.BlockSpec(
                (rows, n_cols), lambda t, n_slabs=n_slabs: (jnp.minimum(t, n_slabs - 1), 0)))
    outs = pl.pallas_call(
        functools.partial(_ffn_body, k0=k0, split_in=split_in, split_out=split_out,
                          cast_next=next_w is not None),
        out_shape=out_shape,
        grid=(n_steps,),
        in_specs=in_specs,
        out_specs=out_specs,
        compiler_params=_params(),
        name="ffn",
    )(*args)
    n_tok = 2 if split_out else 1
    tok = tuple(outs[:n_tok]) if split_out else outs[0]
    return tok, tuple(outs[n_tok:])


def _strided_rows(start):
    return pl.ds(start, 8, stride=CONV_STRIDE)


def _chunk_rows(j):
    chunks_per_seg = MIX_SEG // CONV_ROWS
    r0 = (j % chunks_per_seg) * CONV_ROWS
    return (j // chunks_per_seg) * SEG_ROWS + r0 + CONV_PAD, j * CONV_ROWS, r0


def _conv_chunk(j, *, slab, cw_ref, cb_ref, pad_ref, conv_ref):
    p0, o0, _ = _chunk_rows(j)
    half = CONV_WIDTH // 2
    n_part = 2
    acc = [[None] * n_part for _ in range(CONV_STRIDE)]
    weights = {}
    for off in range(-half, half + CONV_STRIDE):
        tap = pad_ref[slab, _strided_rows(p0 + off), :]
        for rho in range(CONV_STRIDE):
            k = off - rho + half
            if 0 <= k < CONV_WIDTH:
                if k not in weights:
                    weights[k] = cw_ref[slab, pl.ds(k, 8, stride=0), :]
                term = weights[k] * tap
                part = acc[rho][k % n_part]
                acc[rho][k % n_part] = term if part is None else part + term
    bias = cb_ref[slab, pl.ds(0, 8, stride=0), :]
    for rho in range(CONV_STRIDE):
        conv_ref[slab, _strided_rows(o0 + rho), :] = (acc[rho][0] + acc[rho][1]) + bias


def _pool_chunk(j, *, pad_ref, pool_ref, is_prompt):
    p0, o0, r0 = _chunk_rows(j)
    at_edge = r0 == 0 or r0 == MIX_SEG - CONV_ROWS
    if at_edge:
        seq_len = jnp.where(is_prompt, SEQ, DEC_SEQ)
        t0 = jnp.where(is_prompt, r0, o0) + CONV_STRIDE * lax.broadcasted_iota(jnp.int32, (8, 128), 0)
    for gi, w in enumerate(POOL_WINDOWS):
        left = w // 2
        right = w - 1 - left
        slab = D_CONV // 128 + gi
        total = [None] * CONV_STRIDE
        for off in range(-left, right + CONV_STRIDE):
            tap = pad_ref[slab, _strided_rows(p0 + off), :]
            for rho in range(CONV_STRIDE):
                if -left <= off - rho <= right:
                    total[rho] = tap if total[rho] is None else total[rho] + tap
        for rho in range(CONV_STRIDE):
            if at_edge:
                t = t0 + rho
                cnt = (jnp.minimum(t + right, seq_len - 1) - jnp.maximum(t - left, 0) + 1).astype(F32)
                mean = total[rho] / cnt
            else:
                mean = total[rho] / float(w)
            tok = pad_ref[slab, _strided_rows(p0 + rho), :]
            pool_ref[gi, _strided_rows(o0 + rho), :] = mean - tok


def _cp_body(x_ref, mod_ref, g_ref, b_ref, win_ref, cw_ref, cb_ref, cng_ref, cnb_ref,
             pw_ref, ps_ref, wout_ref, c_ref, wada_ref, bada_ref, o_ref, ada_ref,
             pad_ref, conv_ref, pool_ref):
    is_prompt = pl.program_id(0) < N_PROMPT // MIX_TILE
    _ada_body(c_ref, wada_ref, bada_ref, ada_ref)
    shift = mod_ref[3:4, :]
    scale = mod_ref[4:5, :]
    gate = mod_ref[5:6, :]
    n_seg = MIX_TILE // MIX_SEG
    n_slab = (D_CONV + D_POOL) // 128
    feats = []
    for s in range(n_seg):
        x = x_ref[s * MIX_SEG:(s + 1) * MIX_SEG, :]
        h = (x * (1.0 + scale) + shift).astype(BF16)
        proj = _dot(h, win_ref[...])
        glu = proj[:, :D_CONV] * jax.nn.sigmoid(proj[:, D_CONV:2 * D_CONV])
        feats.append(jnp.concatenate([glu, proj[:, 2 * D_CONV:]], axis=-1))
    zeros = jnp.zeros((CONV_PAD, 128), F32)
    for s in range(n_seg):
        base = s * SEG_ROWS
        for slab in range(n_slab):
            lanes = slice(slab * 128, (slab + 1) * 128)
            pad_ref[slab, base:base + CONV_PAD, :] = (
                zeros if s == 0 else jnp.where(is_prompt, zeros, feats[s - 1][MIX_SEG - CONV_PAD:, lanes]))
            pad_ref[slab, base + CONV_PAD:base + CONV_PAD + MIX_SEG, :] = feats[s][:, lanes]
            pad_ref[slab, base + CONV_PAD + MIX_SEG:base + SEG_ROWS, :] = (
                zeros if s == n_seg - 1 else jnp.where(is_prompt, zeros, feats[s + 1][:CONV_PAD, lanes]))
    chunks_per_seg = MIX_SEG // CONV_ROWS
    for s in range(n_seg):
        rows = slice(s * MIX_SEG, (s + 1) * MIX_SEG)
        seg_chunks = range(s * chunks_per_seg, (s + 1) * chunks_per_seg)
        for slab in range(D_CONV // 128):
            for j in seg_chunks:
                _conv_chunk(j, slab=slab, cw_ref=cw_ref, cb_ref=cb_ref, pad_ref=pad_ref, conv_ref=conv_ref)
        for j in seg_chunks:
            _pool_chunk(j, pad_ref=pad_ref, pool_ref=pool_ref, is_prompt=is_prompt)
        conv = jnp.concatenate([conv_ref[c, rows, :] for c in range(D_CONV // 128)], axis=-1)
        a = _silu(_layer_norm(conv, cng_ref[...], cnb_ref[...]))
        pooled = [_dot(pool_ref[gi, rows, :].astype(BF16), pw_ref[gi]) for gi in range(len(POOL_WINDOWS))]
        bmix = jnp.concatenate(pooled, axis=-1) * ps_ref[...]
        cat = jnp.concatenate([a, bmix], axis=-1).astype(BF16)
        y = _dot(cat, wout_ref[...])
        o_ref[rows, :] = _layer_norm(ALPHA * x_ref[rows, :] + gate * y, g_ref[...], b_ref[...])


def _conv_pool_mixer(x, mod, ln_g, ln_b, w_in, conv_w, conv_b, cn_g, cn_b, pool_w, pool_scale, w_out,
                     c_all, w_ada, b_ada, next_layer):
    tm = MIX_TILE
    n_steps = N_TOKENS // tm
    ada_cols = N_MOD * D_MODEL // n_steps
    assert tm == DEC_SEQ and MIX_SEG == SEQ and N_PROMPT % tm == 0 and ada_cols % 128 == 0
    n_cslab = D_CONV // 128
    row_spec = pl.BlockSpec((tm, D_MODEL), lambda t: (t, 0))
    in_specs = [
        row_spec,
        pl.BlockSpec((None, N_MOD, D_MODEL), lambda t: (_group_of_tile(t, tm), 0, 0)),
        _const_spec((1, D_MODEL)),
        _const_spec((1, D_MODEL)),
        _const_spec((D_MODEL, 2 * D_CONV + D_POOL)),
        _const_spec((n_cslab, CONV_WIDTH, 128)),
        _const_spec((n_cslab, 1, 128)),
        _const_spec((1, D_CONV)),
        _const_spec((1, D_CONV)),
        _const_spec((len(POOL_WINDOWS), POOL_GROUP, POOL_GROUP)),
        _const_spec((1, D_POOL)),
        _const_spec((D_CONV + D_POOL, D_MODEL)),
        _const_spec((N_GROUPS, D_MODEL)),
        pl.BlockSpec((None, D_MODEL, ada_cols), lambda t: (next_layer, 0, t)),
        pl.BlockSpec((None, 1, ada_cols), lambda t: (next_layer, 0, t)),
    ]
    conv_w_slabs = conv_w.reshape(CONV_WIDTH, n_cslab, 128).transpose(1, 0, 2)
    args = [x, mod, ln_g.reshape(1, D_MODEL), ln_b.reshape(1, D_MODEL), w_in, conv_w_slabs,
            conv_b.reshape(n_cslab, 1, 128), cn_g.reshape(1, D_CONV), cn_b.reshape(1, D_CONV),
            pool_w, pool_scale.reshape(1, D_POOL), w_out,
            c_all, w_ada, b_ada.reshape(DEPTH, 1, N_MOD * D_MODEL)]
    return pl.pallas_call(
        _cp_body,
        out_shape=[jax.ShapeDtypeStruct((N_TOKENS, D_MODEL), F32),
                   jax.ShapeDtypeStruct((N_GROUPS, N_MOD * D_MODEL), F32)],
        grid=(n_steps,),
        in_specs=in_specs,
        out_specs=[row_spec, pl.BlockSpec((N_GROUPS, ada_cols), lambda t: (0, t))],
        scratch_shapes=[
            pltpu.VMEM(((D_CONV + D_POOL) // 128, tm // MIX_SEG * SEG_ROWS, 128), F32),
            pltpu.VMEM((n_cslab, tm, 128), F32),
            pltpu.VMEM((D_POOL // 128, tm, 128), F32),
        ],
        compiler_params=_params(),
        name="conv_pool",
    )(*args)


def _attend(q, keys, values):
    scores = [_dot_nt(q, k) for k in keys]
    m = functools.reduce(jnp.maximum, [jnp.max(s, axis=-1, keepdims=True) for s in scores])
    c = (QK_NOPE + QK_ROPE) ** -0.5 * math.log2(math.e)
    e = [jnp.exp2((s - m) * c) for s in scores]
    denom = functools.reduce(jnp.add, [jnp.sum(ei, axis=-1, keepdims=True) for ei in e])
    out = functools.reduce(jnp.add, [_dot(ei.astype(BF16), v) for ei, v in zip(e, values)])
    return out * (1.0 / denom)


def _mla_project(x, mod_ref, wdkv_ref, kvg_ref, wdq_ref, qg_ref, wuq_ref):
    h = (x * (1.0 + mod_ref[4:5, :]) + mod_ref[3:4, :]).astype(BF16)
    kv = _dot(h, wdkv_ref[...])
    ckv = _rms_norm(kv[:, :KV_LORA], kvg_ref[...])
    qd = _rms_norm(_dot(h, wdq_ref[...]), qg_ref[...]).astype(BF16)
    return kv, ckv, qd, _dot(qd, wuq_ref[...])


def _mla_out(x, attn, mod_ref, g_ref, b_ref, wo_ref):
    y = _dot(attn, wo_ref[...])
    return _layer_norm(ALPHA * x + mod_ref[5:6, :] * y, g_ref[...], b_ref[...])


def _mla_prompt_body(x_ref, mod_ref, g_ref, b_ref, wdkv_ref, kvg_ref, wdq_ref, qg_ref, wuq_ref,
                     wuk_ref, wuv_ref, wo_ref, o_ref, ckv_ref, kr_ref):
    for s in range(x_ref.shape[0] // SEQ):
        rows = slice(s * SEQ, (s + 1) * SEQ)
        x = x_ref[rows, :]
        kv, ckv, _, q = _mla_project(x, mod_ref, wdkv_ref, kvg_ref, wdq_ref, qg_ref, wuq_ref)
        ckv_ref[rows, :] = ckv
        kr_ref[rows, :] = kv[:, KV_LORA:KV_LORA + QK_ROPE]
        q16 = q.astype(BF16)
        ckv16 = ckv.astype(BF16)
        kn16 = _dot(ckv16, wuk_ref[...]).astype(BF16)
        v16 = _dot(ckv16, wuv_ref[...]).astype(BF16)
        kr16 = kv[:, KV_LORA:KV_LORA + 128].astype(BF16)
        heads = []
        for hd in range(N_HEADS):
            vl = slice(hd * V_DIM, (hd + 1) * V_DIM)
            k_h = jnp.concatenate([kn16[:, hd * QK_NOPE:(hd + 1) * QK_NOPE], kr16], axis=-1)
            heads.append(_attend(q16[:, hd * HEAD_W:(hd + 1) * HEAD_W], [k_h], [v16[:, vl]]).astype(BF16))
        attn = jnp.concatenate(heads, axis=-1)
        o_ref[rows, :] = _mla_out(x, attn, mod_ref, g_ref, b_ref, wo_ref)


def _store_keys(k_ref, rows, kn16, kr16):
    for hd in range(N_HEADS):
        k_ref[rows, hd * HEAD_W:hd * HEAD_W + QK_NOPE] = kn16[:, hd * QK_NOPE:(hd + 1) * QK_NOPE]
        k_ref[rows, hd * HEAD_W + QK_NOPE:(hd + 1) * HEAD_W] = kr16


def _mla_latent_body(x_ref, mod_ref, g_ref, b_ref, wdkv_ref, kvg_ref, wdq_ref, qg_ref, wuq_ref,
                     wuk_ref, wuv_ref, wo_ref, wuqs_ref, cos_ref, sin_ref, cckv_ref, ckr_ref,
                     o_ref, q_s, k_s, v_s, kc_s, vc_s):
    c16 = cckv_ref[...].astype(BF16)
    vc_s[...] = _dot(c16, wuv_ref[...]).astype(BF16)
    _store_keys(kc_s, slice(None), _dot(c16, wuk_ref[...]).astype(BF16), ckr_ref[...].astype(BF16))
    for s in range(DEC_SEQ // ATTN_Q_TILE):
        rows = slice(s * ATTN_Q_TILE, (s + 1) * ATTN_Q_TILE)
        cos = cos_ref[rows, :]
        sin = sin_ref[rows, :]
        kv, ckv, qd, q = _mla_project(x_ref[rows, :], mod_ref, wdkv_ref, kvg_ref, wdq_ref, qg_ref, wuq_ref)
        q_swapped = _dot(qd, wuqs_ref[...])
        kr = kv[:, KV_LORA:KV_LORA + 128] * cos + kv[:, KV_LORA + 128:KV_LORA + 256] * sin
        ckv16 = ckv.astype(BF16)
        v_s[rows, :] = _dot(ckv16, wuv_ref[...]).astype(BF16)
        _store_keys(k_s, rows, _dot(ckv16, wuk_ref[...]).astype(BF16), kr.astype(BF16))
        for hd in range(N_HEADS):
            lo = hd * HEAD_W
            q_s[rows, lo:lo + QK_NOPE] = q[:, lo:lo + QK_NOPE].astype(BF16)
            q_rope = q[:, lo + QK_NOPE:lo + HEAD_W] * cos + q_swapped[:, hd * 128:(hd + 1) * 128] * sin
            q_s[rows, lo + QK_NOPE:lo + HEAD_W] = q_rope.astype(BF16)

    def q_block(i, carry):
        rows = pl.ds(pl.multiple_of(i * ATTN_Q_TILE, ATTN_Q_TILE), ATTN_Q_TILE)
        heads = []
        for hd in range(N_HEADS):
            ql = slice(hd * HEAD_W, (hd + 1) * HEAD_W)
            vl = slice(hd * V_DIM, (hd + 1) * V_DIM)
            heads.append(_attend(q_s[rows, ql], [k_s[:, ql], kc_s[:, ql]],
                                 [v_s[:, vl], vc_s[:, vl]]).astype(BF16))
        attn = jnp.concatenate(heads, axis=-1)
        o_ref[rows, :] = _mla_out(x_ref[rows, :], attn, mod_ref, g_ref, b_ref, wo_ref)
        return carry

    lax.fori_loop(0, DEC_SEQ // ATTN_Q_TILE, q_block, 0)


def _mla_mixer(x, mod, ln_g, ln_b, w, cache_ckv, cache_kr, *, latent):
    tile = MLA_TILE
    first = N_PROMPT // tile if latent else 0
    n_rows = DEC_BATCH * DEC_SEQ if latent else N_PROMPT
    dkv = w["dkv"] if latent else w["dkv"][:, :KV_LORA + 128]
    in_specs = [
        pl.BlockSpec((tile, D_MODEL), lambda t: (first + t, 0)),
        pl.BlockSpec((None, N_MOD, D_MODEL), lambda t: (_group_of_tile(first + t, tile), 0, 0)),
        _const_spec((1, D_MODEL)),
        _const_spec((1, D_MODEL)),
        _const_spec(dkv.shape),
        _const_spec((1, KV_LORA)),
        _const_spec((D_MODEL, Q_LORA)),
        _const_spec((1, Q_LORA)),
        _const_spec((Q_LORA, N_HEADS * HEAD_W)),
        _const_spec((KV_LORA, N_HEADS * QK_NOPE)),
        _const_spec((KV_LORA, N_HEADS * V_DIM)),
        _const_spec((N_HEADS * V_DIM, D_MODEL)),
    ]
    args = [x, mod, ln_g.reshape(1, D_MODEL), ln_b.reshape(1, D_MODEL), dkv, w["kv_g"], w["dq"],
            w["q_g"], w["uq"], w["uk"], w["uv"], w["o"]]
    out_shape = [jax.ShapeDtypeStruct((n_rows, D_MODEL), F32)]
    out_specs = [pl.BlockSpec((tile, D_MODEL), lambda t: (t, 0))]
    scratch = []
    if latent:
        assert tile == DEC_SEQ
        in_specs += [
            _const_spec((Q_LORA, N_HEADS * 128)),
            _const_spec((DEC_SEQ, 128)),
            _const_spec((DEC_SEQ, 128)),
            pl.BlockSpec((None, PAST_LEN, KV_LORA), lambda t: (t, 0, 0)),
            pl.BlockSpec((None, PAST_LEN, 128), lambda t: (t, 0, 0)),
        ]
        args += [w["uq_swapped"], w["cos"], w["sin"], cache_ckv, cache_kr]
        scratch = [
            pltpu.VMEM((DEC_SEQ, N_HEADS * HEAD_W), BF16),
            pltpu.VMEM((DEC_SEQ, N_HEADS * HEAD_W), BF16),
            pltpu.VMEM((DEC_SEQ, N_HEADS * V_DIM), BF16),
            pltpu.VMEM((PAST_LEN, N_HEADS * HEAD_W), BF16),
            pltpu.VMEM((PAST_LEN, N_HEADS * V_DIM), BF16),
        ]
    else:
        out_shape += [jax.ShapeDtypeStruct((n_rows, KV_LORA), F32),
                      jax.ShapeDtypeStruct((n_rows, QK_ROPE), F32)]
        out_specs += [pl.BlockSpec((tile, KV_LORA), lambda t: (t, 0)),
                      pl.BlockSpec((tile, QK_ROPE), lambda t: (t, 0))]
    outs = pl.pallas_call(
        _mla_latent_body if latent else _mla_prompt_body,
        out_shape=out_shape,
        grid=(n_rows // tile,),
        in_specs=in_specs,
        out_specs=out_specs,
        scratch_shapes=scratch,
        compiler_params=_params(),
        name="mla_latent" if latent else "mla_prompt",
    )(*args)
    return outs[0] if latent else outs


def _rope_swap_perm():
    idx = np.arange(QK_ROPE)
    return np.where((idx % 32) < ROPE_AXIS_PAIRS, idx + ROPE_AXIS_PAIRS, idx - ROPE_AXIS_PAIRS)


def _rope_tables():
    n = DEC_SEQ
    row = jnp.repeat(jnp.arange(n // GRID_W), GRID_W)
    col = jnp.tile(jnp.arange(GRID_W), n // GRID_W)
    inv = ROPE_BASE ** (-jnp.arange(ROPE_AXIS_PAIRS, dtype=F32) / ROPE_AXIS_PAIRS)
    ar = row[:, None] * inv
    ac = col[:, None] * inv
    zeros = jnp.zeros((n, 128 - QK_ROPE), F32)
    cos = jnp.concatenate([jnp.cos(ar), jnp.cos(ar), jnp.cos(ac), jnp.cos(ac), zeros], axis=-1)
    sin = jnp.concatenate([-jnp.sin(ar), jnp.sin(ar), -jnp.sin(ac), jnp.sin(ac), zeros], axis=-1)
    return cos, sin


def _mla_weights(w_dq, q_norm_g, w_uq, w_dkv, kv_norm_g, w_ukv, w_o):
    swap = _rope_swap_perm()
    z64 = jnp.zeros((D_MODEL, 128 - QK_ROPE), F32)
    kr_cols = w_dkv[:, KV_LORA:]
    dkv = jnp.concatenate([w_dkv[:, :KV_LORA], kr_cols, z64, kr_cols[:, swap], z64], axis=-1)
    uq = w_uq.reshape(Q_LORA, N_HEADS, QK_NOPE + QK_ROPE)
    zq = jnp.zeros((Q_LORA, N_HEADS, 128 - QK_ROPE), F32)
    uq_main = jnp.concatenate([uq, zq], axis=-1).reshape(Q_LORA, N_HEADS * HEAD_W)
    uq_swapped = jnp.concatenate([uq[:, :, QK_NOPE:][:, :, swap], zq], axis=-1).reshape(Q_LORA, N_HEADS * 128)
    ukv = w_ukv.reshape(KV_LORA, N_HEADS, QK_NOPE + V_DIM)
    cos, sin = _rope_tables()
    return {
        "dkv": dkv.astype(BF16),
        "kv_g": kv_norm_g.reshape(1, KV_LORA),
        "dq": w_dq.astype(BF16),
        "q_g": q_norm_g.reshape(1, Q_LORA),
        "uq": uq_main.astype(BF16),
        "uq_swapped": uq_swapped.astype(BF16),
        "uk": ukv[:, :, :QK_NOPE].reshape(KV_LORA, N_HEADS * QK_NOPE).astype(BF16),
        "uv": ukv[:, :, QK_NOPE:].reshape(KV_LORA, N_HEADS * V_DIM).astype(BF16),
        "o": w_o.astype(BF16),
        "cos": cos,
        "sin": sin,
    }


def kernel(x_prompt, x_sample, cache_mla_ckv, cache_mla_krope, c, c_ctx, w_ada, b_ada, ln_g, ln_b, ffn_w1, ffn_w3, ffn_w2, cp_w_in, conv_w, conv_b, conv_norm_g, conv_norm_b, pool_w, pool_scale, cp_w_out, mla_w_dq, mla_q_norm_g, mla_w_uq, mla_w_dkv, mla_kv_norm_g, mla_w_ukv, mla_w_o):
    c_all = jnp.concatenate(
        [c_ctx[None, :], c, jnp.zeros((N_GROUPS - 1 - DEC_BATCH, D_MODEL), F32)], axis=0)
    ada = {0: _ada(c_all, w_ada, b_ada, 0)}

    x = (x_prompt.reshape(N_PROMPT, D_MODEL), x_sample.reshape(DEC_BATCH * DEC_SEQ, D_MODEL))
    ffn_f32 = (ffn_w1, ffn_w3, ffn_w2)
    w_bf16 = tuple(wf[0, 0].astype(BF16) for wf in ffn_f32)
    new_ckv, new_krope = [], []
    for i in range(DEPTH):
        if i not in ada:
            ada[i] = _ada(c_all, w_ada, b_ada, i)
        mod = ada[i].reshape(N_GROUPS, N_MOD, D_MODEL)
        j = i // 2
        x, w_bf16 = _ffn(x, mod, ln_g[i, 0], ln_b[i, 0], *w_bf16, ffn_f32, (i, 1), k0=0)
        if i % 2 == 0:
            x, ada_next = _conv_pool_mixer(
                x, mod, ln_g[i, 1], ln_b[i, 1], cp_w_in[j].astype(BF16), conv_w[j], conv_b[j],
                conv_norm_g[j], conv_norm_b[j], pool_w[j].astype(BF16), pool_scale[j],
                cp_w_out[j].astype(BF16), c_all, w_ada, b_ada, min(i + 1, DEPTH - 1))
            if i + 1 < DEPTH:
                ada[i + 1] = ada_next
        else:
            w = _mla_weights(mla_w_dq[j], mla_q_norm_g[j], mla_w_uq[j], mla_w_dkv[j],
                             mla_kv_norm_g[j], mla_w_ukv[j], mla_w_o[j])
            attn_args = (x, mod, ln_g[i, 1], ln_b[i, 1], w)
            y_p, ckv_p, kr_p = _mla_mixer(*attn_args, None, None, latent=False)
            new_ckv.append(ckv_p.reshape(BATCH, SEQ, KV_LORA))
            new_krope.append(kr_p.reshape(BATCH, SEQ, QK_ROPE))
            kr_cache = jnp.pad(cache_mla_krope[:, j], ((0, 0), (0, 0), (0, 128 - QK_ROPE)))
            x = (y_p, _mla_mixer(*attn_args, cache_mla_ckv[:, j], kr_cache, latent=True))
        last = i == DEPTH - 1
        x, w_bf16 = _ffn(x, mod, ln_g[i, 2], ln_b[i, 2], *w_bf16,
                         None if last else ffn_f32, None if last else (i + 1, 0),
                         k0=6, split_out=last)

    y_prompt = x[0].reshape(BATCH, SEQ, D_MODEL)
    y_sample = x[1].reshape(DEC_BATCH, DEC_SEQ, D_MODEL)
    return (y_prompt, y_sample, jnp.stack(new_ckv, axis=1), jnp.stack(new_krope, axis=1))
```

```python
import functools
import math

import jax
import jax.numpy as jnp
import numpy as np
from jax import lax
from jax.experimental import pallas as pl
from jax.experimental.pallas import tpu as pltpu

F32 = jnp.float32
BF16 = jnp.bfloat16

D_MODEL = 1024
BATCH = 32
SEQ = 256
DEPTH = 2
DEC_BATCH = 4
DEC_SEQ = 1024
PAST_LEN = 512
GRID_W = 64
N_MOD = 9
D_FF = 2816
D_CONV = 512
CONV_WIDTH = 31
D_POOL = 512
POOL_WINDOWS = (2, 4, 8, 16)
POOL_GROUP = 128
N_HEADS = 8
QK_NOPE = 128
QK_ROPE = 64
V_DIM = 128
KV_LORA = 256
Q_LORA = 384
ROPE_AXIS_PAIRS = 16
ROPE_BASE = 10000.0
ALPHA = (2 * DEPTH) ** 0.25
LN_EPS = 1e-5
RMS_EPS = 1e-6

N_PROMPT = BATCH * SEQ
N_TOKENS = N_PROMPT + DEC_BATCH * DEC_SEQ
N_GROUPS = 8
HEAD_W = 256
CONV_PAD = 16
CONV_STRIDE = 4
CONV_ROWS = 8 * CONV_STRIDE
CONV_UNROLL = 4

VMEM_LIMIT_BYTES = 60 * 1024 * 1024

FFN_TILES = (1024, 512)
FFN_WORK_BYTES = 4 * 1024 * 1024
FFN_SUB = 256
FFN_UNROLL = 4
MIX_TILE = 1024
MIX_SEG = 256
SEG_ROWS = MIX_SEG + 2 * CONV_PAD
MLA_TILE = 1024
ATTN_Q_TILE = 256
ATTN_BLOCKS_PER_TRIP = 1


def _layer_norm(z, g, b):
    mu = jnp.mean(z, axis=-1, keepdims=True)
    zc = z - mu
    var = jnp.mean(zc * zc, axis=-1, keepdims=True)
    return zc * lax.rsqrt(var + LN_EPS) * g + b


def _rms_norm(z, g):
    return z * lax.rsqrt(jnp.mean(z * z, axis=-1, keepdims=True) + RMS_EPS) * g


def _silu(z):
    return z * jax.nn.sigmoid(z)


def _dot(a, b):
    return jnp.dot(a, b, preferred_element_type=F32)


def _dot_nt(a, b):
    return lax.dot_general(a, b, (((1,), (1,)), ((), ())), preferred_element_type=F32)


def _const_spec(shape):
    nd = len(shape)
    return pl.BlockSpec(shape, lambda *_: (0,) * nd, pipeline_mode=pl.Buffered(1))


def _group_of_tile(t, tile):
    n_p = N_PROMPT // tile
    per_seq = DEC_SEQ // tile
    return jnp.where(t < n_p, 0, 1 + (t - n_p) // per_seq)


def _params(n_axes=1):
    return pltpu.CompilerParams(
        dimension_semantics=("arbitrary",) * n_axes,
        vmem_limit_bytes=VMEM_LIMIT_BYTES,
    )


def _ada_body(c_ref, w_ref, b_ref, o_ref):
    s = _silu(c_ref[...]).astype(BF16)
    o_ref[...] = _dot(s, w_ref[...].astype(BF16)) + b_ref[...]


def _ada(c_all, w_ada, b_ada, layer):
    tn = D_MODEL
    return pl.pallas_call(
        _ada_body,
        out_shape=jax.ShapeDtypeStruct((N_GROUPS, N_MOD * D_MODEL), F32),
        grid=(N_MOD,),
        in_specs=[
            pl.BlockSpec((N_GROUPS, D_MODEL), lambda k: (0, 0)),
            pl.BlockSpec((None, D_MODEL, tn), lambda k: (layer, 0, k)),
            pl.BlockSpec((None, 1, tn), lambda k: (layer, 0, k)),
        ],
        out_specs=pl.BlockSpec((N_GROUPS, tn), lambda k: (0, k)),
        compiler_params=_params(),
        name="ada",
    )(c_all, w_ada, b_ada.reshape(DEPTH, 1, N_MOD * D_MODEL))


def _ffn_body(*refs, k0, split_in, split_out, cast_next):
    refs = list(refs)
    n_x = 2 if split_in else 1
    x_refs, refs = refs[:n_x], refs[n_x:]
    (mod_ref, g_ref, b_ref, w1_ref, w3_ref, w2_ref), refs = refs[:6], refs[6:]
    if cast_next:
        next_f32, refs = refs[:3], refs[3:]
    n_o = 2 if split_out else 1
    o_refs, refs = refs[:n_o], refs[n_o:]
    tile = x_refs[0].shape[0]
    is_prompt = pl.program_id(0) < N_PROMPT // tile
    if cast_next:
        for src, dst in zip(next_f32, refs):
            dst[...] = src[...].astype(BF16)
    shift = mod_ref[k0:k0 + 1, :]
    scale = mod_ref[k0 + 1:k0 + 2, :]
    gate = mod_ref[k0 + 2:k0 + 3, :]
    n_sub = tile // FFN_SUB
    unroll = min(FFN_UNROLL, n_sub)

    def sub_tiles(trip, carry):
        outs = []
        for u in range(unroll):
            start = (trip * unroll + u) * FFN_SUB
            rows = pl.ds(start if isinstance(start, int) else pl.multiple_of(start, FFN_SUB), FFN_SUB)
            if split_in:
                x = jnp.where(is_prompt, x_refs[0][rows, :], x_refs[1][rows, :])
            else:
                x = x_refs[0][rows, :]
            h = (x * (1.0 + scale) + shift).astype(BF16)
            a = (_silu(_dot(h, w1_ref[...])) * _dot(h, w3_ref[...])).astype(BF16)
            y = _dot(a, w2_ref[...])
            out = _layer_norm(ALPHA * x + (0.5 * gate) * y, g_ref[...], b_ref[...])
            if split_out:
                outs.append((rows, out))
            else:
                o_refs[0][rows, :] = out
        if split_out:
            @pl.when(is_prompt)
            def _():
                for rows, out in outs:
                    o_refs[0][rows, :] = out

            @pl.when(jnp.logical_not(is_prompt))
            def _():
                for rows, out in outs:
                    o_refs[1][rows, :] = out
        return carry

    if n_sub == unroll:
        sub_tiles(0, 0)
    else:
        lax.fori_loop(0, n_sub // unroll, sub_tiles, 0)


def _slab_rows(n_rows, n_steps):
    n_slabs = max(d for d in range(1, n_steps + 1) if n_rows % d == 0 and (n_rows // d) % 16 == 0)
    return n_rows // n_slabs, n_slabs


def _ffn_tile(n_in, split_out, next_w):
    for tm in FFN_TILES:
        n_steps = N_TOKENS // tm
        tile_bytes = tm * D_MODEL * 4
        est = 3 * D_MODEL * D_FF * 2
        est += (n_in + (2 if split_out else 1)) * 2 * tile_bytes
        est += FFN_WORK_BYTES + (2 * tile_bytes if split_out else 0)
        if next_w is not None:
            for wn in next_w:
                rows, _ = _slab_rows(wn.shape[2], n_steps)
                est += 2 * rows * wn.shape[3] * (4 + 2)
        if est <= VMEM_LIMIT_BYTES:
            return tm
    raise ValueError("no FFN tile fits in VMEM")


def _ffn(xs, mod, ln_g, ln_b, w1, w3, w2, next_w=None, next_idx=None, *, k0, split_out=False):
    split_in = isinstance(xs, (tuple, list))
    xs = list(xs) if split_in else [xs]
    tm = _ffn_tile(len(xs), split_out, next_w)
    n_steps = N_TOKENS // tm
    n_p = N_PROMPT // tm
    prompt_spec = pl.BlockSpec((tm, D_MODEL), lambda t: (jnp.minimum(t, n_p - 1), 0))
    latent_spec = pl.BlockSpec((tm, D_MODEL), lambda t: (jnp.maximum(t - n_p, 0), 0))
    merged_spec = pl.BlockSpec((tm, D_MODEL), lambda t: (t, 0))
    in_specs = ([prompt_spec, latent_spec] if split_in else [merged_spec]) + [
        pl.BlockSpec((None, N_MOD, D_MODEL), lambda t: (_group_of_tile(t, tm), 0, 0)),
        _const_spec((1, D_MODEL)),
        _const_spec((1, D_MODEL)),
        _const_spec((D_MODEL, D_FF)),
        _const_spec((D_MODEL, D_FF)),
        _const_spec((D_FF, D_MODEL)),
    ]
    args = xs + [mod, ln_g.reshape(1, D_MODEL), ln_b.reshape(1, D_MODEL), w1, w3, w2]
    if split_out:
        out_shape = [jax.ShapeDtypeStruct((N_PROMPT, D_MODEL), F32),
                     jax.ShapeDtypeStruct((N_TOKENS - N_PROMPT, D_MODEL), F32)]
        out_specs = [prompt_spec, latent_spec]
    else:
        out_shape = [jax.ShapeDtypeStruct((N_TOKENS, D_MODEL), F32)]
        out_specs = [merged_spec]
    if next_w is not None:
        li, lj = next_idx
        for wn in next_w:
            n_rows, n_cols = wn.shape[2:]
            rows, n_slabs = _slab_rows(n_rows, n_steps)
            in_specs.append(pl.BlockSpec(
                (None, None, rows, n_cols),
                lambda t, n_slabs=n_slabs: (li, lj, jnp.minimum(t, n_slabs - 1), 0)))
            args.append(wn)
            out_shape.append(jax.ShapeDtypeStruct((n_rows, n_cols), BF16))
            out_specs.append(pl.BlockSpec(
                (rows, n_cols), lambda t, n_slabs=n_slabs: (jnp.minimum(t, n_slabs - 1), 0)))
    outs = pl.pallas_call(
        functools.partial(_ffn_body, k0=k0, split_in=split_in, split_out=split_out,
                          cast_next=next_w is not None),
        out_shape=out_shape,
        grid=(n_steps,),
        in_specs=in_specs,
        out_specs=out_specs,
        compiler_params=_params(),
        name="ffn",
    )(*args)
    n_tok = 2 if split_out else 1
    tok = tuple(outs[:n_tok]) if split_out else outs[0]
    return tok, tuple(outs[n_tok:])


def _strided_rows(start):
    return pl.ds(start, 8, stride=CONV_STRIDE)


def _chunk_rows(j):
    chunks_per_seg = MIX_SEG // CONV_ROWS
    r0 = (j % chunks_per_seg) * CONV_ROWS
    return (j // chunks_per_seg) * SEG_ROWS + r0 + CONV_PAD, j * CONV_ROWS, r0


def _conv_chunk(j, *, slab, cw_ref, cb_ref, pad_ref, conv_ref):
    p0, o0, _ = _chunk_rows(j)
    half = CONV_WIDTH // 2
    n_part = 2
    acc = [[None] * n_part for _ in range(CONV_STRIDE)]
    weights = {}
    for off in range(-half, half + CONV_STRIDE):
        tap = pad_ref[slab, _strided_rows(p0 + off), :]
        for rho in range(CONV_STRIDE):
            k = off - rho + half
            if 0 <= k < CONV_WIDTH:
                if k not in weights:
                    weights[k] = cw_ref[slab, pl.ds(k, 8, stride=0), :]
                term = weights[k] * tap
                part = acc[rho][k % n_part]
                acc[rho][k % n_part] = term if part is None else part + term
    bias = cb_ref[slab, pl.ds(0, 8, stride=0), :]
    for rho in range(CONV_STRIDE):
        conv_ref[slab, _strided_rows(o0 + rho), :] = (acc[rho][0] + acc[rho][1]) + bias


def _pool_chunk(j, *, pad_ref, pool_ref, is_prompt):
    p0, o0, r0 = _chunk_rows(j)
    at_edge = r0 == 0 or r0 == MIX_SEG - CONV_ROWS
    if at_edge:
        seq_len = jnp.where(is_prompt, SEQ, DEC_SEQ)
        t0 = jnp.where(is_prompt, r0, o0) + CONV_STRIDE * lax.broadcasted_iota(jnp.int32, (8, 128), 0)
    for gi, w in enumerate(POOL_WINDOWS):
        left = w // 2
        right = w - 1 - left
        slab = D_CONV // 128 + gi
        total = [None] * CONV_STRIDE
        for off in range(-left, right + CONV_STRIDE):
            tap = pad_ref[slab, _strided_rows(p0 + off), :]
            for rho in range(CONV_STRIDE):
                if -left <= off - rho <= right:
                    total[rho] = tap if total[rho] is None else total[rho] + tap
        for rho in range(CONV_STRIDE):
            if at_edge:
                t = t0 + rho
                cnt = (jnp.minimum(t + right, seq_len - 1) - jnp.maximum(t - left, 0) + 1).astype(F32)
                mean = total[rho] / cnt
            else:
                mean = total[rho] / float(w)
            tok = pad_ref[slab, _strided_rows(p0 + rho), :]
            pool_ref[gi, _strided_rows(o0 + rho), :] = mean - tok


def _cp_body(x_ref, mod_ref, g_ref, b_ref, win_ref, cw_ref, cb_ref, cng_ref, cnb_ref,
             pw_ref, ps_ref, wout_ref, c_ref, wada_ref, bada_ref, o_ref, ada_ref,
             pad_ref, conv_ref, pool_ref):
    is_prompt = pl.program_id(0) < N_PROMPT // MIX_TILE
    _ada_body(c_ref, wada_ref, bada_ref, ada_ref)
    shift = mod_ref[3:4, :]
    scale = mod_ref[4:5, :]
    gate = mod_ref[5:6, :]
    n_seg = MIX_TILE // MIX_SEG
    n_slab = (D_CONV + D_POOL) // 128
    feats = []
    for s in range(n_seg):
        x = x_ref[s * MIX_SEG:(s + 1) * MIX_SEG, :]
        h = (x * (1.0 + scale) + shift).astype(BF16)
        proj = _dot(h, win_ref[...])
        glu = proj[:, :D_CONV] * jax.nn.sigmoid(proj[:, D_CONV:2 * D_CONV])
        feats.append(jnp.concatenate([glu, proj[:, 2 * D_CONV:]], axis=-1))
    zeros = jnp.zeros((CONV_PAD, 128), F32)
    for s in range(n_seg):
        base = s * SEG_ROWS
        for slab in range(n_slab):
            lanes = slice(slab * 128, (slab + 1) * 128)
            pad_ref[slab, base:base + CONV_PAD, :] = (
                zeros if s == 0 else jnp.where(is_prompt, zeros, feats[s - 1][MIX_SEG - CONV_PAD:, lanes]))
            pad_ref[slab, base + CONV_PAD:base + CONV_PAD + MIX_SEG, :] = feats[s][:, lanes]
            pad_ref[slab, base + CONV_PAD + MIX_SEG:base + SEG_ROWS, :] = (
                zeros if s == n_seg - 1 else jnp.where(is_prompt, zeros, feats[s + 1][:CONV_PAD, lanes]))
    chunks_per_seg = MIX_SEG // CONV_ROWS
    for s in range(n_seg):
        rows = slice(s * MIX_SEG, (s + 1) * MIX_SEG)
        seg_chunks = range(s * chunks_per_seg, (s + 1) * chunks_per_seg)
        for slab in range(D_CONV // 128):
            for j in seg_chunks:
                _conv_chunk(j, slab=slab, cw_ref=cw_ref, cb_ref=cb_ref, pad_ref=pad_ref, conv_ref=conv_ref)
        for j in seg_chunks:
            _pool_chunk(j, pad_ref=pad_ref, pool_ref=pool_ref, is_prompt=is_prompt)
        conv = jnp.concatenate([conv_ref[c, rows, :] for c in range(D_CONV // 128)], axis=-1)
        a = _silu(_layer_norm(conv, cng_ref[...], cnb_ref[...]))
        pooled = [_dot(pool_ref[gi, rows, :].astype(BF16), pw_ref[gi]) for gi in range(len(POOL_WINDOWS))]
        bmix = jnp.concatenate(pooled, axis=-1) * ps_ref[...]
        cat = jnp.concatenate([a, bmix], axis=-1).astype(BF16)
        y = _dot(cat, wout_ref[...])
        o_ref[rows, :] = _layer_norm(ALPHA * x_ref[rows, :] + gate * y, g_ref[...], b_ref[...])


def _conv_pool_mixer(x, mod, ln_g, ln_b, w_in, conv_w, conv_b, cn_g, cn_b, pool_w, pool_scale, w_out,
                     c_all, w_ada, b_ada, next_layer):
    tm = MIX_TILE
    n_steps = N_TOKENS // tm
    ada_cols = N_MOD * D_MODEL // n_steps
    assert tm == DEC_SEQ and MIX_SEG == SEQ and N_PROMPT % tm == 0 and ada_cols % 128 == 0
    n_cslab = D_CONV // 128
    row_spec = pl.BlockSpec((tm, D_MODEL), lambda t: (t, 0))
    in_specs = [
        row_spec,
        pl.BlockSpec((None, N_MOD, D_MODEL), lambda t: (_group_of_tile(t, tm), 0, 0)),
        _const_spec((1, D_MODEL)),
        _const_spec((1, D_MODEL)),
        _const_spec((D_MODEL, 2 * D_CONV + D_POOL)),
        _const_spec((n_cslab, CONV_WIDTH, 128)),
        _const_spec((n_cslab, 1, 128)),
        _const_spec((1, D_CONV)),
        _const_spec((1, D_CONV)),
        _const_spec((len(POOL_WINDOWS), POOL_GROUP, POOL_GROUP)),
        _const_spec((1, D_POOL)),
        _const_spec((D_CONV + D_POOL, D_MODEL)),
        _const_spec((N_GROUPS, D_MODEL)),
        pl.BlockSpec((None, D_MODEL, ada_cols), lambda t: (next_layer, 0, t)),
        pl.BlockSpec((None, 1, ada_cols), lambda t: (next_layer, 0, t)),
    ]
    conv_w_slabs = conv_w.reshape(CONV_WIDTH, n_cslab, 128).transpose(1, 0, 2)
    args = [x, mod, ln_g.reshape(1, D_MODEL), ln_b.reshape(1, D_MODEL), w_in, conv_w_slabs,
            conv_b.reshape(n_cslab, 1, 128), cn_g.reshape(1, D_CONV), cn_b.reshape(1, D_CONV),
            pool_w, pool_scale.reshape(1, D_POOL), w_out,
            c_all, w_ada, b_ada.reshape(DEPTH, 1, N_MOD * D_MODEL)]
    return pl.pallas_call(
        _cp_body,
        out_shape=[jax.ShapeDtypeStruct((N_TOKENS, D_MODEL), F32),
                   jax.ShapeDtypeStruct((N_GROUPS, N_MOD * D_MODEL), F32)],
        grid=(n_steps,),
        in_specs=in_specs,
        out_specs=[row_spec, pl.BlockSpec((N_GROUPS, ada_cols), lambda t: (0, t))],
        scratch_shapes=[
            pltpu.VMEM(((D_CONV + D_POOL) // 128, tm // MIX_SEG * SEG_ROWS, 128), F32),
            pltpu.VMEM((n_cslab, tm, 128), F32),
            pltpu.VMEM((D_POOL // 128, tm, 128), F32),
        ],
        compiler_params=_params(),
        name="conv_pool",
    )(*args)


def _attend(q, keys, values):
    scores = [_dot_nt(q, k) for k in keys]
    m = functools.reduce(jnp.maximum, [jnp.max(s, axis=-1, keepdims=True) for s in scores])
    c = (QK_NOPE + QK_ROPE) ** -0.5 * math.log2(math.e)
    e = [jnp.exp2((s - m) * c) for s in scores]
    denom = functools.reduce(jnp.add, [jnp.sum(ei, axis=-1, keepdims=True) for ei in e])
    out = functools.reduce(jnp.add, [_dot(ei.astype(BF16), v) for ei, v in zip(e, values)])
    return out * (1.0 / denom)


def _mla_project(x, mod_ref, wdkv_ref, kvg_ref, wdq_ref, qg_ref, wuq_ref):
    h = (x * (1.0 + mod_ref[4:5, :]) + mod_ref[3:4, :]).astype(BF16)
    kv = _dot(h, wdkv_ref[...])
    ckv = _rms_norm(kv[:, :KV_LORA], kvg_ref[...])
    qd = _rms_norm(_dot(h, wdq_ref[...]), qg_ref[...]).astype(BF16)
    return kv, ckv, qd, _dot(qd, wuq_ref[...])


def _mla_out(x, attn, mod_ref, g_ref, b_ref, wo_ref):
    y = _dot(attn, wo_ref[...])
    return _layer_norm(ALPHA * x + mod_ref[5:6, :] * y, g_ref[...], b_ref[...])


def _mla_prompt_body(x_ref, mod_ref, g_ref, b_ref, wdkv_ref, kvg_ref, wdq_ref, qg_ref, wuq_ref,
                     wuk_ref, wuv_ref, wo_ref, o_ref, ckv_ref, kr_ref):
    for s in range(x_ref.shape[0] // SEQ):
        rows = slice(s * SEQ, (s + 1) * SEQ)
        x = x_ref[rows, :]
        kv, ckv, _, q = _mla_project(x, mod_ref, wdkv_ref, kvg_ref, wdq_ref, qg_ref, wuq_ref)
        ckv_ref[rows, :] = ckv
        kr_ref[rows, :] = kv[:, KV_LORA:KV_LORA + QK_ROPE]
        q16 = q.astype(BF16)
        ckv16 = ckv.astype(BF16)
        kn16 = _dot(ckv16, wuk_ref[...]).astype(BF16)
        v16 = _dot(ckv16, wuv_ref[...]).astype(BF16)
        kr16 = kv[:, KV_LORA:KV_LORA + 128].astype(BF16)
        heads = []
        for hd in range(N_HEADS):
            vl = slice(hd * V_DIM, (hd + 1) * V_DIM)
            k_h = jnp.concatenate([kn16[:, hd * QK_NOPE:(hd + 1) * QK_NOPE], kr16], axis=-1)
            heads.append(_attend(q16[:, hd * HEAD_W:(hd + 1) * HEAD_W], [k_h], [v16[:, vl]]).astype(BF16))
        attn = jnp.concatenate(heads, axis=-1)
        o_ref[rows, :] = _mla_out(x, attn, mod_ref, g_ref, b_ref, wo_ref)


def _store_keys(k_ref, rows, kn16, kr16):
    for hd in range(N_HEADS):
        k_ref[rows, hd * HEAD_W:hd * HEAD_W + QK_NOPE] = kn16[:, hd * QK_NOPE:(hd + 1) * QK_NOPE]
        k_ref[rows, hd * HEAD_W + QK_NOPE:(hd + 1) * HEAD_W] = kr16


def _mla_latent_body(x_ref, mod_ref, g_ref, b_ref, wdkv_ref, kvg_ref, wdq_ref, qg_ref, wuq_ref,
                     wuk_ref, wuv_ref, wo_ref, wuqs_ref, cos_ref, sin_ref, cckv_ref, ckr_ref,
                     o_ref, q_s, k_s, v_s, kc_s, vc_s):
    c16 = cckv_ref[...].astype(BF16)
    vc_s[...] = _dot(c16, wuv_ref[...]).astype(BF16)
    _store_keys(kc_s, slice(None), _dot(c16, wuk_ref[...]).astype(BF16), ckr_ref[...].astype(BF16))
    for s in range(DEC_SEQ // ATTN_Q_TILE):
        rows = slice(s * ATTN_Q_TILE, (s + 1) * ATTN_Q_TILE)
        cos = cos_ref[rows, :]
        sin = sin_ref[rows, :]
        kv, ckv, qd, q = _mla_project(x_ref[rows, :], mod_ref, wdkv_ref, kvg_ref, wdq_ref, qg_ref, wuq_ref)
        q_swapped = _dot(qd, wuqs_ref[...])
        kr = kv[:, KV_LORA:KV_LORA + 128] * cos + kv[:, KV_LORA + 128:KV_LORA + 256] * sin
        ckv16 = ckv.astype(BF16)
        v_s[rows, :] = _dot(ckv16, wuv_ref[...]).astype(BF16)
        _store_keys(k_s, rows, _dot(ckv16, wuk_ref[...]).astype(BF16), kr.astype(BF16))
        for hd in range(N_HEADS):
            lo = hd * HEAD_W
            q_s[rows, lo:lo + QK_NOPE] = q[:, lo:lo + QK_NOPE].astype(BF16)
            q_rope = q[:, lo + QK_NOPE:lo + HEAD_W] * cos + q_swapped[:, hd * 128:(hd + 1) * 128] * sin
            q_s[rows, lo + QK_NOPE:lo + HEAD_W] = q_rope.astype(BF16)

    def q_blocks(i, carry):
        for u in range(ATTN_BLOCKS_PER_TRIP):
            start = (i * ATTN_BLOCKS_PER_TRIP + u) * ATTN_Q_TILE
            rows = pl.ds(pl.multiple_of(start, ATTN_Q_TILE), ATTN_Q_TILE)
            heads = []
            for hd in range(N_HEADS):
                ql = slice(hd * HEAD_W, (hd + 1) * HEAD_W)
                vl = slice(hd * V_DIM, (hd + 1) * V_DIM)
                heads.append(_attend(q_s[rows, ql], [k_s[:, ql], kc_s[:, ql]],
                                     [v_s[:, vl], vc_s[:, vl]]).astype(BF16))
            attn = jnp.concatenate(heads, axis=-1)
            o_ref[rows, :] = _mla_out(x_ref[rows, :], attn, mod_ref, g_ref, b_ref, wo_ref)
        return carry

    lax.fori_loop(0, DEC_SEQ // (ATTN_Q_TILE * ATTN_BLOCKS_PER_TRIP), q_blocks, 0)


def _mla_mixer(x, mod, ln_g, ln_b, w, cache_ckv, cache_kr, *, latent):
    tile = MLA_TILE
    first = N_PROMPT // tile if latent else 0
    n_rows = DEC_BATCH * DEC_SEQ if latent else N_PROMPT
    dkv = w["dkv"] if latent else w["dkv"][:, :KV_LORA + 128]
    in_specs = [
        pl.BlockSpec((tile, D_MODEL), lambda t: (first + t, 0)),
        pl.BlockSpec((None, N_MOD, D_MODEL), lambda t: (_group_of_tile(first + t, tile), 0, 0)),
        _const_spec((1, D_MODEL)),
        _const_spec((1, D_MODEL)),
        _const_spec(dkv.shape),
        _const_spec((1, KV_LORA)),
        _const_spec((D_MODEL, Q_LORA)),
        _const_spec((1, Q_LORA)),
        _const_spec((Q_LORA, N_HEADS * HEAD_W)),
        _const_spec((KV_LORA, N_HEADS * QK_NOPE)),
        _const_spec((KV_LORA, N_HEADS * V_DIM)),
        _const_spec((N_HEADS * V_DIM, D_MODEL)),
    ]
    args = [x, mod, ln_g.reshape(1, D_MODEL), ln_b.reshape(1, D_MODEL), dkv, w["kv_g"], w["dq"],
            w["q_g"], w["uq"], w["uk"], w["uv"], w["o"]]
    out_shape = [jax.ShapeDtypeStruct((n_rows, D_MODEL), F32)]
    out_specs = [pl.BlockSpec((tile, D_MODEL), lambda t: (t, 0))]
    scratch = []
    if latent:
        assert tile == DEC_SEQ
        in_specs += [
            _const_spec((Q_LORA, N_HEADS * 128)),
            _const_spec((DEC_SEQ, 128)),
            _const_spec((DEC_SEQ, 128)),
            pl.BlockSpec((None, PAST_LEN, KV_LORA), lambda t: (t, 0, 0)),
            pl.BlockSpec((None, PAST_LEN, 128), lambda t: (t, 0, 0)),
        ]
        args += [w["uq_swapped"], w["cos"], w["sin"], cache_ckv, cache_kr]
        scratch = [
            pltpu.VMEM((DEC_SEQ, N_HEADS * HEAD_W), BF16),
            pltpu.VMEM((DEC_SEQ, N_HEADS * HEAD_W), BF16),
            pltpu.VMEM((DEC_SEQ, N_HEADS * V_DIM), BF16),
            pltpu.VMEM((PAST_LEN, N_HEADS * HEAD_W), BF16),
            pltpu.VMEM((PAST_LEN, N_HEADS * V_DIM), BF16),
        ]
    else:
        out_shape += [jax.ShapeDtypeStruct((n_rows, KV_LORA), F32),
                      jax.ShapeDtypeStruct((n_rows, QK_ROPE), F32)]
        out_specs += [pl.BlockSpec((tile, KV_LORA), lambda t: (t, 0)),
                      pl.BlockSpec((tile, QK_ROPE), lambda t: (t, 0))]
    outs = pl.pallas_call(
        _mla_latent_body if latent else _mla_prompt_body,
        out_shape=out_shape,
        grid=(n_rows // tile,),
        in_specs=in_specs,
        out_specs=out_specs,
        scratch_shapes=scratch,
        compiler_params=_params(),
        name="mla_latent" if latent else "mla_prompt",
    )(*args)
    return outs[0] if latent else outs


def _rope_swap_perm():
    idx = np.arange(QK_ROPE)
    return np.where((idx % 32) < ROPE_AXIS_PAIRS, idx + ROPE_AXIS_PAIRS, idx - ROPE_AXIS_PAIRS)


def _rope_tables():
    n = DEC_SEQ
    row = np.repeat(np.arange(n // GRID_W), GRID_W)
    col = np.tile(np.arange(GRID_W), n // GRID_W)
    inv = ROPE_BASE ** (-np.arange(ROPE_AXIS_PAIRS, dtype=np.float64) / ROPE_AXIS_PAIRS)
    ar = row[:, None] * inv
    ac = col[:, None] * inv
    zeros = np.zeros((n, 128 - QK_ROPE))
    cos = np.concatenate([np.cos(ar), np.cos(ar), np.cos(ac), np.cos(ac), zeros], axis=-1)
    sin = np.concatenate([-np.sin(ar), np.sin(ar), -np.sin(ac), np.sin(ac), zeros], axis=-1)
    return jnp.asarray(cos, F32), jnp.asarray(sin, F32)


def _mla_weights(w_dq, q_norm_g, w_uq, w_dkv, kv_norm_g, w_ukv, w_o):
    swap = _rope_swap_perm()
    z64 = jnp.zeros((D_MODEL, 128 - QK_ROPE), F32)
    kr_cols = w_dkv[:, KV_LORA:]
    dkv = jnp.concatenate([w_dkv[:, :KV_LORA], kr_cols, z64, kr_cols[:, swap], z64], axis=-1)
    uq = w_uq.reshape(Q_LORA, N_HEADS, QK_NOPE + QK_ROPE)
    zq = jnp.zeros((Q_LORA, N_HEADS, 128 - QK_ROPE), F32)
    uq_main = jnp.concatenate([uq, zq], axis=-1).reshape(Q_LORA, N_HEADS * HEAD_W)
    uq_swapped = jnp.concatenate([uq[:, :, QK_NOPE:][:, :, swap], zq], axis=-1).reshape(Q_LORA, N_HEADS * 128)
    ukv = w_ukv.reshape(KV_LORA, N_HEADS, QK_NOPE + V_DIM)
    cos, sin = _rope_tables()
    return {
        "dkv": dkv.astype(BF16),
        "kv_g": kv_norm_g.reshape(1, KV_LORA),
        "dq": w_dq.astype(BF16),
        "q_g": q_norm_g.reshape(1, Q_LORA),
        "uq": uq_main.astype(BF16),
        "uq_swapped": uq_swapped.astype(BF16),
        "uk": ukv[:, :, :QK_NOPE].reshape(KV_LORA, N_HEADS * QK_NOPE).astype(BF16),
        "uv": ukv[:, :, QK_NOPE:].reshape(KV_LORA, N_HEADS * V_DIM).astype(BF16),
        "o": w_o.astype(BF16),
        "cos": cos,
        "sin": sin,
    }


def kernel(x_prompt, x_sample, cache_mla_ckv, cache_mla_krope, c, c_ctx, w_ada, b_ada, ln_g, ln_b, ffn_w1, ffn_w3, ffn_w2, cp_w_in, conv_w, conv_b, conv_norm_g, conv_norm_b, pool_w, pool_scale, cp_w_out, mla_w_dq, mla_q_norm_g, mla_w_uq, mla_w_dkv, mla_kv_norm_g, mla_w_ukv, mla_w_o):
    c_all = jnp.concatenate(
        [c_ctx[None, :], c, jnp.zeros((N_GROUPS - 1 - DEC_BATCH, D_MODEL), F32)], axis=0)
    ada = {0: _ada(c_all, w_ada, b_ada, 0)}

    x = (x_prompt.reshape(N_PROMPT, D_MODEL), x_sample.reshape(DEC_BATCH * DEC_SEQ, D_MODEL))
    ffn_f32 = (ffn_w1, ffn_w3, ffn_w2)
    w_bf16 = tuple(wf[0, 0].astype(BF16) for wf in ffn_f32)
    new_ckv, new_krope = [], []
    for i in range(DEPTH):
        if i not in ada:
            ada[i] = _ada(c_all, w_ada, b_ada, i)
        mod = ada[i].reshape(N_GROUPS, N_MOD, D_MODEL)
        j = i // 2
        x, w_bf16 = _ffn(x, mod, ln_g[i, 0], ln_b[i, 0], *w_bf16, ffn_f32, (i, 1), k0=0)
        if i % 2 == 0:
            x, ada_next = _conv_pool_mixer(
                x, mod, ln_g[i, 1], ln_b[i, 1], cp_w_in[j].astype(BF16), conv_w[j], conv_b[j],
                conv_norm_g[j], conv_norm_b[j], pool_w[j].astype(BF16), pool_scale[j],
                cp_w_out[j].astype(BF16), c_all, w_ada, b_ada, min(i + 1, DEPTH - 1))
            if i + 1 < DEPTH:
                ada[i + 1] = ada_next
        else:
            w = _mla_weights(mla_w_dq[j], mla_q_norm_g[j], mla_w_uq[j], mla_w_dkv[j],
                             mla_kv_norm_g[j], mla_w_ukv[j], mla_w_o[j])
            attn_args = (x, mod, ln_g[i, 1], ln_b[i, 1], w)
            y_p, ckv_p, kr_p = _mla_mixer(*attn_args, None, None, latent=False)
            new_ckv.append(ckv_p.reshape(BATCH, SEQ, KV_LORA))
            new_krope.append(kr_p.reshape(BATCH, SEQ, QK_ROPE))
            kr_cache = jnp.pad(cache_mla_krope[:, j], ((0, 0), (0, 0), (0, 128 - QK_ROPE)))
            x = (y_p, _mla_mixer(*attn_args, cache_mla_ckv[:, j], kr_cache, latent=True))
        last = i == DEPTH - 1
        x, w_bf16 = _ffn(x, mod, ln_g[i, 2], ln_b[i, 2], *w_bf16,
                         None if last else ffn_f32, None if last else (i + 1, 0),
                         k0=6, split_out=last)

    y_prompt = x[0].reshape(BATCH, SEQ, D_MODEL)
    y_sample = x[1].reshape(DEC_BATCH, DEC_SEQ, D_MODEL)
    return (y_prompt, y_sample, jnp.stack(new_ckv, axis=1), jnp.stack(new_krope, axis=1))
```

```python
import functools
import math

import jax
import jax.numpy as jnp
import numpy as np
from jax import lax
from jax.experimental import pallas as pl
from jax.experimental.pallas import tpu as pltpu

F32 = jnp.float32
BF16 = jnp.bfloat16

D_MODEL = 1024
BATCH = 32
SEQ = 256
DEPTH = 2
DEC_BATCH = 4
DEC_SEQ = 1024
PAST_LEN = 512
GRID_W = 64
N_MOD = 9
D_FF = 2816
D_CONV = 512
CONV_WIDTH = 31
D_POOL = 512
POOL_WINDOWS = (2, 4, 8, 16)
POOL_GROUP = 128
N_HEADS = 8
QK_NOPE = 128
QK_ROPE = 64
V_DIM = 128
KV_LORA = 256
Q_LORA = 384
ROPE_AXIS_PAIRS = 16
ROPE_BASE = 10000.0
ALPHA = (2 * DEPTH) ** 0.25
LN_EPS = 1e-5
RMS_EPS = 1e-6

LANES = 128
SUBLANES = 8

N_PROMPT = BATCH * SEQ
N_TOKENS = N_PROMPT + DEC_BATCH * DEC_SEQ
N_GROUPS = 8
HEAD_W = 256
CONV_PAD = 16
CONV_STRIDE = 4
CONV_ROWS = SUBLANES * CONV_STRIDE

VMEM_LIMIT_BYTES = 60 * 1024 * 1024

FFN_TILES = (1024, 512)
FFN_WORK_BYTES = 4 * 1024 * 1024
FFN_STAGE_BYTES = 64 * D_FF * 4
FFN_SUB = 256
MIX_TILE = 1024
MIX_SEG = 256
SEG_ROWS = MIX_SEG + 2 * CONV_PAD
MLA_TILE = 1024
ATTN_Q_TILE = 256


def _layer_norm(z, g, b):
    mu = jnp.mean(z, axis=-1, keepdims=True)
    zc = z - mu
    var = jnp.mean(zc * zc, axis=-1, keepdims=True)
    return zc * lax.rsqrt(var + LN_EPS) * g + b


def _rms_norm(z, g):
    return z * lax.rsqrt(jnp.mean(z * z, axis=-1, keepdims=True) + RMS_EPS) * g


def _silu(z):
    return z * jax.nn.sigmoid(z)


def _dot(a, b):
    return jnp.dot(a, b, preferred_element_type=F32)


def _dot_nt(a, b):
    return lax.dot_general(a, b, (((1,), (1,)), ((), ())), preferred_element_type=F32)


def _const_spec(shape):
    nd = len(shape)
    return pl.BlockSpec(shape, lambda *_: (0,) * nd, pipeline_mode=pl.Buffered(1))


def _group_of_tile(t, tile):
    n_p = N_PROMPT // tile
    per_seq = DEC_SEQ // tile
    return jnp.where(t < n_p, 0, 1 + (t - n_p) // per_seq)


def _params(n_axes=1):
    return pltpu.CompilerParams(
        dimension_semantics=("arbitrary",) * n_axes,
        vmem_limit_bytes=VMEM_LIMIT_BYTES,
    )


def _ada_body(c_ref, w_ref, b_ref, o_ref):
    s = _silu(c_ref[...]).astype(BF16)
    o_ref[...] = _dot(s, w_ref[...].astype(BF16)) + b_ref[...]


def _ada(c_all, w_ada, b_ada, layer):
    tn = D_MODEL
    return pl.pallas_call(
        _ada_body,
        out_shape=jax.ShapeDtypeStruct((N_GROUPS, N_MOD * D_MODEL), F32),
        grid=(N_MOD,),
        in_specs=[
            pl.BlockSpec((N_GROUPS, D_MODEL), lambda k: (0, 0)),
            pl.BlockSpec((None, D_MODEL, tn), lambda k: (layer, 0, k)),
            pl.BlockSpec((None, 1, tn), lambda k: (layer, 0, k)),
        ],
        out_specs=pl.BlockSpec((N_GROUPS, tn), lambda k: (0, k)),
        compiler_params=_params(),
        name="ada",
    )(c_all, w_ada, b_ada.reshape(DEPTH, 1, N_MOD * D_MODEL))


def _stream_cast(src, dst, stage, sem):
    slab = stage.shape[1]
    n_slabs = src.shape[0] // slab

    def copy(i, slot):
        return pltpu.make_async_copy(src.at[pl.ds(i * slab, slab), :], stage.at[slot], sem.at[slot])

    copy(0, 0).start()

    def step(i, carry):
        slot = i % 2

        @pl.when(i + 1 < n_slabs)
        def _():
            copy(i + 1, 1 - slot).start()

        copy(i, slot).wait()
        dst[pl.ds(pl.multiple_of(i * slab, slab), slab), :] = stage[slot].astype(BF16)
        return carry

    lax.fori_loop(0, n_slabs, step, 0)


def _ffn_body(*refs, k0, split_in, split_out, cast_next, own_idx):
    refs = list(refs)
    n_x = 2 if split_in else 1
    x_refs, refs = refs[:n_x], refs[n_x:]
    (mod_ref, g_ref, b_ref, w1_ref, w3_ref, w2_ref), refs = refs[:6], refs[6:]
    if cast_next:
        next_f32, refs = refs[:3], refs[3:]
    n_o = 2 if split_out else 1
    o_refs, refs = refs[:n_o], refs[n_o:]
    if cast_next:
        next_bf16, refs = refs[:3], refs[3:]
    tile = x_refs[0].shape[0]
    is_prompt = pl.program_id(0) < N_PROMPT // tile
    if own_idx is not None:
        w1_s, w3_s, w2_s, stage_wide, stage_narrow, sem = refs

        @pl.when(pl.program_id(0) == 0)
        def _():
            _stream_cast(w1_ref.at[own_idx], w1_s, stage_wide, sem)
            _stream_cast(w3_ref.at[own_idx], w3_s, stage_wide, sem)
            _stream_cast(w2_ref.at[own_idx], w2_s, stage_narrow, sem)

        w1_ref, w3_ref, w2_ref = w1_s, w3_s, w2_s
    if cast_next:
        for src, dst in zip(next_f32, next_bf16):
            dst[...] = src[...].astype(BF16)
    shift = mod_ref[k0:k0 + 1, :]
    scale = mod_ref[k0 + 1:k0 + 2, :]
    gate = mod_ref[k0 + 2:k0 + 3, :]
    outs = []
    for s in range(tile // FFN_SUB):
        rows = slice(s * FFN_SUB, (s + 1) * FFN_SUB)
        if split_in:
            x = jnp.where(is_prompt, x_refs[0][rows, :], x_refs[1][rows, :])
        else:
            x = x_refs[0][rows, :]
        h = (x * (1.0 + scale) + shift).astype(BF16)
        a = (_silu(_dot(h, w1_ref[...])) * _dot(h, w3_ref[...])).astype(BF16)
        y = _dot(a, w2_ref[...])
        out = _layer_norm(ALPHA * x + (0.5 * gate) * y, g_ref[...], b_ref[...])
        if split_out:
            outs.append((rows, out))
        else:
            o_refs[0][rows, :] = out
    if split_out:
        @pl.when(is_prompt)
        def _():
            for rows, out in outs:
                o_refs[0][rows, :] = out

        @pl.when(jnp.logical_not(is_prompt))
        def _():
            for rows, out in outs:
                o_refs[1][rows, :] = out


def _slab_rows(n_rows, n_steps):
    packed_rows = 2 * SUBLANES
    n_slabs = max(d for d in range(1, n_steps + 1)
                  if n_rows % d == 0 and (n_rows // d) % packed_rows == 0)
    return n_rows // n_slabs, n_slabs


def _ffn_tile(n_in, split_out, next_w, stream_own):
    for tm in FFN_TILES:
        n_steps = N_TOKENS // tm
        tile_bytes = tm * D_MODEL * 4
        est = 3 * D_MODEL * D_FF * 2
        est += 2 * 2 * FFN_STAGE_BYTES if stream_own else 0
        est += (n_in + (2 if split_out else 1)) * 2 * tile_bytes
        est += FFN_WORK_BYTES + (2 * tile_bytes if split_out else 0)
        if next_w is not None:
            for wn in next_w:
                rows, _ = _slab_rows(wn.shape[2], n_steps)
                est += 2 * rows * wn.shape[3] * (4 + 2)
        if est <= VMEM_LIMIT_BYTES:
            return tm
    raise ValueError("no FFN tile fits in VMEM")


def _ffn(xs, mod, ln_g, ln_b, w1, w3, w2, next_w=None, next_idx=None, *, k0, split_out=False,
         own_idx=None):
    split_in = isinstance(xs, (tuple, list))
    xs = list(xs) if split_in else [xs]
    tm = _ffn_tile(len(xs), split_out, next_w, own_idx is not None)
    n_steps = N_TOKENS // tm
    n_p = N_PROMPT // tm
    prompt_spec = pl.BlockSpec((tm, D_MODEL), lambda t: (jnp.minimum(t, n_p - 1), 0))
    latent_spec = pl.BlockSpec((tm, D_MODEL), lambda t: (jnp.maximum(t - n_p, 0), 0))
    merged_spec = pl.BlockSpec((tm, D_MODEL), lambda t: (t, 0))
    if own_idx is None:
        w_specs = [_const_spec((D_MODEL, D_FF)), _const_spec((D_MODEL, D_FF)), _const_spec((D_FF, D_MODEL))]
        scratch = []
    else:
        w_specs = [pl.BlockSpec(memory_space=pl.ANY)] * 3
        wide_rows = FFN_STAGE_BYTES // (D_FF * 4)
        narrow_rows = FFN_STAGE_BYTES // (D_MODEL * 4)
        assert D_MODEL % wide_rows == 0 and D_FF % narrow_rows == 0
        scratch = [
            pltpu.VMEM((D_MODEL, D_FF), BF16),
            pltpu.VMEM((D_MODEL, D_FF), BF16),
            pltpu.VMEM((D_FF, D_MODEL), BF16),
            pltpu.VMEM((2, wide_rows, D_FF), F32),
            pltpu.VMEM((2, narrow_rows, D_MODEL), F32),
            pltpu.SemaphoreType.DMA((2,)),
        ]
    in_specs = ([prompt_spec, latent_spec] if split_in else [merged_spec]) + [
        pl.BlockSpec((None, N_MOD, D_MODEL), lambda t: (_group_of_tile(t, tm), 0, 0)),
        _const_spec((1, D_MODEL)),
        _const_spec((1, D_MODEL)),
    ] + w_specs
    args = xs + [mod, ln_g.reshape(1, D_MODEL), ln_b.reshape(1, D_MODEL), w1, w3, w2]
    if split_out:
        out_shape = [jax.ShapeDtypeStruct((N_PROMPT, D_MODEL), F32),
                     jax.ShapeDtypeStruct((N_TOKENS - N_PROMPT, D_MODEL), F32)]
        out_specs = [prompt_spec, latent_spec]
    else:
        out_shape = [jax.ShapeDtypeStruct((N_TOKENS, D_MODEL), F32)]
        out_specs = [merged_spec]
    if next_w is not None:
        li, lj = next_idx
        for wn in next_w:
            n_rows, n_cols = wn.shape[2:]
            rows, n_slabs = _slab_rows(n_rows, n_steps)
            in_specs.append(pl.BlockSpec(
                (None, None, rows, n_cols),
                lambda t, n_slabs=n_slabs: (li, lj, jnp.minimum(t, n_slabs - 1), 0)))
            args.append(wn)
            out_shape.append(jax.ShapeDtypeStruct((n_rows, n_cols), BF16))
            out_specs.append(pl.BlockSpec(
                (rows, n_cols), lambda t, n_slabs=n_slabs: (jnp.minimum(t, n_slabs - 1), 0)))
    outs = pl.pallas_call(
        functools.partial(_ffn_body, k0=k0, split_in=split_in, split_out=split_out,
                          cast_next=next_w is not None, own_idx=own_idx),
        out_shape=out_shape,
        grid=(n_steps,),
        in_specs=in_specs,
        out_specs=out_specs,
        scratch_shapes=scratch,
        compiler_params=_params(),
        name="ffn",
    )(*args)
    n_tok = 2 if split_out else 1
    tok = tuple(outs[:n_tok]) if split_out else outs[0]
    return tok, tuple(outs[n_tok:])


def _strided_rows(start):
    return pl.ds(start, SUBLANES, stride=CONV_STRIDE)


def _chunk_rows(j):
    chunks_per_seg = MIX_SEG // CONV_ROWS
    r0 = (j % chunks_per_seg) * CONV_ROWS
    return (j // chunks_per_seg) * SEG_ROWS + r0 + CONV_PAD, j * CONV_ROWS, r0


def _conv_chunk(j, *, slab, cw_ref, cb_ref, pad_ref, conv_ref):
    p0, o0, _ = _chunk_rows(j)
    half = CONV_WIDTH // 2
    n_part = 2
    acc = [[None] * n_part for _ in range(CONV_STRIDE)]
    weights = {}
    for off in range(-half, half + CONV_STRIDE):
        tap = pad_ref[slab, _strided_rows(p0 + off), :]
        for rho in range(CONV_STRIDE):
            k = off - rho + half
            if 0 <= k < CONV_WIDTH:
                if k not in weights:
                    weights[k] = cw_ref[slab, pl.ds(k, SUBLANES, stride=0), :]
                term = weights[k] * tap
                part = acc[rho][k % n_part]
                acc[rho][k % n_part] = term if part is None else part + term
    bias = cb_ref[slab, pl.ds(0, SUBLANES, stride=0), :]
    for rho in range(CONV_STRIDE):
        conv_ref[slab, _strided_rows(o0 + rho), :] = (acc[rho][0] + acc[rho][1]) + bias


def _pool_chunk(j, *, pad_ref, pool_ref, is_prompt):
    p0, o0, r0 = _chunk_rows(j)
    at_edge = r0 == 0 or r0 == MIX_SEG - CONV_ROWS
    if at_edge:
        seq_len = jnp.where(is_prompt, SEQ, DEC_SEQ)
        t0 = jnp.where(is_prompt, r0, o0) + CONV_STRIDE * lax.broadcasted_iota(jnp.int32, (SUBLANES, LANES), 0)
    for gi, w in enumerate(POOL_WINDOWS):
        left = w // 2
        right = w - 1 - left
        slab = D_CONV // LANES + gi
        total = [None] * CONV_STRIDE
        for off in range(-left, right + CONV_STRIDE):
            tap = pad_ref[slab, _strided_rows(p0 + off), :]
            for rho in range(CONV_STRIDE):
                if -left <= off - rho <= right:
                    total[rho] = tap if total[rho] is None else total[rho] + tap
        for rho in range(CONV_STRIDE):
            if at_edge:
                t = t0 + rho
                cnt = (jnp.minimum(t + right, seq_len - 1) - jnp.maximum(t - left, 0) + 1).astype(F32)
                mean = total[rho] / cnt
            else:
                mean = total[rho] / float(w)
            tok = pad_ref[slab, _strided_rows(p0 + rho), :]
            pool_ref[gi, _strided_rows(o0 + rho), :] = mean - tok


def _cp_body(x_ref, mod_ref, g_ref, b_ref, win_ref, cw_ref, cb_ref, cng_ref, cnb_ref,
             pw_ref, ps_ref, wout_ref, c_ref, wada_ref, bada_ref, o_ref, ada_ref,
             pad_ref, conv_ref, pool_ref):
    is_prompt = pl.program_id(0) < N_PROMPT // MIX_TILE
    _ada_body(c_ref, wada_ref, bada_ref, ada_ref)
    shift = mod_ref[3:4, :]
    scale = mod_ref[4:5, :]
    gate = mod_ref[5:6, :]
    n_seg = MIX_TILE // MIX_SEG
    n_slab = (D_CONV + D_POOL) // LANES
    feats = []
    for s in range(n_seg):
        x = x_ref[s * MIX_SEG:(s + 1) * MIX_SEG, :]
        h = (x * (1.0 + scale) + shift).astype(BF16)
        proj = _dot(h, win_ref[...])
        glu = proj[:, :D_CONV] * jax.nn.sigmoid(proj[:, D_CONV:2 * D_CONV])
        feats.append(jnp.concatenate([glu, proj[:, 2 * D_CONV:]], axis=-1))
    zeros = jnp.zeros((CONV_PAD, LANES), F32)
    for s in range(n_seg):
        base = s * SEG_ROWS
        for slab in range(n_slab):
            lanes = slice(slab * LANES, (slab + 1) * LANES)
            pad_ref[slab, base:base + CONV_PAD, :] = (
                zeros if s == 0 else jnp.where(is_prompt, zeros, feats[s - 1][MIX_SEG - CONV_PAD:, lanes]))
            pad_ref[slab, base + CONV_PAD:base + CONV_PAD + MIX_SEG, :] = feats[s][:, lanes]
            pad_ref[slab, base + CONV_PAD + MIX_SEG:base + SEG_ROWS, :] = (
                zeros if s == n_seg - 1 else jnp.where(is_prompt, zeros, feats[s + 1][:CONV_PAD, lanes]))
    chunks_per_seg = MIX_SEG // CONV_ROWS
    for s in range(n_seg):
        rows = slice(s * MIX_SEG, (s + 1) * MIX_SEG)
        seg_chunks = range(s * chunks_per_seg, (s + 1) * chunks_per_seg)
        for slab in range(D_CONV // LANES):
            for j in seg_chunks:
                _conv_chunk(j, slab=slab, cw_ref=cw_ref, cb_ref=cb_ref, pad_ref=pad_ref, conv_ref=conv_ref)
        for j in seg_chunks:
            _pool_chunk(j, pad_ref=pad_ref, pool_ref=pool_ref, is_prompt=is_prompt)
        conv = jnp.concatenate([conv_ref[c, rows, :] for c in range(D_CONV // LANES)], axis=-1)
        a = _silu(_layer_norm(conv, cng_ref[...], cnb_ref[...]))
        pooled = [_dot(pool_ref[gi, rows, :].astype(BF16), pw_ref[gi]) for gi in range(len(POOL_WINDOWS))]
        bmix = jnp.concatenate(pooled, axis=-1) * ps_ref[...]
        cat = jnp.concatenate([a, bmix], axis=-1).astype(BF16)
        y = _dot(cat, wout_ref[...])
        o_ref[rows, :] = _layer_norm(ALPHA * x_ref[rows, :] + gate * y, g_ref[...], b_ref[...])


def _conv_pool_mixer(x, mod, ln_g, ln_b, w_in, conv_w, conv_b, cn_g, cn_b, pool_w, pool_scale, w_out,
                     c_all, w_ada, b_ada, next_layer):
    tm = MIX_TILE
    n_steps = N_TOKENS // tm
    ada_cols = N_MOD * D_MODEL // n_steps
    assert tm == DEC_SEQ and MIX_SEG == SEQ and N_PROMPT % tm == 0 and ada_cols % LANES == 0
    n_cslab = D_CONV // LANES
    row_spec = pl.BlockSpec((tm, D_MODEL), lambda t: (t, 0))
    in_specs = [
        row_spec,
        pl.BlockSpec((None, N_MOD, D_MODEL), lambda t: (_group_of_tile(t, tm), 0, 0)),
        _const_spec((1, D_MODEL)),
        _const_spec((1, D_MODEL)),
        _const_spec((D_MODEL, 2 * D_CONV + D_POOL)),
        _const_spec((n_cslab, CONV_WIDTH, LANES)),
        _const_spec((n_cslab, 1, LANES)),
        _const_spec((1, D_CONV)),
        _const_spec((1, D_CONV)),
        _const_spec((len(POOL_WINDOWS), POOL_GROUP, POOL_GROUP)),
        _const_spec((1, D_POOL)),
        _const_spec((D_CONV + D_POOL, D_MODEL)),
        _const_spec((N_GROUPS, D_MODEL)),
        pl.BlockSpec((None, D_MODEL, ada_cols), lambda t: (next_layer, 0, t)),
        pl.BlockSpec((None, 1, ada_cols), lambda t: (next_layer, 0, t)),
    ]
    conv_w_slabs = conv_w.reshape(CONV_WIDTH, n_cslab, LANES).transpose(1, 0, 2)
    args = [x, mod, ln_g.reshape(1, D_MODEL), ln_b.reshape(1, D_MODEL), w_in, conv_w_slabs,
            conv_b.reshape(n_cslab, 1, LANES), cn_g.reshape(1, D_CONV), cn_b.reshape(1, D_CONV),
            pool_w, pool_scale.reshape(1, D_POOL), w_out,
            c_all, w_ada, b_ada.reshape(DEPTH, 1, N_MOD * D_MODEL)]
    return pl.pallas_call(
        _cp_body,
        out_shape=[jax.ShapeDtypeStruct((N_TOKENS, D_MODEL), F32),
                   jax.ShapeDtypeStruct((N_GROUPS, N_MOD * D_MODEL), F32)],
        grid=(n_steps,),
        in_specs=in_specs,
        out_specs=[row_spec, pl.BlockSpec((N_GROUPS, ada_cols), lambda t: (0, t))],
        scratch_shapes=[
            pltpu.VMEM(((D_CONV + D_POOL) // LANES, tm // MIX_SEG * SEG_ROWS, LANES), F32),
            pltpu.VMEM((n_cslab, tm, LANES), F32),
            pltpu.VMEM((D_POOL // LANES, tm, LANES), F32),
        ],
        compiler_params=_params(),
        name="conv_pool",
    )(*args)


def _attend(q, keys, values):
    scores = [_dot_nt(q, k) for k in keys]
    m = functools.reduce(jnp.maximum, [jnp.max(s, axis=-1, keepdims=True) for s in scores])
    c = (QK_NOPE + QK_ROPE) ** -0.5 * math.log2(math.e)
    e = [jnp.exp2((s - m) * c) for s in scores]
    denom = functools.reduce(jnp.add, [jnp.sum(ei, axis=-1, keepdims=True) for ei in e])
    out = functools.reduce(jnp.add, [_dot(ei.astype(BF16), v) for ei, v in zip(e, values)])
    return out * (1.0 / denom)


def _mla_project(x, mod_ref, wdkv_ref, kvg_ref, wdq_ref, qg_ref, wuq_ref):
    h = (x * (1.0 + mod_ref[4:5, :]) + mod_ref[3:4, :]).astype(BF16)
    kv = _dot(h, wdkv_ref[...])
    ckv = _rms_norm(kv[:, :KV_LORA], kvg_ref[...])
    qd = _rms_norm(_dot(h, wdq_ref[...]), qg_ref[...]).astype(BF16)
    return kv, ckv, qd, _dot(qd, wuq_ref[...])


def _mla_out(x, attn, mod_ref, g_ref, b_ref, wo_ref):
    y = _dot(attn, wo_ref[...])
    return _layer_norm(ALPHA * x + mod_ref[5:6, :] * y, g_ref[...], b_ref[...])


def _mla_prompt_body(x_ref, mod_ref, g_ref, b_ref, wdkv_ref, kvg_ref, wdq_ref, qg_ref, wuq_ref,
                     wuk_ref, wuv_ref, wo_ref, o_ref, ckv_ref, kr_ref):
    for s in range(x_ref.shape[0] // SEQ):
        rows = slice(s * SEQ, (s + 1) * SEQ)
        x = x_ref[rows, :]
        kv, ckv, _, q = _mla_project(x, mod_ref, wdkv_ref, kvg_ref, wdq_ref, qg_ref, wuq_ref)
        ckv_ref[rows, :] = ckv
        kr_ref[rows, :] = kv[:, KV_LORA:KV_LORA + QK_ROPE]
        q16 = q.astype(BF16)
        ckv16 = ckv.astype(BF16)
        kn16 = _dot(ckv16, wuk_ref[...]).astype(BF16)
        v16 = _dot(ckv16, wuv_ref[...]).astype(BF16)
        kr16 = kv[:, KV_LORA:KV_LORA + LANES].astype(BF16)
        heads = []
        for hd in range(N_HEADS):
            vl = slice(hd * V_DIM, (hd + 1) * V_DIM)
            k_h = jnp.concatenate([kn16[:, hd * QK_NOPE:(hd + 1) * QK_NOPE], kr16], axis=-1)
            heads.append(_attend(q16[:, hd * HEAD_W:(hd + 1) * HEAD_W], [k_h], [v16[:, vl]]).astype(BF16))
        attn = jnp.concatenate(heads, axis=-1)
        o_ref[rows, :] = _mla_out(x, attn, mod_ref, g_ref, b_ref, wo_ref)


def _store_keys(k_ref, rows, kn16, kr16):
    for hd in range(N_HEADS):
        k_ref[rows, hd * HEAD_W:hd * HEAD_W + QK_NOPE] = kn16[:, hd * QK_NOPE:(hd + 1) * QK_NOPE]
        k_ref[rows, hd * HEAD_W + QK_NOPE:(hd + 1) * HEAD_W] = kr16


def _mla_latent_body(x_ref, mod_ref, g_ref, b_ref, wdkv_ref, kvg_ref, wdq_ref, qg_ref, wuq_ref,
                     wuk_ref, wuv_ref, wo_ref, wuqs_ref, cos_ref, sin_ref, cckv_ref, ckr_ref,
                     o_ref, q_s, k_s, v_s, kc_s, vc_s):
    c16 = cckv_ref[...].astype(BF16)
    vc_s[...] = _dot(c16, wuv_ref[...]).astype(BF16)
    _store_keys(kc_s, slice(None), _dot(c16, wuk_ref[...]).astype(BF16), ckr_ref[...].astype(BF16))
    for s in range(DEC_SEQ // ATTN_Q_TILE):
        rows = slice(s * ATTN_Q_TILE, (s + 1) * ATTN_Q_TILE)
        cos = cos_ref[rows, :]
        sin = sin_ref[rows, :]
        kv, ckv, qd, q = _mla_project(x_ref[rows, :], mod_ref, wdkv_ref, kvg_ref, wdq_ref, qg_ref, wuq_ref)
        q_swapped = _dot(qd, wuqs_ref[...])
        kr = kv[:, KV_LORA:KV_LORA + LANES] * cos + kv[:, KV_LORA + LANES:KV_LORA + 2 * LANES] * sin
        ckv16 = ckv.astype(BF16)
        v_s[rows, :] = _dot(ckv16, wuv_ref[...]).astype(BF16)
        _store_keys(k_s, rows, _dot(ckv16, wuk_ref[...]).astype(BF16), kr.astype(BF16))
        for hd in range(N_HEADS):
            lo = hd * HEAD_W
            q_s[rows, lo:lo + QK_NOPE] = q[:, lo:lo + QK_NOPE].astype(BF16)
            q_rope = q[:, lo + QK_NOPE:lo + HEAD_W] * cos + q_swapped[:, hd * LANES:(hd + 1) * LANES] * sin
            q_s[rows, lo + QK_NOPE:lo + HEAD_W] = q_rope.astype(BF16)

    def q_block(i, carry):
        rows = pl.ds(pl.multiple_of(i * ATTN_Q_TILE, ATTN_Q_TILE), ATTN_Q_TILE)
        heads = []
        for hd in range(N_HEADS):
            ql = slice(hd * HEAD_W, (hd + 1) * HEAD_W)
            vl = slice(hd * V_DIM, (hd + 1) * V_DIM)
            heads.append(_attend(q_s[rows, ql], [k_s[:, ql], kc_s[:, ql]],
                                 [v_s[:, vl], vc_s[:, vl]]).astype(BF16))
        attn = jnp.concatenate(heads, axis=-1)
        o_ref[rows, :] = _mla_out(x_ref[rows, :], attn, mod_ref, g_ref, b_ref, wo_ref)
        return carry

    lax.fori_loop(0, DEC_SEQ // ATTN_Q_TILE, q_block, 0)


def _mla_mixer(x, mod, ln_g, ln_b, w, cache_ckv, cache_kr, *, latent):
    tile = MLA_TILE
    first = N_PROMPT // tile if latent else 0
    n_rows = DEC_BATCH * DEC_SEQ if latent else N_PROMPT
    dkv = w["dkv"] if latent else w["dkv"][:, :KV_LORA + LANES]
    in_specs = [
        pl.BlockSpec((tile, D_MODEL), lambda t: (first + t, 0)),
        pl.BlockSpec((None, N_MOD, D_MODEL), lambda t: (_group_of_tile(first + t, tile), 0, 0)),
        _const_spec((1, D_MODEL)),
        _const_spec((1, D_MODEL)),
        _const_spec(dkv.shape),
        _const_spec((1, KV_LORA)),
        _const_spec((D_MODEL, Q_LORA)),
        _const_spec((1, Q_LORA)),
        _const_spec((Q_LORA, N_HEADS * HEAD_W)),
        _const_spec((KV_LORA, N_HEADS * QK_NOPE)),
        _const_spec((KV_LORA, N_HEADS * V_DIM)),
        _const_spec((N_HEADS * V_DIM, D_MODEL)),
    ]
    args = [x, mod, ln_g.reshape(1, D_MODEL), ln_b.reshape(1, D_MODEL), dkv, w["kv_g"], w["dq"],
            w["q_g"], w["uq"], w["uk"], w["uv"], w["o"]]
    out_shape = [jax.ShapeDtypeStruct((n_rows, D_MODEL), F32)]
    out_specs = [pl.BlockSpec((tile, D_MODEL), lambda t: (t, 0))]
    scratch = []
    if latent:
        assert tile == DEC_SEQ
        in_specs += [
            _const_spec((Q_LORA, N_HEADS * LANES)),
            _const_spec((DEC_SEQ, LANES)),
            _const_spec((DEC_SEQ, LANES)),
            pl.BlockSpec((None, PAST_LEN, KV_LORA), lambda t: (t, 0, 0)),
            pl.BlockSpec((None, PAST_LEN, LANES), lambda t: (t, 0, 0)),
        ]
        args += [w["uq_swapped"], w["cos"], w["sin"], cache_ckv, cache_kr]
        scratch = [
            pltpu.VMEM((DEC_SEQ, N_HEADS * HEAD_W), BF16),
            pltpu.VMEM((DEC_SEQ, N_HEADS * HEAD_W), BF16),
            pltpu.VMEM((DEC_SEQ, N_HEADS * V_DIM), BF16),
            pltpu.VMEM((PAST_LEN, N_HEADS * HEAD_W), BF16),
            pltpu.VMEM((PAST_LEN, N_HEADS * V_DIM), BF16),
        ]
    else:
        out_shape += [jax.ShapeDtypeStruct((n_rows, KV_LORA), F32),
                      jax.ShapeDtypeStruct((n_rows, QK_ROPE), F32)]
        out_specs += [pl.BlockSpec((tile, KV_LORA), lambda t: (t, 0)),
                      pl.BlockSpec((tile, QK_ROPE), lambda t: (t, 0))]
    outs = pl.pallas_call(
        _mla_latent_body if latent else _mla_prompt_body,
        out_shape=out_shape,
        grid=(n_rows // tile,),
        in_specs=in_specs,
        out_specs=out_specs,
        scratch_shapes=scratch,
        compiler_params=_params(),
        name="mla_latent" if latent else "mla_prompt",
    )(*args)
    return outs[0] if latent else outs


def _rope_swap_perm():
    idx = np.arange(QK_ROPE)
    return np.where((idx % 32) < ROPE_AXIS_PAIRS, idx + ROPE_AXIS_PAIRS, idx - ROPE_AXIS_PAIRS)


def _rope_tables():
    n = DEC_SEQ
    row = np.repeat(np.arange(n // GRID_W), GRID_W)
    col = np.tile(np.arange(GRID_W), n // GRID_W)
    inv = ROPE_BASE ** (-np.arange(ROPE_AXIS_PAIRS, dtype=np.float64) / ROPE_AXIS_PAIRS)
    ar = row[:, None] * inv
    ac = col[:, None] * inv
    zeros = np.zeros((n, LANES - QK_ROPE))
    cos = np.concatenate([np.cos(ar), np.cos(ar), np.cos(ac), np.cos(ac), zeros], axis=-1)
    sin = np.concatenate([-np.sin(ar), np.sin(ar), -np.sin(ac), np.sin(ac), zeros], axis=-1)
    return jnp.asarray(cos, F32), jnp.asarray(sin, F32)


def _mla_weights(w_dq, q_norm_g, w_uq, w_dkv, kv_norm_g, w_ukv, w_o):
    swap = _rope_swap_perm()
    z64 = jnp.zeros((D_MODEL, LANES - QK_ROPE), F32)
    kr_cols = w_dkv[:, KV_LORA:]
    dkv = jnp.concatenate([w_dkv[:, :KV_LORA], kr_cols, z64, kr_cols[:, swap], z64], axis=-1)
    uq = w_uq.reshape(Q_LORA, N_HEADS, QK_NOPE + QK_ROPE)
    zq = jnp.zeros((Q_LORA, N_HEADS, LANES - QK_ROPE), F32)
    uq_main = jnp.concatenate([uq, zq], axis=-1).reshape(Q_LORA, N_HEADS * HEAD_W)
    uq_swapped = jnp.concatenate([uq[:, :, QK_NOPE:][:, :, swap], zq], axis=-1).reshape(Q_LORA, N_HEADS * LANES)
    ukv = w_ukv.reshape(KV_LORA, N_HEADS, QK_NOPE + V_DIM)
    cos, sin = _rope_tables()
    return {
        "dkv": dkv.astype(BF16),
        "kv_g": kv_norm_g.reshape(1, KV_LORA),
        "dq": w_dq.astype(BF16),
        "q_g": q_norm_g.reshape(1, Q_LORA),
        "uq": uq_main.astype(BF16),
        "uq_swapped": uq_swapped.astype(BF16),
        "uk": ukv[:, :, :QK_NOPE].reshape(KV_LORA, N_HEADS * QK_NOPE).astype(BF16),
        "uv": ukv[:, :, QK_NOPE:].reshape(KV_LORA, N_HEADS * V_DIM).astype(BF16),
        "o": w_o.astype(BF16),
        "cos": cos,
        "sin": sin,
    }


def kernel(x_prompt, x_sample, cache_mla_ckv, cache_mla_krope, c, c_ctx, w_ada, b_ada, ln_g, ln_b, ffn_w1, ffn_w3, ffn_w2, cp_w_in, conv_w, conv_b, conv_norm_g, conv_norm_b, pool_w, pool_scale, cp_w_out, mla_w_dq, mla_q_norm_g, mla_w_uq, mla_w_dkv, mla_kv_norm_g, mla_w_ukv, mla_w_o):
    c_all = jnp.concatenate(
        [c_ctx[None, :], c, jnp.zeros((N_GROUPS - 1 - DEC_BATCH, D_MODEL), F32)], axis=0)
    ada = {0: _ada(c_all, w_ada, b_ada, 0)}

    x = (x_prompt.reshape(N_PROMPT, D_MODEL), x_sample.reshape(DEC_BATCH * DEC_SEQ, D_MODEL))
    ffn_f32 = (ffn_w1, ffn_w3, ffn_w2)
    w_own, own_idx = ffn_f32, (0, 0)
    new_ckv, new_krope = [], []
    for i in range(DEPTH):
        if i not in ada:
            ada[i] = _ada(c_all, w_ada, b_ada, i)
        mod = ada[i].reshape(N_GROUPS, N_MOD, D_MODEL)
        j = i // 2
        x, w_bf16 = _ffn(x, mod, ln_g[i, 0], ln_b[i, 0], *w_own, ffn_f32, (i, 1), k0=0, own_idx=own_idx)
        own_idx = None
        if i % 2 == 0:
            x, ada_next = _conv_pool_mixer(
                x, mod, ln_g[i, 1], ln_b[i, 1], cp_w_in[j].astype(BF16), conv_w[j], conv_b[j],
                conv_norm_g[j], conv_norm_b[j], pool_w[j].astype(BF16), pool_scale[j],
                cp_w_out[j].astype(BF16), c_all, w_ada, b_ada, min(i + 1, DEPTH - 1))
            if i + 1 < DEPTH:
                ada[i + 1] = ada_next
        else:
            w = _mla_weights(mla_w_dq[j], mla_q_norm_g[j], mla_w_uq[j], mla_w_dkv[j],
                             mla_kv_norm_g[j], mla_w_ukv[j], mla_w_o[j])
            attn_args = (x, mod, ln_g[i, 1], ln_b[i, 1], w)
            y_p, ckv_p, kr_p = _mla_mixer(*attn_args, None, None, latent=False)
            new_ckv.append(ckv_p.reshape(BATCH, SEQ, KV_LORA))
            new_krope.append(kr_p.reshape(BATCH, SEQ, QK_ROPE))
            kr_cache = jnp.pad(cache_mla_krope[:, j], ((0, 0), (0, 0), (0, LANES - QK_ROPE)))
            x = (y_p, _mla_mixer(*attn_args, cache_mla_ckv[:, j], kr_cache, latent=True))
        last = i == DEPTH - 1
        x, w_own = _ffn(x, mod, ln_g[i, 2], ln_b[i, 2], *w_bf16,
                        None if last else ffn_f32, None if last else (i + 1, 0),
                        k0=6, split_out=last)

    y_prompt = x[0].reshape(BATCH, SEQ, D_MODEL)
    y_sample = x[1].reshape(DEC_BATCH, DEC_SEQ, D_MODEL)
    return (y_prompt, y_sample, jnp.stack(new_ckv, axis=1), jnp.stack(new_krope, axis=1))
```

```python
import functools
import math

import jax
import jax.numpy as jnp
import numpy as np
from jax import lax
from jax.experimental import pallas as pl
from jax.experimental.pallas import tpu as pltpu

F32 = jnp.float32
BF16 = jnp.bfloat16

D_MODEL = 1024
BATCH = 32
SEQ = 256
DEPTH = 2
DEC_BATCH = 4
DEC_SEQ = 1024
PAST_LEN = 512
GRID_W = 64
N_MOD = 9
D_FF = 2816
D_CONV = 512
CONV_WIDTH = 31
D_POOL = 512
POOL_WINDOWS = (2, 4, 8, 16)
POOL_GROUP = 128
N_HEADS = 8
QK_NOPE = 128
QK_ROPE = 64
V_DIM = 128
KV_LORA = 256
Q_LORA = 384
ROPE_AXIS_PAIRS = 16
ROPE_BASE = 10000.0
ALPHA = (2 * DEPTH) ** 0.25
LN_EPS = 1e-5
RMS_EPS = 1e-6

LANES = 128
SUBLANES = 8

N_PROMPT = BATCH * SEQ
N_TOKENS = N_PROMPT + DEC_BATCH * DEC_SEQ
N_GROUPS = 8
HEAD_W = 256
CONV_PAD = 16
CONV_STRIDE = 4
CONV_ROWS = SUBLANES * CONV_STRIDE

VMEM_LIMIT_BYTES = 60 * 1024 * 1024

FFN_TILES = (1024, 512)
FFN_WORK_BYTES = 4 * 1024 * 1024
FFN_SUB = 256
MIX_TILE = 1024
MIX_SEG = 256
SEG_ROWS = MIX_SEG + 2 * CONV_PAD
MLA_TILE = 1024
ATTN_Q_TILE = 256


def _layer_norm(z, g, b):
    mu = jnp.mean(z, axis=-1, keepdims=True)
    zc = z - mu
    var = jnp.mean(zc * zc, axis=-1, keepdims=True)
    return zc * lax.rsqrt(var + LN_EPS) * g + b


def _rms_norm(z, g):
    return z * lax.rsqrt(jnp.mean(z * z, axis=-1, keepdims=True) + RMS_EPS) * g


def _silu(z):
    return z * jax.nn.sigmoid(z)


def _dot(a, b):
    return jnp.dot(a, b, preferred_element_type=F32)


def _dot_nt(a, b):
    return lax.dot_general(a, b, (((1,), (1,)), ((), ())), preferred_element_type=F32)


def _const_spec(shape):
    nd = len(shape)
    return pl.BlockSpec(shape, lambda *_: (0,) * nd, pipeline_mode=pl.Buffered(1))


def _group_of_tile(t, tile):
    n_p = N_PROMPT // tile
    per_seq = DEC_SEQ // tile
    return jnp.where(t < n_p, 0, 1 + (t - n_p) // per_seq)


def _params(n_axes=1):
    return pltpu.CompilerParams(
        dimension_semantics=("arbitrary",) * n_axes,
        vmem_limit_bytes=VMEM_LIMIT_BYTES,
    )


def _ada_body(c_ref, w_ref, b_ref, o_ref):
    s = _silu(c_ref[...]).astype(BF16)
    o_ref[...] = _dot(s, w_ref[...].astype(BF16)) + b_ref[...]


def _ada(c_all, w_ada, b_ada, layer):
    tn = D_MODEL
    return pl.pallas_call(
        _ada_body,
        out_shape=jax.ShapeDtypeStruct((N_GROUPS, N_MOD * D_MODEL), F32),
        grid=(N_MOD,),
        in_specs=[
            pl.BlockSpec((N_GROUPS, D_MODEL), lambda k: (0, 0)),
            pl.BlockSpec((None, D_MODEL, tn), lambda k: (layer, 0, k)),
            pl.BlockSpec((None, 1, tn), lambda k: (layer, 0, k)),
        ],
        out_specs=pl.BlockSpec((N_GROUPS, tn), lambda k: (0, k)),
        compiler_params=_params(),
        name="ada",
    )(c_all, w_ada, b_ada.reshape(DEPTH, 1, N_MOD * D_MODEL))


def _ffn_body(*refs, k0, split_in, split_out, cast_next):
    refs = list(refs)
    n_x = 2 if split_in else 1
    x_refs, refs = refs[:n_x], refs[n_x:]
    (mod_ref, g_ref, b_ref, w1_ref, w3_ref, w2_ref), refs = refs[:6], refs[6:]
    if cast_next:
        next_f32, refs = refs[:3], refs[3:]
    n_o = 2 if split_out else 1
    o_refs, refs = refs[:n_o], refs[n_o:]
    if cast_next:
        next_bf16, refs = refs[:3], refs[3:]
    tile = x_refs[0].shape[0]
    is_prompt = pl.program_id(0) < N_PROMPT // tile
    if cast_next:
        for src, dst in zip(next_f32, next_bf16):
            dst[...] = src[...].astype(BF16)
    shift = mod_ref[k0:k0 + 1, :]
    scale = mod_ref[k0 + 1:k0 + 2, :]
    gate = mod_ref[k0 + 2:k0 + 3, :]
    outs = []
    for s in range(tile // FFN_SUB):
        rows = slice(s * FFN_SUB, (s + 1) * FFN_SUB)
        if split_in:
            x = jnp.where(is_prompt, x_refs[0][rows, :], x_refs[1][rows, :])
        else:
            x = x_refs[0][rows, :]
        h = (x * (1.0 + scale) + shift).astype(BF16)
        a = (_silu(_dot(h, w1_ref[...])) * _dot(h, w3_ref[...])).astype(BF16)
        y = _dot(a, w2_ref[...])
        out = _layer_norm(ALPHA * x + (0.5 * gate) * y, g_ref[...], b_ref[...])
        if split_out:
            outs.append((rows, out))
        else:
            o_refs[0][rows, :] = out
    if split_out:
        @pl.when(is_prompt)
        def _():
            for rows, out in outs:
                o_refs[0][rows, :] = out

        @pl.when(jnp.logical_not(is_prompt))
        def _():
            for rows, out in outs:
                o_refs[1][rows, :] = out


def _slab_rows(n_rows, n_steps):
    packed_rows = 2 * SUBLANES
    n_slabs = max(d for d in range(1, n_steps + 1)
                  if n_rows % d == 0 and (n_rows // d) % packed_rows == 0)
    return n_rows // n_slabs, n_slabs


def _ffn_tile(n_in, split_out, next_w):
    for tm in FFN_TILES:
        n_steps = N_TOKENS // tm
        tile_bytes = tm * D_MODEL * 4
        est = 3 * D_MODEL * D_FF * 2
        est += (n_in + (2 if split_out else 1)) * 2 * tile_bytes
        est += FFN_WORK_BYTES + (2 * tile_bytes if split_out else 0)
        if next_w is not None:
            for wn in next_w:
                rows, _ = _slab_rows(wn.shape[2], n_steps)
                est += 2 * rows * wn.shape[3] * (4 + 2)
        if est <= VMEM_LIMIT_BYTES:
            return tm
    raise ValueError("no FFN tile fits in VMEM")


def _ffn(xs, mod, ln_g, ln_b, w1, w3, w2, next_w=None, next_idx=None, *, k0, split_out=False):
    split_in = isinstance(xs, (tuple, list))
    xs = list(xs) if split_in else [xs]
    tm = _ffn_tile(len(xs), split_out, next_w)
    n_steps = N_TOKENS // tm
    n_p = N_PROMPT // tm
    prompt_spec = pl.BlockSpec((tm, D_MODEL), lambda t: (jnp.minimum(t, n_p - 1), 0))
    latent_spec = pl.BlockSpec((tm, D_MODEL), lambda t: (jnp.maximum(t - n_p, 0), 0))
    merged_spec = pl.BlockSpec((tm, D_MODEL), lambda t: (t, 0))
    in_specs = ([prompt_spec, latent_spec] if split_in else [merged_spec]) + [
        pl.BlockSpec((None, N_MOD, D_MODEL), lambda t: (_group_of_tile(t, tm), 0, 0)),
        _const_spec((1, D_MODEL)),
        _const_spec((1, D_MODEL)),
        _const_spec((D_MODEL, D_FF)),
        _const_spec((D_MODEL, D_FF)),
        _const_spec((D_FF, D_MODEL)),
    ]
    args = xs + [mod, ln_g.reshape(1, D_MODEL), ln_b.reshape(1, D_MODEL), w1, w3, w2]
    if split_out:
        out_shape = [jax.ShapeDtypeStruct((N_PROMPT, D_MODEL), F32),
                     jax.ShapeDtypeStruct((N_TOKENS - N_PROMPT, D_MODEL), F32)]
        out_specs = [prompt_spec, latent_spec]
    else:
        out_shape = [jax.ShapeDtypeStruct((N_TOKENS, D_MODEL), F32)]
        out_specs = [merged_spec]
    if next_w is not None:
        li, lj = next_idx
        for wn in next_w:
            n_rows, n_cols = wn.shape[2:]
            rows, n_slabs = _slab_rows(n_rows, n_steps)
            in_specs.append(pl.BlockSpec(
                (None, None, rows, n_cols),
                lambda t, n_slabs=n_slabs: (li, lj, jnp.minimum(t, n_slabs - 1), 0)))
            args.append(wn)
            out_shape.append(jax.ShapeDtypeStruct((n_rows, n_cols), BF16))
            out_specs.append(pl.BlockSpec(
                (rows, n_cols), lambda t, n_slabs=n_slabs: (jnp.minimum(t, n_slabs - 1), 0)))
    outs = pl.pallas_call(
        functools.partial(_ffn_body, k0=k0, split_in=split_in, split_out=split_out,
                          cast_next=next_w is not None),
        out_shape=out_shape,
        grid=(n_steps,),
        in_specs=in_specs,
        out_specs=out_specs,
        compiler_params=_params(),
        name="ffn",
    )(*args)
    n_tok = 2 if split_out else 1
    tok = tuple(outs[:n_tok]) if split_out else outs[0]
    return tok, tuple(outs[n_tok:])


def _strided_rows(start):
    return pl.ds(start, SUBLANES, stride=CONV_STRIDE)


def _chunk_rows(j):
    chunks_per_seg = MIX_SEG // CONV_ROWS
    r0 = (j % chunks_per_seg) * CONV_ROWS
    return (j // chunks_per_seg) * SEG_ROWS + r0 + CONV_PAD, j * CONV_ROWS, r0


def _conv_chunk(j, *, slab, cw_ref, cb_ref, pad_ref, conv_ref):
    p0, o0, _ = _chunk_rows(j)
    half = CONV_WIDTH // 2
    n_part = 2
    acc = [[None] * n_part for _ in range(CONV_STRIDE)]
    weights = {}
    for off in range(-half, half + CONV_STRIDE):
        tap = pad_ref[slab, _strided_rows(p0 + off), :]
        for rho in range(CONV_STRIDE):
            k = off - rho + half
            if 0 <= k < CONV_WIDTH:
                if k not in weights:
                    weights[k] = cw_ref[slab, pl.ds(k, SUBLANES, stride=0), :]
                term = weights[k] * tap
                part = acc[rho][k % n_part]
                acc[rho][k % n_part] = term if part is None else part + term
    bias = cb_ref[slab, pl.ds(0, SUBLANES, stride=0), :]
    for rho in range(CONV_STRIDE):
        conv_ref[slab, _strided_rows(o0 + rho), :] = (acc[rho][0] + acc[rho][1]) + bias


def _pool_chunk(j, *, pad_ref, pool_ref, is_prompt):
    p0, o0, r0 = _chunk_rows(j)
    at_edge = r0 == 0 or r0 == MIX_SEG - CONV_ROWS
    if at_edge:
        seq_len = jnp.where(is_prompt, SEQ, DEC_SEQ)
        t0 = jnp.where(is_prompt, r0, o0) + CONV_STRIDE * lax.broadcasted_iota(jnp.int32, (SUBLANES, LANES), 0)
    for gi, w in enumerate(POOL_WINDOWS):
        left = w // 2
        right = w - 1 - left
        slab = D_CONV // LANES + gi
        total = [None] * CONV_STRIDE
        for off in range(-left, right + CONV_STRIDE):
            tap = pad_ref[slab, _strided_rows(p0 + off), :]
            for rho in range(CONV_STRIDE):
                if -left <= off - rho <= right:
                    total[rho] = tap if total[rho] is None else total[rho] + tap
        for rho in range(CONV_STRIDE):
            if at_edge:
                t = t0 + rho
                cnt = (jnp.minimum(t + right, seq_len - 1) - jnp.maximum(t - left, 0) + 1).astype(F32)
                mean = total[rho] / cnt
            else:
                mean = total[rho] / float(w)
            tok = pad_ref[slab, _strided_rows(p0 + rho), :]
            pool_ref[gi, _strided_rows(o0 + rho), :] = mean - tok


def _cp_body(x_ref, mod_ref, g_ref, b_ref, win_ref, cw_ref, cb_ref, cng_ref, cnb_ref,
             pw_ref, ps_ref, wout_ref, c_ref, wada_ref, bada_ref, o_ref, ada_ref,
             pad_ref, conv_ref, pool_ref):
    is_prompt = pl.program_id(0) < N_PROMPT // MIX_TILE
    _ada_body(c_ref, wada_ref, bada_ref, ada_ref)
    shift = mod_ref[3:4, :]
    scale = mod_ref[4:5, :]
    gate = mod_ref[5:6, :]
    n_seg = MIX_TILE // MIX_SEG
    n_slab = (D_CONV + D_POOL) // LANES
    feats = []
    for s in range(n_seg):
        x = x_ref[s * MIX_SEG:(s + 1) * MIX_SEG, :]
        h = (x * (1.0 + scale) + shift).astype(BF16)
        proj = _dot(h, win_ref[...])
        glu = proj[:, :D_CONV] * jax.nn.sigmoid(proj[:, D_CONV:2 * D_CONV])
        feats.append(jnp.concatenate([glu, proj[:, 2 * D_CONV:]], axis=-1))
    zeros = jnp.zeros((CONV_PAD, LANES), F32)
    for s in range(n_seg):
        base = s * SEG_ROWS
        for slab in range(n_slab):
            lanes = slice(slab * LANES, (slab + 1) * LANES)
            pad_ref[slab, base:base + CONV_PAD, :] = (
                zeros if s == 0 else jnp.where(is_prompt, zeros, feats[s - 1][MIX_SEG - CONV_PAD:, lanes]))
            pad_ref[slab, base + CONV_PAD:base + CONV_PAD + MIX_SEG, :] = feats[s][:, lanes]
            pad_ref[slab, base + CONV_PAD + MIX_SEG:base + SEG_ROWS, :] = (
                zeros if s == n_seg - 1 else jnp.where(is_prompt, zeros, feats[s + 1][:CONV_PAD, lanes]))
    chunks_per_seg = MIX_SEG // CONV_ROWS
    for s in range(n_seg):
        rows = slice(s * MIX_SEG, (s + 1) * MIX_SEG)
        seg_chunks = range(s * chunks_per_seg, (s + 1) * chunks_per_seg)
        for slab in range(D_CONV // LANES):
            for j in seg_chunks:
                _conv_chunk(j, slab=slab, cw_ref=cw_ref, cb_ref=cb_ref, pad_ref=pad_ref, conv_ref=conv_ref)
        for j in seg_chunks:
            _pool_chunk(j, pad_ref=pad_ref, pool_ref=pool_ref, is_prompt=is_prompt)
        conv = jnp.concatenate([conv_ref[c, rows, :] for c in range(D_CONV // LANES)], axis=-1)
        a = _silu(_layer_norm(conv, cng_ref[...], cnb_ref[...]))
        pooled = [_dot(pool_ref[gi, rows, :].astype(BF16), pw_ref[gi]) for gi in range(len(POOL_WINDOWS))]
        bmix = jnp.concatenate(pooled, axis=-1) * ps_ref[...]
        cat = jnp.concatenate([a, bmix], axis=-1).astype(BF16)
        y = _dot(cat, wout_ref[...])
        o_ref[rows, :] = _layer_norm(ALPHA * x_ref[rows, :] + gate * y, g_ref[...], b_ref[...])


def _conv_pool_mixer(x, mod, ln_g, ln_b, w_in, conv_w, conv_b, cn_g, cn_b, pool_w, pool_scale, w_out,
                     c_all, w_ada, b_ada, next_layer):
    tm = MIX_TILE
    n_steps = N_TOKENS // tm
    ada_cols = N_MOD * D_MODEL // n_steps
    assert tm == DEC_SEQ and MIX_SEG == SEQ and N_PROMPT % tm == 0 and ada_cols % LANES == 0
    n_cslab = D_CONV // LANES
    row_spec = pl.BlockSpec((tm, D_MODEL), lambda t: (t, 0))
    in_specs = [
        row_spec,
        pl.BlockSpec((None, N_MOD, D_MODEL), lambda t: (_group_of_tile(t, tm), 0, 0)),
        _const_spec((1, D_MODEL)),
        _const_spec((1, D_MODEL)),
        _const_spec((D_MODEL, 2 * D_CONV + D_POOL)),
        _const_spec((n_cslab, CONV_WIDTH, LANES)),
        _const_spec((n_cslab, 1, LANES)),
        _const_spec((1, D_CONV)),
        _const_spec((1, D_CONV)),
        _const_spec((len(POOL_WINDOWS), POOL_GROUP, POOL_GROUP)),
        _const_spec((1, D_POOL)),
        _const_spec((D_CONV + D_POOL, D_MODEL)),
        _const_spec((N_GROUPS, D_MODEL)),
        pl.BlockSpec((None, D_MODEL, ada_cols), lambda t: (next_layer, 0, t)),
        pl.BlockSpec((None, 1, ada_cols), lambda t: (next_layer, 0, t)),
    ]
    conv_w_slabs = conv_w.reshape(CONV_WIDTH, n_cslab, LANES).transpose(1, 0, 2)
    args = [x, mod, ln_g.reshape(1, D_MODEL), ln_b.reshape(1, D_MODEL), w_in, conv_w_slabs,
            conv_b.reshape(n_cslab, 1, LANES), cn_g.reshape(1, D_CONV), cn_b.reshape(1, D_CONV),
            pool_w, pool_scale.reshape(1, D_POOL), w_out,
            c_all, w_ada, b_ada.reshape(DEPTH, 1, N_MOD * D_MODEL)]
    return pl.pallas_call(
        _cp_body,
        out_shape=[jax.ShapeDtypeStruct((N_TOKENS, D_MODEL), F32),
                   jax.ShapeDtypeStruct((N_GROUPS, N_MOD * D_MODEL), F32)],
        grid=(n_steps,),
        in_specs=in_specs,
        out_specs=[row_spec, pl.BlockSpec((N_GROUPS, ada_cols), lambda t: (0, t))],
        scratch_shapes=[
            pltpu.VMEM(((D_CONV + D_POOL) // LANES, tm // MIX_SEG * SEG_ROWS, LANES), F32),
            pltpu.VMEM((n_cslab, tm, LANES), F32),
            pltpu.VMEM((D_POOL // LANES, tm, LANES), F32),
        ],
        compiler_params=_params(),
        name="conv_pool",
    )(*args)


def _attend(q, keys, values):
    scores = [_dot_nt(q, k) for k in keys]
    m = functools.reduce(jnp.maximum, [jnp.max(s, axis=-1, keepdims=True) for s in scores])
    c = (QK_NOPE + QK_ROPE) ** -0.5 * math.log2(math.e)
    e = [jnp.exp2((s - m) * c) for s in scores]
    denom = functools.reduce(jnp.add, [jnp.sum(ei, axis=-1, keepdims=True) for ei in e])
    out = functools.reduce(jnp.add, [_dot(ei.astype(BF16), v) for ei, v in zip(e, values)])
    return out * (1.0 / denom)


def _mla_project(x, mod_ref, wdkv_ref, kvg_ref, wdq_ref, qg_ref, wuq_ref):
    h = (x * (1.0 + mod_ref[4:5, :]) + mod_ref[3:4, :]).astype(BF16)
    kv = _dot(h, wdkv_ref[...])
    ckv = _rms_norm(kv[:, :KV_LORA], kvg_ref[...])
    qd = _rms_norm(_dot(h, wdq_ref[...]), qg_ref[...]).astype(BF16)
    return kv, ckv, qd, _dot(qd, wuq_ref[...])


def _mla_out(x, attn, mod_ref, g_ref, b_ref, wo_ref):
    y = _dot(attn, wo_ref[...])
    return _layer_norm(ALPHA * x + mod_ref[5:6, :] * y, g_ref[...], b_ref[...])


def _mla_prompt_body(x_ref, mod_ref, g_ref, b_ref, wdkv_ref, kvg_ref, wdq_ref, qg_ref, wuq_ref,
                     wuk_ref, wuv_ref, wo_ref, o_ref, ckv_ref, kr_ref):
    for s in range(x_ref.shape[0] // SEQ):
        rows = slice(s * SEQ, (s + 1) * SEQ)
        x = x_ref[rows, :]
        kv, ckv, _, q = _mla_project(x, mod_ref, wdkv_ref, kvg_ref, wdq_ref, qg_ref, wuq_ref)
        ckv_ref[rows, :] = ckv
        kr_ref[rows, :] = kv[:, KV_LORA:KV_LORA + QK_ROPE]
        q16 = q.astype(BF16)
        ckv16 = ckv.astype(BF16)
        kn16 = _dot(ckv16, wuk_ref[...]).astype(BF16)
        v16 = _dot(ckv16, wuv_ref[...]).astype(BF16)
        kr16 = kv[:, KV_LORA:KV_LORA + LANES].astype(BF16)
        heads = []
        for hd in range(N_HEADS):
            vl = slice(hd * V_DIM, (hd + 1) * V_DIM)
            k_h = jnp.concatenate([kn16[:, hd * QK_NOPE:(hd + 1) * QK_NOPE], kr16], axis=-1)
            heads.append(_attend(q16[:, hd * HEAD_W:(hd + 1) * HEAD_W], [k_h], [v16[:, vl]]).astype(BF16))
        attn = jnp.concatenate(heads, axis=-1)
        o_ref[rows, :] = _mla_out(x, attn, mod_ref, g_ref, b_ref, wo_ref)


def _store_keys(k_ref, rows, kn16, kr16):
    for hd in range(N_HEADS):
        k_ref[rows, hd * HEAD_W:hd * HEAD_W + QK_NOPE] = kn16[:, hd * QK_NOPE:(hd + 1) * QK_NOPE]
        k_ref[rows, hd * HEAD_W + QK_NOPE:(hd + 1) * HEAD_W] = kr16


def _mla_latent_body(x_ref, mod_ref, g_ref, b_ref, wdkv_ref, kvg_ref, wdq_ref, qg_ref, wuq_ref,
                     wuk_ref, wuv_ref, wo_ref, wuqs_ref, cos_ref, sin_ref, cckv_ref, ckr_ref,
                     o_ref, q_s, k_s, v_s, kc_s, vc_s):
    c16 = cckv_ref[...].astype(BF16)
    vc_s[...] = _dot(c16, wuv_ref[...]).astype(BF16)
    _store_keys(kc_s, slice(None), _dot(c16, wuk_ref[...]).astype(BF16), ckr_ref[...].astype(BF16))
    for s in range(DEC_SEQ // ATTN_Q_TILE):
        rows = slice(s * ATTN_Q_TILE, (s + 1) * ATTN_Q_TILE)
        cos = cos_ref[rows, :]
        sin = sin_ref[rows, :]
        kv, ckv, qd, q = _mla_project(x_ref[rows, :], mod_ref, wdkv_ref, kvg_ref, wdq_ref, qg_ref, wuq_ref)
        q_swapped = _dot(qd, wuqs_ref[...])
        kr = kv[:, KV_LORA:KV_LORA + LANES] * cos + kv[:, KV_LORA + LANES:KV_LORA + 2 * LANES] * sin
        ckv16 = ckv.astype(BF16)
        v_s[rows, :] = _dot(ckv16, wuv_ref[...]).astype(BF16)
        _store_keys(k_s, rows, _dot(ckv16, wuk_ref[...]).astype(BF16), kr.astype(BF16))
        for hd in range(N_HEADS):
            lo = hd * HEAD_W
            q_s[rows, lo:lo + QK_NOPE] = q[:, lo:lo + QK_NOPE].astype(BF16)
            q_rope = q[:, lo + QK_NOPE:lo + HEAD_W] * cos + q_swapped[:, hd * LANES:(hd + 1) * LANES] * sin
            q_s[rows, lo + QK_NOPE:lo + HEAD_W] = q_rope.astype(BF16)

    def q_block(i, carry):
        rows = pl.ds(pl.multiple_of(i * ATTN_Q_TILE, ATTN_Q_TILE), ATTN_Q_TILE)
        heads = []
        for hd in range(N_HEADS):
            ql = slice(hd * HEAD_W, (hd + 1) * HEAD_W)
            vl = slice(hd * V_DIM, (hd + 1) * V_DIM)
            heads.append(_attend(q_s[rows, ql], [k_s[:, ql], kc_s[:, ql]],
                                 [v_s[:, vl], vc_s[:, vl]]).astype(BF16))
        attn = jnp.concatenate(heads, axis=-1)
        o_ref[rows, :] = _mla_out(x_ref[rows, :], attn, mod_ref, g_ref, b_ref, wo_ref)
        return carry

    lax.fori_loop(0, DEC_SEQ // ATTN_Q_TILE, q_block, 0)


def _mla_mixer(x, mod, ln_g, ln_b, w, cache_ckv, cache_kr, *, latent):
    tile = MLA_TILE
    first = N_PROMPT // tile if latent else 0
    n_rows = DEC_BATCH * DEC_SEQ if latent else N_PROMPT
    dkv = w["dkv"] if latent else w["dkv"][:, :KV_LORA + LANES]
    in_specs = [
        pl.BlockSpec((tile, D_MODEL), lambda t: (first + t, 0)),
        pl.BlockSpec((None, N_MOD, D_MODEL), lambda t: (_group_of_tile(first + t, tile), 0, 0)),
        _const_spec((1, D_MODEL)),
        _const_spec((1, D_MODEL)),
        _const_spec(dkv.shape),
        _const_spec((1, KV_LORA)),
        _const_spec((D_MODEL, Q_LORA)),
        _const_spec((1, Q_LORA)),
        _const_spec((Q_LORA, N_HEADS * HEAD_W)),
        _const_spec((KV_LORA, N_HEADS * QK_NOPE)),
        _const_spec((KV_LORA, N_HEADS * V_DIM)),
        _const_spec((N_HEADS * V_DIM, D_MODEL)),
    ]
    args = [x, mod, ln_g.reshape(1, D_MODEL), ln_b.reshape(1, D_MODEL), dkv, w["kv_g"], w["dq"],
            w["q_g"], w["uq"], w["uk"], w["uv"], w["o"]]
    out_shape = [jax.ShapeDtypeStruct((n_rows, D_MODEL), F32)]
    out_specs = [pl.BlockSpec((tile, D_MODEL), lambda t: (t, 0))]
    scratch = []
    if latent:
        assert tile == DEC_SEQ
        in_specs += [
            _const_spec((Q_LORA, N_HEADS * LANES)),
            _const_spec((DEC_SEQ, LANES)),
            _const_spec((DEC_SEQ, LANES)),
            pl.BlockSpec((None, PAST_LEN, KV_LORA), lambda t: (t, 0, 0)),
            pl.BlockSpec((None, PAST_LEN, LANES), lambda t: (t, 0, 0)),
        ]
        args += [w["uq_swapped"], w["cos"], w["sin"], cache_ckv, cache_kr]
        scratch = [
            pltpu.VMEM((DEC_SEQ, N_HEADS * HEAD_W), BF16),
            pltpu.VMEM((DEC_SEQ, N_HEADS * HEAD_W), BF16),
            pltpu.VMEM((DEC_SEQ, N_HEADS * V_DIM), BF16),
            pltpu.VMEM((PAST_LEN, N_HEADS * HEAD_W), BF16),
            pltpu.VMEM((PAST_LEN, N_HEADS * V_DIM), BF16),
        ]
    else:
        out_shape += [jax.ShapeDtypeStruct((n_rows, KV_LORA), F32),
                      jax.ShapeDtypeStruct((n_rows, QK_ROPE), F32)]
        out_specs += [pl.BlockSpec((tile, KV_LORA), lambda t: (t, 0)),
                      pl.BlockSpec((tile, QK_ROPE), lambda t: (t, 0))]
    outs = pl.pallas_call(
        _mla_latent_body if latent else _mla_prompt_body,
        out_shape=out_shape,
        grid=(n_rows // tile,),
        in_specs=in_specs,
        out_specs=out_specs,
        scratch_shapes=scratch,
        compiler_params=_params(),
        name="mla_latent" if latent else "mla_prompt",
    )(*args)
    return outs[0] if latent else outs


def _rope_swap_perm():
    idx = np.arange(QK_ROPE)
    return np.where((idx % 32) < ROPE_AXIS_PAIRS, idx + ROPE_AXIS_PAIRS, idx - ROPE_AXIS_PAIRS)


def _rope_tables():
    n = DEC_SEQ
    row = np.repeat(np.arange(n // GRID_W), GRID_W)
    col = np.tile(np.arange(GRID_W), n // GRID_W)
    inv = ROPE_BASE ** (-np.arange(ROPE_AXIS_PAIRS, dtype=np.float64) / ROPE_AXIS_PAIRS)
    ar = row[:, None] * inv
    ac = col[:, None] * inv
    zeros = np.zeros((n, LANES - QK_ROPE))
    cos = np.concatenate([np.cos(ar), np.cos(ar), np.cos(ac), np.cos(ac), zeros], axis=-1)
    sin = np.concatenate([-np.sin(ar), np.sin(ar), -np.sin(ac), np.sin(ac), zeros], axis=-1)
    return jnp.asarray(cos, F32), jnp.asarray(sin, F32)


def _mla_weights(w_dq, q_norm_g, w_uq, w_dkv, kv_norm_g, w_ukv, w_o):
    swap = _rope_swap_perm()
    z64 = jnp.zeros((D_MODEL, LANES - QK_ROPE), F32)
    kr_cols = w_dkv[:, KV_LORA:]
    dkv = jnp.concatenate([w_dkv[:, :KV_LORA], kr_cols, z64, kr_cols[:, swap], z64], axis=-1)
    uq = w_uq.reshape(Q_LORA, N_HEADS, QK_NOPE + QK_ROPE)
    zq = jnp.zeros((Q_LORA, N_HEADS, LANES - QK_ROPE), F32)
    uq_main = jnp.concatenate([uq, zq], axis=-1).reshape(Q_LORA, N_HEADS * HEAD_W)
    uq_swapped = jnp.concatenate([uq[:, :, QK_NOPE:][:, :, swap], zq], axis=-1).reshape(Q_LORA, N_HEADS * LANES)
    ukv = w_ukv.reshape(KV_LORA, N_HEADS, QK_NOPE + V_DIM)
    cos, sin = _rope_tables()
    return {
        "dkv": dkv.astype(BF16),
        "kv_g": kv_norm_g.reshape(1, KV_LORA),
        "dq": w_dq.astype(BF16),
        "q_g": q_norm_g.reshape(1, Q_LORA),
        "uq": uq_main.astype(BF16),
        "uq_swapped": uq_swapped.astype(BF16),
        "uk": ukv[:, :, :QK_NOPE].reshape(KV_LORA, N_HEADS * QK_NOPE).astype(BF16),
        "uv": ukv[:, :, QK_NOPE:].reshape(KV_LORA, N_HEADS * V_DIM).astype(BF16),
        "o": w_o.astype(BF16),
        "cos": cos,
        "sin": sin,
    }


def kernel(x_prompt, x_sample, cache_mla_ckv, cache_mla_krope, c, c_ctx, w_ada, b_ada, ln_g, ln_b, ffn_w1, ffn_w3, ffn_w2, cp_w_in, conv_w, conv_b, conv_norm_g, conv_norm_b, pool_w, pool_scale, cp_w_out, mla_w_dq, mla_q_norm_g, mla_w_uq, mla_w_dkv, mla_kv_norm_g, mla_w_ukv, mla_w_o):
    c_all = jnp.concatenate(
        [c_ctx[None, :], c, jnp.zeros((N_GROUPS - 1 - DEC_BATCH, D_MODEL), F32)], axis=0)
    ada = {0: _ada(c_all, w_ada, b_ada, 0)}

    x = (x_prompt.reshape(N_PROMPT, D_MODEL), x_sample.reshape(DEC_BATCH * DEC_SEQ, D_MODEL))
    ffn_f32 = (ffn_w1, ffn_w3, ffn_w2)
    w_bf16 = tuple(wf[0, 0].astype(BF16) for wf in ffn_f32)
    new_ckv, new_krope = [], []
    for i in range(DEPTH):
        if i not in ada:
            ada[i] = _ada(c_all, w_ada, b_ada, i)
        mod = ada[i].reshape(N_GROUPS, N_MOD, D_MODEL)
        j = i // 2
        x, w_bf16 = _ffn(x, mod, ln_g[i, 0], ln_b[i, 0], *w_bf16, ffn_f32, (i, 1), k0=0)
        if i % 2 == 0:
            x, ada_next = _conv_pool_mixer(
                x, mod, ln_g[i, 1], ln_b[i, 1], cp_w_in[j].astype(BF16), conv_w[j], conv_b[j],
                conv_norm_g[j], conv_norm_b[j], pool_w[j].astype(BF16), pool_scale[j],
                cp_w_out[j].astype(BF16), c_all, w_ada, b_ada, min(i + 1, DEPTH - 1))
            if i + 1 < DEPTH:
                ada[i + 1] = ada_next
        else:
            w = _mla_weights(mla_w_dq[j], mla_q_norm_g[j], mla_w_uq[j], mla_w_dkv[j],
                             mla_kv_norm_g[j], mla_w_ukv[j], mla_w_o[j])
            attn_args = (x, mod, ln_g[i, 1], ln_b[i, 1], w)
            y_p, ckv_p, kr_p = _mla_mixer(*attn_args, None, None, latent=False)
            new_ckv.append(ckv_p.reshape(BATCH, SEQ, KV_LORA))
            new_krope.append(kr_p.reshape(BATCH, SEQ, QK_ROPE))
            kr_cache = jnp.pad(cache_mla_krope[:, j], ((0, 0), (0, 0), (0, LANES - QK_ROPE)))
            x = (y_p, _mla_mixer(*attn_args, cache_mla_ckv[:, j], kr_cache, latent=True))
        last = i == DEPTH - 1
        x, w_bf16 = _ffn(x, mod, ln_g[i, 2], ln_b[i, 2], *w_bf16,
                         None if last else ffn_f32, None if last else (i + 1, 0),
                         k0=6, split_out=last)

    y_prompt = x[0].reshape(BATCH, SEQ, D_MODEL)
    y_sample = x[1].reshape(DEC_BATCH, DEC_SEQ, D_MODEL)
    return (y_prompt, y_sample, jnp.stack(new_ckv, axis=1), jnp.stack(new_krope, axis=1))
```

```python
import functools
import math

import jax
import jax.numpy as jnp
import numpy as np
from jax import lax
from jax.experimental import pallas as pl
from jax.experimental.pallas import tpu as pltpu

F32 = jnp.float32
BF16 = jnp.bfloat16

D_MODEL = 1024
BATCH = 32
SEQ = 256
DEPTH = 2
DEC_BATCH = 4
DEC_SEQ = 1024
PAST_LEN = 512
GRID_W = 64
N_MOD = 9
D_FF = 2816
D_CONV = 512
CONV_WIDTH = 31
D_POOL = 512
POOL_WINDOWS = (2, 4, 8, 16)
POOL_GROUP = 128
N_HEADS = 8
QK_NOPE = 128
QK_ROPE = 64
V_DIM = 128
KV_LORA = 256
Q_LORA = 384
ROPE_AXIS_PAIRS = 16
ROPE_BASE = 10000.0
ALPHA = (2 * DEPTH) ** 0.25
LN_EPS = 1e-5
RMS_EPS = 1e-6

LANES = 128
SUBLANES = 8

N_PROMPT = BATCH * SEQ
N_TOKENS = N_PROMPT + DEC_BATCH * DEC_SEQ
N_GROUPS = 8
HEAD_W = 256
CONV_PAD = 16
CONV_STRIDE = 4
CONV_ROWS = SUBLANES * CONV_STRIDE

VMEM_LIMIT_BYTES = 60 * 1024 * 1024

FFN_TILES = (1024, 512)
FFN_WORK_BYTES = 4 * 1024 * 1024
FFN_SUB = 256
MIX_TILE = 1024
MIX_SEG = 256
SEG_ROWS = MIX_SEG + 2 * CONV_PAD
MLA_TILE = 1024
ATTN_Q_TILE = 256


def _layer_norm(z, g, b):
    mu = jnp.mean(z, axis=-1, keepdims=True)
    zc = z - mu
    var = jnp.mean(zc * zc, axis=-1, keepdims=True)
    return zc * lax.rsqrt(var + LN_EPS) * g + b


def _rms_norm(z, g):
    return z * lax.rsqrt(jnp.mean(z * z, axis=-1, keepdims=True) + RMS_EPS) * g


def _silu(z):
    return z * jax.nn.sigmoid(z)


def _dot(a, b):
    return jnp.dot(a, b, preferred_element_type=F32)


def _dot_nt(a, b):
    return lax.dot_general(a, b, (((1,), (1,)), ((), ())), preferred_element_type=F32)


def _const_spec(shape):
    nd = len(shape)
    return pl.BlockSpec(shape, lambda *_: (0,) * nd, pipeline_mode=pl.Buffered(1))


def _group_of_tile(t, tile):
    n_p = N_PROMPT // tile
    per_seq = DEC_SEQ // tile
    return jnp.where(t < n_p, 0, 1 + (t - n_p) // per_seq)


def _params(n_axes=1):
    return pltpu.CompilerParams(
        dimension_semantics=("arbitrary",) * n_axes,
        vmem_limit_bytes=VMEM_LIMIT_BYTES,
    )


def _ada_body(c_ref, w_ref, b_ref, o_ref):
    s = _silu(c_ref[...]).astype(BF16)
    o_ref[...] = _dot(s, w_ref[...].astype(BF16)) + b_ref[...]


def _slab_rows(n_rows, n_steps):
    packed_rows = 2 * SUBLANES
    n_slabs = max(d for d in range(1, n_steps + 1)
                  if n_rows % d == 0 and (n_rows // d) % packed_rows == 0)
    return n_rows // n_slabs, n_slabs


def _cast_side_job(stacks, idx, n_steps):
    li, lj = idx
    in_specs, out_shapes, out_specs = [], [], []
    for wn in stacks:
        n_rows, n_cols = wn.shape[2:]
        rows, n_slabs = _slab_rows(n_rows, n_steps)
        in_specs.append(pl.BlockSpec(
            (None, None, rows, n_cols),
            lambda t, n_slabs=n_slabs: (li, lj, jnp.minimum(t, n_slabs - 1), 0)))
        out_shapes.append(jax.ShapeDtypeStruct((n_rows, n_cols), BF16))
        out_specs.append(pl.BlockSpec(
            (rows, n_cols), lambda t, n_slabs=n_slabs: (jnp.minimum(t, n_slabs - 1), 0)))
    return in_specs, list(stacks), out_shapes, out_specs


def _cast_slabs(srcs, dsts):
    for src, dst in zip(srcs, dsts):
        dst[...] = src[...].astype(BF16)


def _ada_cast_body(c_ref, w_ref, b_ref, *refs):
    n = (len(refs) - 1) // 2
    _cast_slabs(refs[:n], refs[n + 1:])
    _ada_body(c_ref, w_ref, b_ref, refs[n])


def _ada(c_all, w_ada, b_ada, layer, cast_w, cast_idx):
    tn = D_MODEL
    c_specs, c_args, c_shapes, c_out_specs = _cast_side_job(cast_w, cast_idx, N_MOD)
    outs = pl.pallas_call(
        _ada_cast_body,
        out_shape=[jax.ShapeDtypeStruct((N_GROUPS, N_MOD * D_MODEL), F32)] + c_shapes,
        grid=(N_MOD,),
        in_specs=[
            pl.BlockSpec((N_GROUPS, D_MODEL), lambda k: (0, 0)),
            pl.BlockSpec((None, D_MODEL, tn), lambda k: (layer, 0, k)),
            pl.BlockSpec((None, 1, tn), lambda k: (layer, 0, k)),
        ] + c_specs,
        out_specs=[pl.BlockSpec((N_GROUPS, tn), lambda k: (0, k))] + c_out_specs,
        compiler_params=_params(),
        name="ada",
    )(c_all, w_ada, b_ada.reshape(DEPTH, 1, N_MOD * D_MODEL), *c_args)
    return outs[0], tuple(outs[1:])


def _ffn_body(*refs, k0, split_in, split_out, cast_next):
    refs = list(refs)
    n_x = 2 if split_in else 1
    x_refs, refs = refs[:n_x], refs[n_x:]
    (mod_ref, g_ref, b_ref, w1_ref, w3_ref, w2_ref), refs = refs[:6], refs[6:]
    if cast_next:
        next_f32, refs = refs[:3], refs[3:]
    n_o = 2 if split_out else 1
    o_refs, refs = refs[:n_o], refs[n_o:]
    if cast_next:
        next_bf16, refs = refs[:3], refs[3:]
    tile = x_refs[0].shape[0]
    is_prompt = pl.program_id(0) < N_PROMPT // tile
    if cast_next:
        _cast_slabs(next_f32, next_bf16)
    shift = mod_ref[k0:k0 + 1, :]
    scale = mod_ref[k0 + 1:k0 + 2, :]
    gate = mod_ref[k0 + 2:k0 + 3, :]
    outs = []
    for s in range(tile // FFN_SUB):
        rows = slice(s * FFN_SUB, (s + 1) * FFN_SUB)
        if split_in:
            x = jnp.where(is_prompt, x_refs[0][rows, :], x_refs[1][rows, :])
        else:
            x = x_refs[0][rows, :]
        h = (x * (1.0 + scale) + shift).astype(BF16)
        a = (_silu(_dot(h, w1_ref[...])) * _dot(h, w3_ref[...])).astype(BF16)
        y = _dot(a, w2_ref[...])
        out = _layer_norm(ALPHA * x + (0.5 * gate) * y, g_ref[...], b_ref[...])
        if split_out:
            outs.append((rows, out))
        else:
            o_refs[0][rows, :] = out
    if split_out:
        @pl.when(is_prompt)
        def _():
            for rows, out in outs:
                o_refs[0][rows, :] = out

        @pl.when(jnp.logical_not(is_prompt))
        def _():
            for rows, out in outs:
                o_refs[1][rows, :] = out


def _ffn_tile(n_in, split_out, next_w):
    for tm in FFN_TILES:
        n_steps = N_TOKENS // tm
        tile_bytes = tm * D_MODEL * 4
        est = 3 * D_MODEL * D_FF * 2
        est += (n_in + (2 if split_out else 1)) * 2 * tile_bytes
        est += FFN_WORK_BYTES + (2 * tile_bytes if split_out else 0)
        if next_w is not None:
            for wn in next_w:
                rows, _ = _slab_rows(wn.shape[2], n_steps)
                est += 2 * rows * wn.shape[3] * (4 + 2)
        if est <= VMEM_LIMIT_BYTES:
            return tm
    raise ValueError("no FFN tile fits in VMEM")


def _ffn(xs, mod, ln_g, ln_b, w1, w3, w2, next_w=None, next_idx=None, *, k0, split_out=False):
    split_in = isinstance(xs, (tuple, list))
    xs = list(xs) if split_in else [xs]
    tm = _ffn_tile(len(xs), split_out, next_w)
    n_steps = N_TOKENS // tm
    n_p = N_PROMPT // tm
    prompt_spec = pl.BlockSpec((tm, D_MODEL), lambda t: (jnp.minimum(t, n_p - 1), 0))
    latent_spec = pl.BlockSpec((tm, D_MODEL), lambda t: (jnp.maximum(t - n_p, 0), 0))
    merged_spec = pl.BlockSpec((tm, D_MODEL), lambda t: (t, 0))
    in_specs = ([prompt_spec, latent_spec] if split_in else [merged_spec]) + [
        pl.BlockSpec((None, N_MOD, D_MODEL), lambda t: (_group_of_tile(t, tm), 0, 0)),
        _const_spec((1, D_MODEL)),
        _const_spec((1, D_MODEL)),
        _const_spec((D_MODEL, D_FF)),
        _const_spec((D_MODEL, D_FF)),
        _const_spec((D_FF, D_MODEL)),
    ]
    args = xs + [mod, ln_g.reshape(1, D_MODEL), ln_b.reshape(1, D_MODEL), w1, w3, w2]
    if split_out:
        out_shape = [jax.ShapeDtypeStruct((N_PROMPT, D_MODEL), F32),
                     jax.ShapeDtypeStruct((N_TOKENS - N_PROMPT, D_MODEL), F32)]
        out_specs = [prompt_spec, latent_spec]
    else:
        out_shape = [jax.ShapeDtypeStruct((N_TOKENS, D_MODEL), F32)]
        out_specs = [merged_spec]
    if next_w is not None:
        c_specs, c_args, c_shapes, c_out_specs = _cast_side_job(next_w, next_idx, n_steps)
        in_specs += c_specs
        args += c_args
        out_shape += c_shapes
        out_specs += c_out_specs
    outs = pl.pallas_call(
        functools.partial(_ffn_body, k0=k0, split_in=split_in, split_out=split_out,
                          cast_next=next_w is not None),
        out_shape=out_shape,
        grid=(n_steps,),
        in_specs=in_specs,
        out_specs=out_specs,
        compiler_params=_params(),
        name="ffn",
    )(*args)
    n_tok = 2 if split_out else 1
    tok = tuple(outs[:n_tok]) if split_out else outs[0]
    return tok, tuple(outs[n_tok:])


def _strided_rows(start):
    return pl.ds(start, SUBLANES, stride=CONV_STRIDE)


def _chunk_rows(j):
    chunks_per_seg = MIX_SEG // CONV_ROWS
    r0 = (j % chunks_per_seg) * CONV_ROWS
    return (j // chunks_per_seg) * SEG_ROWS + r0 + CONV_PAD, j * CONV_ROWS, r0


def _conv_chunk(j, *, slab, cw_ref, cb_ref, pad_ref, conv_ref):
    p0, o0, _ = _chunk_rows(j)
    half = CONV_WIDTH // 2
    n_part = 2
    acc = [[None] * n_part for _ in range(CONV_STRIDE)]
    weights = {}
    for off in range(-half, half + CONV_STRIDE):
        tap = pad_ref[slab, _strided_rows(p0 + off), :]
        for rho in range(CONV_STRIDE):
            k = off - rho + half
            if 0 <= k < CONV_WIDTH:
                if k not in weights:
                    weights[k] = cw_ref[slab, pl.ds(k, SUBLANES, stride=0), :]
                term = weights[k] * tap
                part = acc[rho][k % n_part]
                acc[rho][k % n_part] = term if part is None else part + term
    bias = cb_ref[slab, pl.ds(0, SUBLANES, stride=0), :]
    for rho in range(CONV_STRIDE):
        conv_ref[slab, _strided_rows(o0 + rho), :] = (acc[rho][0] + acc[rho][1]) + bias


def _pool_chunk(j, *, pad_ref, pool_ref, is_prompt):
    p0, o0, r0 = _chunk_rows(j)
    at_edge = r0 == 0 or r0 == MIX_SEG - CONV_ROWS
    if at_edge:
        seq_len = jnp.where(is_prompt, SEQ, DEC_SEQ)
        t0 = jnp.where(is_prompt, r0, o0) + CONV_STRIDE * lax.broadcasted_iota(jnp.int32, (SUBLANES, LANES), 0)
    for gi, w in enumerate(POOL_WINDOWS):
        left = w // 2
        right = w - 1 - left
        slab = D_CONV // LANES + gi
        total = [None] * CONV_STRIDE
        for off in range(-left, right + CONV_STRIDE):
            tap = pad_ref[slab, _strided_rows(p0 + off), :]
            for rho in range(CONV_STRIDE):
                if -left <= off - rho <= right:
                    total[rho] = tap if total[rho] is None else total[rho] + tap
        for rho in range(CONV_STRIDE):
            if at_edge:
                t = t0 + rho
                cnt = (jnp.minimum(t + right, seq_len - 1) - jnp.maximum(t - left, 0) + 1).astype(F32)
                mean = total[rho] / cnt
            else:
                mean = total[rho] / float(w)
            tok = pad_ref[slab, _strided_rows(p0 + rho), :]
            pool_ref[gi, _strided_rows(o0 + rho), :] = mean - tok


def _cp_body(x_ref, mod_ref, g_ref, b_ref, win_ref, cw_ref, cb_ref, cng_ref, cnb_ref,
             pw_ref, ps_ref, wout_ref, c_ref, wada_ref, bada_ref, o_ref, ada_ref,
             pad_ref, conv_ref, pool_ref):
    is_prompt = pl.program_id(0) < N_PROMPT // MIX_TILE
    _ada_body(c_ref, wada_ref, bada_ref, ada_ref)
    shift = mod_ref[3:4, :]
    scale = mod_ref[4:5, :]
    gate = mod_ref[5:6, :]
    n_seg = MIX_TILE // MIX_SEG
    n_slab = (D_CONV + D_POOL) // LANES
    feats = []
    for s in range(n_seg):
        x = x_ref[s * MIX_SEG:(s + 1) * MIX_SEG, :]
        h = (x * (1.0 + scale) + shift).astype(BF16)
        proj = _dot(h, win_ref[...])
        glu = proj[:, :D_CONV] * jax.nn.sigmoid(proj[:, D_CONV:2 * D_CONV])
        feats.append(jnp.concatenate([glu, proj[:, 2 * D_CONV:]], axis=-1))
    zeros = jnp.zeros((CONV_PAD, LANES), F32)
    for s in range(n_seg):
        base = s * SEG_ROWS
        for slab in range(n_slab):
            lanes = slice(slab * LANES, (slab + 1) * LANES)
            pad_ref[slab, base:base + CONV_PAD, :] = (
                zeros if s == 0 else jnp.where(is_prompt, zeros, feats[s - 1][MIX_SEG - CONV_PAD:, lanes]))
            pad_ref[slab, base + CONV_PAD:base + CONV_PAD + MIX_SEG, :] = feats[s][:, lanes]
            pad_ref[slab, base + CONV_PAD + MIX_SEG:base + SEG_ROWS, :] = (
                zeros if s == n_seg - 1 else jnp.where(is_prompt, zeros, feats[s + 1][:CONV_PAD, lanes]))
    chunks_per_seg = MIX_SEG // CONV_ROWS
    for s in range(n_seg):
        rows = slice(s * MIX_SEG, (s + 1) * MIX_SEG)
        seg_chunks = range(s * chunks_per_seg, (s + 1) * chunks_per_seg)
        for slab in range(D_CONV // LANES):
            for j in seg_chunks:
                _conv_chunk(j, slab=slab, cw_ref=cw_ref, cb_ref=cb_ref, pad_ref=pad_ref, conv_ref=conv_ref)
        for j in seg_chunks:
            _pool_chunk(j, pad_ref=pad_ref, pool_ref=pool_ref, is_prompt=is_prompt)
        conv = jnp.concatenate([conv_ref[c, rows, :] for c in range(D_CONV // LANES)], axis=-1)
        a = _silu(_layer_norm(conv, cng_ref[...], cnb_ref[...]))
        pooled = [_dot(pool_ref[gi, rows, :].astype(BF16), pw_ref[gi]) for gi in range(len(POOL_WINDOWS))]
        bmix = jnp.concatenate(pooled, axis=-1) * ps_ref[...]
        cat = jnp.concatenate([a, bmix], axis=-1).astype(BF16)
        y = _dot(cat, wout_ref[...])
        o_ref[rows, :] = _layer_norm(ALPHA * x_ref[rows, :] + gate * y, g_ref[...], b_ref[...])


def _conv_pool_mixer(x, mod, ln_g, ln_b, w_in, conv_w, conv_b, cn_g, cn_b, pool_w, pool_scale, w_out,
                     c_all, w_ada, b_ada, next_layer):
    tm = MIX_TILE
    n_steps = N_TOKENS // tm
    ada_cols = N_MOD * D_MODEL // n_steps
    assert tm == DEC_SEQ and MIX_SEG == SEQ and N_PROMPT % tm == 0 and ada_cols % LANES == 0
    n_cslab = D_CONV // LANES
    row_spec = pl.BlockSpec((tm, D_MODEL), lambda t: (t, 0))
    in_specs = [
        row_spec,
        pl.BlockSpec((None, N_MOD, D_MODEL), lambda t: (_group_of_tile(t, tm), 0, 0)),
        _const_spec((1, D_MODEL)),
        _const_spec((1, D_MODEL)),
        _const_spec((D_MODEL, 2 * D_CONV + D_POOL)),
        _const_spec((n_cslab, CONV_WIDTH, LANES)),
        _const_spec((n_cslab, 1, LANES)),
        _const_spec((1, D_CONV)),
        _const_spec((1, D_CONV)),
        _const_spec((len(POOL_WINDOWS), POOL_GROUP, POOL_GROUP)),
        _const_spec((1, D_POOL)),
        _const_spec((D_CONV + D_POOL, D_MODEL)),
        _const_spec((N_GROUPS, D_MODEL)),
        pl.BlockSpec((None, D_MODEL, ada_cols), lambda t: (next_layer, 0, t)),
        pl.BlockSpec((None, 1, ada_cols), lambda t: (next_layer, 0, t)),
    ]
    conv_w_slabs = conv_w.reshape(CONV_WIDTH, n_cslab, LANES).transpose(1, 0, 2)
    args = [x, mod, ln_g.reshape(1, D_MODEL), ln_b.reshape(1, D_MODEL), w_in, conv_w_slabs,
            conv_b.reshape(n_cslab, 1, LANES), cn_g.reshape(1, D_CONV), cn_b.reshape(1, D_CONV),
            pool_w, pool_scale.reshape(1, D_POOL), w_out,
            c_all, w_ada, b_ada.reshape(DEPTH, 1, N_MOD * D_MODEL)]
    return pl.pallas_call(
        _cp_body,
        out_shape=[jax.ShapeDtypeStruct((N_TOKENS, D_MODEL), F32),
                   jax.ShapeDtypeStruct((N_GROUPS, N_MOD * D_MODEL), F32)],
        grid=(n_steps,),
        in_specs=in_specs,
        out_specs=[row_spec, pl.BlockSpec((N_GROUPS, ada_cols), lambda t: (0, t))],
        scratch_shapes=[
            pltpu.VMEM(((D_CONV + D_POOL) // LANES, tm // MIX_SEG * SEG_ROWS, LANES), F32),
            pltpu.VMEM((n_cslab, tm, LANES), F32),
            pltpu.VMEM((D_POOL // LANES, tm, LANES), F32),
        ],
        compiler_params=_params(),
        name="conv_pool",
    )(*args)


def _attend(q, keys, values):
    scores = [_dot_nt(q, k) for k in keys]
    m = functools.reduce(jnp.maximum, [jnp.max(s, axis=-1, keepdims=True) for s in scores])
    c = (QK_NOPE + QK_ROPE) ** -0.5 * math.log2(math.e)
    e = [jnp.exp2((s - m) * c) for s in scores]
    denom = functools.reduce(jnp.add, [jnp.sum(ei, axis=-1, keepdims=True) for ei in e])
    out = functools.reduce(jnp.add, [_dot(ei.astype(BF16), v) for ei, v in zip(e, values)])
    return out * (1.0 / denom)


def _mla_project(x, mod_ref, wdkv_ref, kvg_ref, wdq_ref, qg_ref, wuq_ref):
    h = (x * (1.0 + mod_ref[4:5, :]) + mod_ref[3:4, :]).astype(BF16)
    kv = _dot(h, wdkv_ref[...])
    ckv = _rms_norm(kv[:, :KV_LORA], kvg_ref[...])
    qd = _rms_norm(_dot(h, wdq_ref[...]), qg_ref[...]).astype(BF16)
    return kv, ckv, qd, _dot(qd, wuq_ref[...])


def _mla_out(x, attn, mod_ref, g_ref, b_ref, wo_ref):
    y = _dot(attn, wo_ref[...])
    return _layer_norm(ALPHA * x + mod_ref[5:6, :] * y, g_ref[...], b_ref[...])


def _mla_prompt_body(x_ref, mod_ref, g_ref, b_ref, wdkv_ref, kvg_ref, wdq_ref, qg_ref, wuq_ref,
                     wuk_ref, wuv_ref, wo_ref, o_ref, ckv_ref, kr_ref):
    for s in range(x_ref.shape[0] // SEQ):
        rows = slice(s * SEQ, (s + 1) * SEQ)
        x = x_ref[rows, :]
        kv, ckv, _, q = _mla_project(x, mod_ref, wdkv_ref, kvg_ref, wdq_ref, qg_ref, wuq_ref)
        ckv_ref[rows, :] = ckv
        kr_ref[rows, :] = kv[:, KV_LORA:KV_LORA + QK_ROPE]
        q16 = q.astype(BF16)
        ckv16 = ckv.astype(BF16)
        kn16 = _dot(ckv16, wuk_ref[...]).astype(BF16)
        v16 = _dot(ckv16, wuv_ref[...]).astype(BF16)
        kr16 = kv[:, KV_LORA:KV_LORA + LANES].astype(BF16)
        heads = []
        for hd in range(N_HEADS):
            vl = slice(hd * V_DIM, (hd + 1) * V_DIM)
            k_h = jnp.concatenate([kn16[:, hd * QK_NOPE:(hd + 1) * QK_NOPE], kr16], axis=-1)
            heads.append(_attend(q16[:, hd * HEAD_W:(hd + 1) * HEAD_W], [k_h], [v16[:, vl]]).astype(BF16))
        attn = jnp.concatenate(heads, axis=-1)
        o_ref[rows, :] = _mla_out(x, attn, mod_ref, g_ref, b_ref, wo_ref)


def _store_keys(k_ref, rows, kn16, kr16):
    for hd in range(N_HEADS):
        k_ref[rows, hd * HEAD_W:hd * HEAD_W + QK_NOPE] = kn16[:, hd * QK_NOPE:(hd + 1) * QK_NOPE]
        k_ref[rows, hd * HEAD_W + QK_NOPE:(hd + 1) * HEAD_W] = kr16


def _mla_latent_body(x_ref, mod_ref, g_ref, b_ref, wdkv_ref, kvg_ref, wdq_ref, qg_ref, wuq_ref,
                     wuk_ref, wuv_ref, wo_ref, wuqs_ref, cos_ref, sin_ref, cckv_ref, ckr_ref,
                     o_ref, q_s, k_s, v_s, kc_s, vc_s):
    c16 = cckv_ref[...].astype(BF16)
    vc_s[...] = _dot(c16, wuv_ref[...]).astype(BF16)
    _store_keys(kc_s, slice(None), _dot(c16, wuk_ref[...]).astype(BF16), ckr_ref[...].astype(BF16))
    for s in range(DEC_SEQ // ATTN_Q_TILE):
        rows = slice(s * ATTN_Q_TILE, (s + 1) * ATTN_Q_TILE)
        cos = cos_ref[rows, :]
        sin = sin_ref[rows, :]
        kv, ckv, qd, q = _mla_project(x_ref[rows, :], mod_ref, wdkv_ref, kvg_ref, wdq_ref, qg_ref, wuq_ref)
        q_swapped = _dot(qd, wuqs_ref[...])
        kr = kv[:, KV_LORA:KV_LORA + LANES] * cos + kv[:, KV_LORA + LANES:KV_LORA + 2 * LANES] * sin
        ckv16 = ckv.astype(BF16)
        v_s[rows, :] = _dot(ckv16, wuv_ref[...]).astype(BF16)
        _store_keys(k_s, rows, _dot(ckv16, wuk_ref[...]).astype(BF16), kr.astype(BF16))
        for hd in range(N_HEADS):
            lo = hd * HEAD_W
            q_s[rows, lo:lo + QK_NOPE] = q[:, lo:lo + QK_NOPE].astype(BF16)
            q_rope = q[:, lo + QK_NOPE:lo + HEAD_W] * cos + q_swapped[:, hd * LANES:(hd + 1) * LANES] * sin
            q_s[rows, lo + QK_NOPE:lo + HEAD_W] = q_rope.astype(BF16)

    def q_block(i, carry):
        rows = pl.ds(pl.multiple_of(i * ATTN_Q_TILE, ATTN_Q_TILE), ATTN_Q_TILE)
        heads = []
        for hd in range(N_HEADS):
            ql = slice(hd * HEAD_W, (hd + 1) * HEAD_W)
            vl = slice(hd * V_DIM, (hd + 1) * V_DIM)
            heads.append(_attend(q_s[rows, ql], [k_s[:, ql], kc_s[:, ql]],
                                 [v_s[:, vl], vc_s[:, vl]]).astype(BF16))
        attn = jnp.concatenate(heads, axis=-1)
        o_ref[rows, :] = _mla_out(x_ref[rows, :], attn, mod_ref, g_ref, b_ref, wo_ref)
        return carry

    lax.fori_loop(0, DEC_SEQ // ATTN_Q_TILE, q_block, 0)


def _mla_mixer(x, mod, ln_g, ln_b, w, cache_ckv, cache_kr, *, latent):
    tile = MLA_TILE
    first = N_PROMPT // tile if latent else 0
    n_rows = DEC_BATCH * DEC_SEQ if latent else N_PROMPT
    dkv = w["dkv"] if latent else w["dkv"][:, :KV_LORA + LANES]
    in_specs = [
        pl.BlockSpec((tile, D_MODEL), lambda t: (first + t, 0)),
        pl.BlockSpec((None, N_MOD, D_MODEL), lambda t: (_group_of_tile(first + t, tile), 0, 0)),
        _const_spec((1, D_MODEL)),
        _const_spec((1, D_MODEL)),
        _const_spec(dkv.shape),
        _const_spec((1, KV_LORA)),
        _const_spec((D_MODEL, Q_LORA)),
        _const_spec((1, Q_LORA)),
        _const_spec((Q_LORA, N_HEADS * HEAD_W)),
        _const_spec((KV_LORA, N_HEADS * QK_NOPE)),
        _const_spec((KV_LORA, N_HEADS * V_DIM)),
        _const_spec((N_HEADS * V_DIM, D_MODEL)),
    ]
    args = [x, mod, ln_g.reshape(1, D_MODEL), ln_b.reshape(1, D_MODEL), dkv, w["kv_g"], w["dq"],
            w["q_g"], w["uq"], w["uk"], w["uv"], w["o"]]
    out_shape = [jax.ShapeDtypeStruct((n_rows, D_MODEL), F32)]
    out_specs = [pl.BlockSpec((tile, D_MODEL), lambda t: (t, 0))]
    scratch = []
    if latent:
        assert tile == DEC_SEQ
        in_specs += [
            _const_spec((Q_LORA, N_HEADS * LANES)),
            _const_spec((DEC_SEQ, LANES)),
            _const_spec((DEC_SEQ, LANES)),
            pl.BlockSpec((None, PAST_LEN, KV_LORA), lambda t: (t, 0, 0)),
            pl.BlockSpec((None, PAST_LEN, LANES), lambda t: (t, 0, 0)),
        ]
        args += [w["uq_swapped"], w["cos"], w["sin"], cache_ckv, cache_kr]
        scratch = [
            pltpu.VMEM((DEC_SEQ, N_HEADS * HEAD_W), BF16),
            pltpu.VMEM((DEC_SEQ, N_HEADS * HEAD_W), BF16),
            pltpu.VMEM((DEC_SEQ, N_HEADS * V_DIM), BF16),
            pltpu.VMEM((PAST_LEN, N_HEADS * HEAD_W), BF16),
            pltpu.VMEM((PAST_LEN, N_HEADS * V_DIM), BF16),
        ]
    else:
        out_shape += [jax.ShapeDtypeStruct((n_rows, KV_LORA), F32),
                      jax.ShapeDtypeStruct((n_rows, QK_ROPE), F32)]
        out_specs += [pl.BlockSpec((tile, KV_LORA), lambda t: (t, 0)),
                      pl.BlockSpec((tile, QK_ROPE), lambda t: (t, 0))]
    outs = pl.pallas_call(
        _mla_latent_body if latent else _mla_prompt_body,
        out_shape=out_shape,
        grid=(n_rows // tile,),
        in_specs=in_specs,
        out_specs=out_specs,
        scratch_shapes=scratch,
        compiler_params=_params(),
        name="mla_latent" if latent else "mla_prompt",
    )(*args)
    return outs[0] if latent else outs


def _rope_swap_perm():
    idx = np.arange(QK_ROPE)
    return np.where((idx % 32) < ROPE_AXIS_PAIRS, idx + ROPE_AXIS_PAIRS, idx - ROPE_AXIS_PAIRS)


def _rope_tables():
    n = DEC_SEQ
    row = np.repeat(np.arange(n // GRID_W), GRID_W)
    col = np.tile(np.arange(GRID_W), n // GRID_W)
    inv = ROPE_BASE ** (-np.arange(ROPE_AXIS_PAIRS, dtype=np.float64) / ROPE_AXIS_PAIRS)
    ar = row[:, None] * inv
    ac = col[:, None] * inv
    zeros = np.zeros((n, LANES - QK_ROPE))
    cos = np.concatenate([np.cos(ar), np.cos(ar), np.cos(ac), np.cos(ac), zeros], axis=-1)
    sin = np.concatenate([-np.sin(ar), np.sin(ar), -np.sin(ac), np.sin(ac), zeros], axis=-1)
    return jnp.asarray(cos, F32), jnp.asarray(sin, F32)


def _mla_weights(w_dq, q_norm_g, w_uq, w_dkv, kv_norm_g, w_ukv, w_o):
    swap = _rope_swap_perm()
    z64 = jnp.zeros((D_MODEL, LANES - QK_ROPE), F32)
    kr_cols = w_dkv[:, KV_LORA:]
    dkv = jnp.concatenate([w_dkv[:, :KV_LORA], kr_cols, z64, kr_cols[:, swap], z64], axis=-1)
    uq = w_uq.reshape(Q_LORA, N_HEADS, QK_NOPE + QK_ROPE)
    zq = jnp.zeros((Q_LORA, N_HEADS, LANES - QK_ROPE), F32)
    uq_main = jnp.concatenate([uq, zq], axis=-1).reshape(Q_LORA, N_HEADS * HEAD_W)
    uq_swapped = jnp.concatenate([uq[:, :, QK_NOPE:][:, :, swap], zq], axis=-1).reshape(Q_LORA, N_HEADS * LANES)
    ukv = w_ukv.reshape(KV_LORA, N_HEADS, QK_NOPE + V_DIM)
    cos, sin = _rope_tables()
    return {
        "dkv": dkv.astype(BF16),
        "kv_g": kv_norm_g.reshape(1, KV_LORA),
        "dq": w_dq.astype(BF16),
        "q_g": q_norm_g.reshape(1, Q_LORA),
        "uq": uq_main.astype(BF16),
        "uq_swapped": uq_swapped.astype(BF16),
        "uk": ukv[:, :, :QK_NOPE].reshape(KV_LORA, N_HEADS * QK_NOPE).astype(BF16),
        "uv": ukv[:, :, QK_NOPE:].reshape(KV_LORA, N_HEADS * V_DIM).astype(BF16),
        "o": w_o.astype(BF16),
        "cos": cos,
        "sin": sin,
    }


def kernel(x_prompt, x_sample, cache_mla_ckv, cache_mla_krope, c, c_ctx, w_ada, b_ada, ln_g, ln_b, ffn_w1, ffn_w3, ffn_w2, cp_w_in, conv_w, conv_b, conv_norm_g, conv_norm_b, pool_w, pool_scale, cp_w_out, mla_w_dq, mla_q_norm_g, mla_w_uq, mla_w_dkv, mla_kv_norm_g, mla_w_ukv, mla_w_o):
    c_all = jnp.concatenate(
        [c_ctx[None, :], c, jnp.zeros((N_GROUPS - 1 - DEC_BATCH, D_MODEL), F32)], axis=0)
    ffn_f32 = (ffn_w1, ffn_w3, ffn_w2)
    ada0, w_bf16 = _ada(c_all, w_ada, b_ada, 0, ffn_f32, (0, 0))
    ada = {0: ada0}

    x = (x_prompt.reshape(N_PROMPT, D_MODEL), x_sample.reshape(DEC_BATCH * DEC_SEQ, D_MODEL))
    new_ckv, new_krope = [], []
    for i in range(DEPTH):
        mod = ada[i].reshape(N_GROUPS, N_MOD, D_MODEL)
        j = i // 2
        x, w_bf16 = _ffn(x, mod, ln_g[i, 0], ln_b[i, 0], *w_bf16, ffn_f32, (i, 1), k0=0)
        if i % 2 == 0:
            x, ada_next = _conv_pool_mixer(
                x, mod, ln_g[i, 1], ln_b[i, 1], cp_w_in[j].astype(BF16), conv_w[j], conv_b[j],
                conv_norm_g[j], conv_norm_b[j], pool_w[j].astype(BF16), pool_scale[j],
                cp_w_out[j].astype(BF16), c_all, w_ada, b_ada, min(i + 1, DEPTH - 1))
            if i + 1 < DEPTH:
                ada[i + 1] = ada_next
        else:
            w = _mla_weights(mla_w_dq[j], mla_q_norm_g[j], mla_w_uq[j], mla_w_dkv[j],
                             mla_kv_norm_g[j], mla_w_ukv[j], mla_w_o[j])
            attn_args = (x, mod, ln_g[i, 1], ln_b[i, 1], w)
            y_p, ckv_p, kr_p = _mla_mixer(*attn_args, None, None, latent=False)
            new_ckv.append(ckv_p.reshape(BATCH, SEQ, KV_LORA))
            new_krope.append(kr_p.reshape(BATCH, SEQ, QK_ROPE))
            kr_cache = jnp.pad(cache_mla_krope[:, j], ((0, 0), (0, 0), (0, LANES - QK_ROPE)))
            x = (y_p, _mla_mixer(*attn_args, cache_mla_ckv[:, j], kr_cache, latent=True))
        last = i == DEPTH - 1
        x, w_bf16 = _ffn(x, mod, ln_g[i, 2], ln_b[i, 2], *w_bf16,
                         None if last else ffn_f32, None if last else (i + 1, 0),
                         k0=6, split_out=last)

    y_prompt = x[0].reshape(BATCH, SEQ, D_MODEL)
    y_sample = x[1].reshape(DEC_BATCH, DEC_SEQ, D_MODEL)
    return (y_prompt, y_sample, jnp.stack(new_ckv, axis=1), jnp.stack(new_krope, axis=1))
```

```python
import functools
import math

import jax
import jax.numpy as jnp
import numpy as np
from jax import lax
from jax.experimental import pallas as pl
from jax.experimental.pallas import tpu as pltpu

F32 = jnp.float32
BF16 = jnp.bfloat16

D_MODEL = 1024
BATCH = 32
SEQ = 256
DEPTH = 2
DEC_BATCH = 4
DEC_SEQ = 1024
PAST_LEN = 512
GRID_W = 64
N_MOD = 9
D_FF = 2816
D_CONV = 512
CONV_WIDTH = 31
D_POOL = 512
POOL_WINDOWS = (2, 4, 8, 16)
POOL_GROUP = 128
N_HEADS = 8
QK_NOPE = 128
QK_ROPE = 64
V_DIM = 128
KV_LORA = 256
Q_LORA = 384
ROPE_AXIS_PAIRS = 16
ROPE_BASE = 10000.0
ALPHA = (2 * DEPTH) ** 0.25
LN_EPS = 1e-5
RMS_EPS = 1e-6

LANES = 128
SUBLANES = 8

N_PROMPT = BATCH * SEQ
N_TOKENS = N_PROMPT + DEC_BATCH * DEC_SEQ
N_GROUPS = 8
HEAD_W = 256
CONV_PAD = 16
CONV_STRIDE = 4
CONV_ROWS = SUBLANES * CONV_STRIDE

VMEM_LIMIT_BYTES = 60 * 1024 * 1024

FFN_TILES = (1024, 512)
FFN_WORK_BYTES = 4 * 1024 * 1024
FFN_SUB = 256
MIX_TILE = 1024
MIX_SEG = 256
SEG_ROWS = MIX_SEG + 2 * CONV_PAD
MLA_TILE = 1024
ATTN_Q_TILE = 256


def _layer_norm(z, g, b):
    mu = jnp.mean(z, axis=-1, keepdims=True)
    zc = z - mu
    var = jnp.mean(zc * zc, axis=-1, keepdims=True)
    return zc * lax.rsqrt(var + LN_EPS) * g + b


def _rms_norm(z, g):
    return z * lax.rsqrt(jnp.mean(z * z, axis=-1, keepdims=True) + RMS_EPS) * g


def _silu(z):
    return z * jax.nn.sigmoid(z)


def _dot(a, b):
    return jnp.dot(a, b, preferred_element_type=F32)


def _dot_nt(a, b):
    return lax.dot_general(a, b, (((1,), (1,)), ((), ())), preferred_element_type=F32)


def _const_spec(shape):
    nd = len(shape)
    return pl.BlockSpec(shape, lambda *_: (0,) * nd, pipeline_mode=pl.Buffered(1))


def _group_of_tile(t, tile):
    n_p = N_PROMPT // tile
    per_seq = DEC_SEQ // tile
    return jnp.where(t < n_p, 0, 1 + (t - n_p) // per_seq)


def _params(n_axes=1):
    return pltpu.CompilerParams(
        dimension_semantics=("arbitrary",) * n_axes,
        vmem_limit_bytes=VMEM_LIMIT_BYTES,
    )


def _ada_body(c_ref, w_ref, b_ref, o_ref):
    s = _silu(c_ref[...]).astype(BF16)
    o_ref[...] = _dot(s, w_ref[...].astype(BF16)) + b_ref[...]


def _slab_rows(n_rows, n_steps):
    packed_rows = 2 * SUBLANES
    n_slabs = max(d for d in range(1, n_steps + 1)
                  if n_rows % d == 0 and (n_rows // d) % packed_rows == 0)
    return n_rows // n_slabs, n_slabs


def _cast_side_job(mats, n_steps):
    in_specs, out_shapes, out_specs = [], [], []
    for wn, lead in mats:
        n_rows, n_cols = wn.shape[-2:]
        rows, n_slabs = _slab_rows(n_rows, n_steps)
        in_specs.append(pl.BlockSpec(
            (None,) * len(lead) + (rows, n_cols),
            lambda t, n_slabs=n_slabs, lead=tuple(lead): lead + (jnp.minimum(t, n_slabs - 1), 0)))
        out_shapes.append(jax.ShapeDtypeStruct((n_rows, n_cols), BF16))
        out_specs.append(pl.BlockSpec(
            (rows, n_cols), lambda t, n_slabs=n_slabs: (jnp.minimum(t, n_slabs - 1), 0)))
    return in_specs, [wn for wn, _ in mats], out_shapes, out_specs


def _cast_slabs(srcs, dsts):
    for src, dst in zip(srcs, dsts):
        dst[...] = src[...].astype(BF16)


def _ada_cast_body(c_ref, w_ref, b_ref, *refs):
    n = (len(refs) - 1) // 2
    _cast_slabs(refs[:n], refs[n + 1:])
    _ada_body(c_ref, w_ref, b_ref, refs[n])


def _ada(c_all, w_ada, b_ada, layer, cast_mats):
    tn = D_MODEL
    c_specs, c_args, c_shapes, c_out_specs = _cast_side_job(cast_mats, N_MOD)
    outs = pl.pallas_call(
        _ada_cast_body,
        out_shape=[jax.ShapeDtypeStruct((N_GROUPS, N_MOD * D_MODEL), F32)] + c_shapes,
        grid=(N_MOD,),
        in_specs=[
            pl.BlockSpec((N_GROUPS, D_MODEL), lambda k: (0, 0)),
            pl.BlockSpec((None, D_MODEL, tn), lambda k: (layer, 0, k)),
            pl.BlockSpec((None, 1, tn), lambda k: (layer, 0, k)),
        ] + c_specs,
        out_specs=[pl.BlockSpec((N_GROUPS, tn), lambda k: (0, k))] + c_out_specs,
        compiler_params=_params(),
        name="ada",
    )(c_all, w_ada, b_ada.reshape(DEPTH, 1, N_MOD * D_MODEL), *c_args)
    return outs[0], tuple(outs[1:])


def _ffn_body(*refs, k0, split_in, split_out, cast_next):
    refs = list(refs)
    n_x = 2 if split_in else 1
    x_refs, refs = refs[:n_x], refs[n_x:]
    (mod_ref, g_ref, b_ref, w1_ref, w3_ref, w2_ref), refs = refs[:6], refs[6:]
    if cast_next:
        next_f32, refs = refs[:3], refs[3:]
    n_o = 2 if split_out else 1
    o_refs, refs = refs[:n_o], refs[n_o:]
    if cast_next:
        next_bf16, refs = refs[:3], refs[3:]
    tile = x_refs[0].shape[0]
    is_prompt = pl.program_id(0) < N_PROMPT // tile
    if cast_next:
        _cast_slabs(next_f32, next_bf16)
    shift = mod_ref[k0:k0 + 1, :]
    scale = mod_ref[k0 + 1:k0 + 2, :]
    gate = mod_ref[k0 + 2:k0 + 3, :]
    res_ref = refs[0] if split_out else o_refs[0]
    for s in range(tile // FFN_SUB):
        rows = slice(s * FFN_SUB, (s + 1) * FFN_SUB)
        if split_in:
            x = jnp.where(is_prompt, x_refs[0][rows, :], x_refs[1][rows, :])
        else:
            x = x_refs[0][rows, :]
        h = (x * (1.0 + scale) + shift).astype(BF16)
        a = (_silu(_dot(h, w1_ref[...])) * _dot(h, w3_ref[...])).astype(BF16)
        y = _dot(a, w2_ref[...])
        res_ref[rows, :] = _layer_norm(ALPHA * x + (0.5 * gate) * y, g_ref[...], b_ref[...])
    if split_out:
        @pl.when(is_prompt)
        def _():
            o_refs[0][...] = res_ref[...]

        @pl.when(jnp.logical_not(is_prompt))
        def _():
            o_refs[1][...] = res_ref[...]


def _ffn_tile(n_in, split_out, next_w):
    for tm in FFN_TILES:
        n_steps = N_TOKENS // tm
        tile_bytes = tm * D_MODEL * 4
        est = 3 * D_MODEL * D_FF * 2
        est += (n_in + (2 if split_out else 1)) * 2 * tile_bytes
        est += FFN_WORK_BYTES + (tile_bytes if split_out else 0)
        if next_w is not None:
            for wn in next_w:
                rows, _ = _slab_rows(wn.shape[2], n_steps)
                est += 2 * rows * wn.shape[3] * (4 + 2)
        if est <= VMEM_LIMIT_BYTES:
            return tm
    raise ValueError("no FFN tile fits in VMEM")


def _ffn(xs, mod, ln_g, ln_b, w1, w3, w2, next_w=None, next_idx=None, *, k0, split_out=False):
    split_in = isinstance(xs, (tuple, list))
    xs = list(xs) if split_in else [xs]
    tm = _ffn_tile(len(xs), split_out, next_w)
    n_steps = N_TOKENS // tm
    n_p = N_PROMPT // tm
    prompt_spec = pl.BlockSpec((tm, D_MODEL), lambda t: (jnp.minimum(t, n_p - 1), 0))
    latent_spec = pl.BlockSpec((tm, D_MODEL), lambda t: (jnp.maximum(t - n_p, 0), 0))
    merged_spec = pl.BlockSpec((tm, D_MODEL), lambda t: (t, 0))
    in_specs = ([prompt_spec, latent_spec] if split_in else [merged_spec]) + [
        pl.BlockSpec((None, N_MOD, D_MODEL), lambda t: (_group_of_tile(t, tm), 0, 0)),
        _const_spec((1, D_MODEL)),
        _const_spec((1, D_MODEL)),
        _const_spec((D_MODEL, D_FF)),
        _const_spec((D_MODEL, D_FF)),
        _const_spec((D_FF, D_MODEL)),
    ]
    args = xs + [mod, ln_g.reshape(1, D_MODEL), ln_b.reshape(1, D_MODEL), w1, w3, w2]
    if split_out:
        out_shape = [jax.ShapeDtypeStruct((N_PROMPT, D_MODEL), F32),
                     jax.ShapeDtypeStruct((N_TOKENS - N_PROMPT, D_MODEL), F32)]
        out_specs = [prompt_spec, latent_spec]
    else:
        out_shape = [jax.ShapeDtypeStruct((N_TOKENS, D_MODEL), F32)]
        out_specs = [merged_spec]
    if next_w is not None:
        c_specs, c_args, c_shapes, c_out_specs = _cast_side_job(
            [(wn, next_idx) for wn in next_w], n_steps)
        in_specs += c_specs
        args += c_args
        out_shape += c_shapes
        out_specs += c_out_specs
    outs = pl.pallas_call(
        functools.partial(_ffn_body, k0=k0, split_in=split_in, split_out=split_out,
                          cast_next=next_w is not None),
        out_shape=out_shape,
        grid=(n_steps,),
        in_specs=in_specs,
        out_specs=out_specs,
        scratch_shapes=[pltpu.VMEM((tm, D_MODEL), F32)] if split_out else [],
        compiler_params=_params(),
        name="ffn",
    )(*args)
    n_tok = 2 if split_out else 1
    tok = tuple(outs[:n_tok]) if split_out else outs[0]
    return tok, tuple(outs[n_tok:])


def _strided_rows(start):
    return pl.ds(start, SUBLANES, stride=CONV_STRIDE)


def _chunk_rows(j):
    chunks_per_seg = MIX_SEG // CONV_ROWS
    r0 = (j % chunks_per_seg) * CONV_ROWS
    return (j // chunks_per_seg) * SEG_ROWS + r0 + CONV_PAD, j * CONV_ROWS, r0


def _conv_chunk(j, *, slab, cw_ref, cb_ref, pad_ref, conv_ref):
    p0, o0, _ = _chunk_rows(j)
    half = CONV_WIDTH // 2
    n_part = 2
    acc = [[None] * n_part for _ in range(CONV_STRIDE)]
    weights = {}
    for off in range(-half, half + CONV_STRIDE):
        tap = pad_ref[slab, _strided_rows(p0 + off), :]
        for rho in range(CONV_STRIDE):
            k = off - rho + half
            if 0 <= k < CONV_WIDTH:
                if k not in weights:
                    weights[k] = cw_ref[slab, pl.ds(k, SUBLANES, stride=0), :]
                term = weights[k] * tap
                part = acc[rho][k % n_part]
                acc[rho][k % n_part] = term if part is None else part + term
    bias = cb_ref[slab, pl.ds(0, SUBLANES, stride=0), :]
    for rho in range(CONV_STRIDE):
        conv_ref[slab, _strided_rows(o0 + rho), :] = (acc[rho][0] + acc[rho][1]) + bias


def _pool_chunk(j, *, pad_ref, pool_ref, is_prompt):
    p0, o0, r0 = _chunk_rows(j)
    at_edge = r0 == 0 or r0 == MIX_SEG - CONV_ROWS
    if at_edge:
        seq_len = jnp.where(is_prompt, SEQ, DEC_SEQ)
        t0 = jnp.where(is_prompt, r0, o0) + CONV_STRIDE * lax.broadcasted_iota(jnp.int32, (SUBLANES, LANES), 0)
    for gi, w in enumerate(POOL_WINDOWS):
        left = w // 2
        right = w - 1 - left
        slab = D_CONV // LANES + gi
        total = [None] * CONV_STRIDE
        for off in range(-left, right + CONV_STRIDE):
            tap = pad_ref[slab, _strided_rows(p0 + off), :]
            for rho in range(CONV_STRIDE):
                if -left <= off - rho <= right:
                    total[rho] = tap if total[rho] is None else total[rho] + tap
        for rho in range(CONV_STRIDE):
            if at_edge:
                t = t0 + rho
                cnt = (jnp.minimum(t + right, seq_len - 1) - jnp.maximum(t - left, 0) + 1).astype(F32)
                mean = total[rho] / cnt
            else:
                mean = total[rho] / float(w)
            tok = pad_ref[slab, _strided_rows(p0 + rho), :]
            pool_ref[gi, _strided_rows(o0 + rho), :] = mean - tok


def _cp_body(x_ref, mod_ref, g_ref, b_ref, win_ref, cw_ref, cb_ref, cng_ref, cnb_ref,
             pw_ref, ps_ref, wout_ref, c_ref, wada_ref, bada_ref, o_ref, ada_ref,
             pad_ref, conv_ref, pool_ref):
    is_prompt = pl.program_id(0) < N_PROMPT // MIX_TILE
    _ada_body(c_ref, wada_ref, bada_ref, ada_ref)
    shift = mod_ref[3:4, :]
    scale = mod_ref[4:5, :]
    gate = mod_ref[5:6, :]
    n_seg = MIX_TILE // MIX_SEG
    n_slab = (D_CONV + D_POOL) // LANES
    feats = []
    for s in range(n_seg):
        x = x_ref[s * MIX_SEG:(s + 1) * MIX_SEG, :]
        h = (x * (1.0 + scale) + shift).astype(BF16)
        proj = _dot(h, win_ref[...])
        glu = proj[:, :D_CONV] * jax.nn.sigmoid(proj[:, D_CONV:2 * D_CONV])
        feats.append(jnp.concatenate([glu, proj[:, 2 * D_CONV:]], axis=-1))
    zeros = jnp.zeros((CONV_PAD, LANES), F32)
    for s in range(n_seg):
        base = s * SEG_ROWS
        for slab in range(n_slab):
            lanes = slice(slab * LANES, (slab + 1) * LANES)
            pad_ref[slab, base:base + CONV_PAD, :] = (
                zeros if s == 0 else jnp.where(is_prompt, zeros, feats[s - 1][MIX_SEG - CONV_PAD:, lanes]))
            pad_ref[slab, base + CONV_PAD:base + CONV_PAD + MIX_SEG, :] = feats[s][:, lanes]
            pad_ref[slab, base + CONV_PAD + MIX_SEG:base + SEG_ROWS, :] = (
                zeros if s == n_seg - 1 else jnp.where(is_prompt, zeros, feats[s + 1][:CONV_PAD, lanes]))
    chunks_per_seg = MIX_SEG // CONV_ROWS
    for s in range(n_seg):
        rows = slice(s * MIX_SEG, (s + 1) * MIX_SEG)
        seg_chunks = range(s * chunks_per_seg, (s + 1) * chunks_per_seg)
        for slab in range(D_CONV // LANES):
            for j in seg_chunks:
                _conv_chunk(j, slab=slab, cw_ref=cw_ref, cb_ref=cb_ref, pad_ref=pad_ref, conv_ref=conv_ref)
        for j in seg_chunks:
            _pool_chunk(j, pad_ref=pad_ref, pool_ref=pool_ref, is_prompt=is_prompt)
        conv = jnp.concatenate([conv_ref[c, rows, :] for c in range(D_CONV // LANES)], axis=-1)
        a = _silu(_layer_norm(conv, cng_ref[...], cnb_ref[...]))
        pooled = [_dot(pool_ref[gi, rows, :].astype(BF16), pw_ref[gi]) for gi in range(len(POOL_WINDOWS))]
        bmix = jnp.concatenate(pooled, axis=-1) * ps_ref[...]
        cat = jnp.concatenate([a, bmix], axis=-1).astype(BF16)
        y = _dot(cat, wout_ref[...])
        o_ref[rows, :] = _layer_norm(ALPHA * x_ref[rows, :] + gate * y, g_ref[...], b_ref[...])


def _conv_pool_mixer(x, mod, ln_g, ln_b, w_in, conv_w, conv_b, cn_g, cn_b, pool_w, pool_scale, w_out,
                     c_all, w_ada, b_ada, next_layer):
    tm = MIX_TILE
    n_steps = N_TOKENS // tm
    ada_cols = N_MOD * D_MODEL // n_steps
    assert tm == DEC_SEQ and MIX_SEG == SEQ and N_PROMPT % tm == 0 and ada_cols % LANES == 0
    n_cslab = D_CONV // LANES
    row_spec = pl.BlockSpec((tm, D_MODEL), lambda t: (t, 0))
    in_specs = [
        row_spec,
        pl.BlockSpec((None, N_MOD, D_MODEL), lambda t: (_group_of_tile(t, tm), 0, 0)),
        _const_spec((1, D_MODEL)),
        _const_spec((1, D_MODEL)),
        _const_spec((D_MODEL, 2 * D_CONV + D_POOL)),
        _const_spec((n_cslab, CONV_WIDTH, LANES)),
        _const_spec((n_cslab, 1, LANES)),
        _const_spec((1, D_CONV)),
        _const_spec((1, D_CONV)),
        _const_spec((len(POOL_WINDOWS), POOL_GROUP, POOL_GROUP)),
        _const_spec((1, D_POOL)),
        _const_spec((D_CONV + D_POOL, D_MODEL)),
        _const_spec((N_GROUPS, D_MODEL)),
        pl.BlockSpec((None, D_MODEL, ada_cols), lambda t: (next_layer, 0, t)),
        pl.BlockSpec((None, 1, ada_cols), lambda t: (next_layer, 0, t)),
    ]
    conv_w_slabs = conv_w.reshape(CONV_WIDTH, n_cslab, LANES).transpose(1, 0, 2)
    args = [x, mod, ln_g.reshape(1, D_MODEL), ln_b.reshape(1, D_MODEL), w_in, conv_w_slabs,
            conv_b.reshape(n_cslab, 1, LANES), cn_g.reshape(1, D_CONV), cn_b.reshape(1, D_CONV),
            pool_w, pool_scale.reshape(1, D_POOL), w_out,
            c_all, w_ada, b_ada.reshape(DEPTH, 1, N_MOD * D_MODEL)]
    return pl.pallas_call(
        _cp_body,
        out_shape=[jax.ShapeDtypeStruct((N_TOKENS, D_MODEL), F32),
                   jax.ShapeDtypeStruct((N_GROUPS, N_MOD * D_MODEL), F32)],
        grid=(n_steps,),
        in_specs=in_specs,
        out_specs=[row_spec, pl.BlockSpec((N_GROUPS, ada_cols), lambda t: (0, t))],
        scratch_shapes=[
            pltpu.VMEM(((D_CONV + D_POOL) // LANES, tm // MIX_SEG * SEG_ROWS, LANES), F32),
            pltpu.VMEM((n_cslab, tm, LANES), F32),
            pltpu.VMEM((D_POOL // LANES, tm, LANES), F32),
        ],
        compiler_params=_params(),
        name="conv_pool",
    )(*args)


def _attend(q, keys, values):
    scores = [_dot_nt(q, k) for k in keys]
    m = functools.reduce(jnp.maximum, [jnp.max(s, axis=-1, keepdims=True) for s in scores])
    c = (QK_NOPE + QK_ROPE) ** -0.5 * math.log2(math.e)
    e = [jnp.exp2((s - m) * c) for s in scores]
    denom = functools.reduce(jnp.add, [jnp.sum(ei, axis=-1, keepdims=True) for ei in e])
    out = functools.reduce(jnp.add, [_dot(ei.astype(BF16), v) for ei, v in zip(e, values)])
    return out * (1.0 / denom)


def _mla_project(x, mod_ref, wdkv_ref, kvg_ref, wdq_ref, qg_ref, wuq_ref):
    h = (x * (1.0 + mod_ref[4:5, :]) + mod_ref[3:4, :]).astype(BF16)
    kv = _dot(h, wdkv_ref[...])
    ckv = _rms_norm(kv[:, :KV_LORA], kvg_ref[...])
    qd = _rms_norm(_dot(h, wdq_ref[...]), qg_ref[...]).astype(BF16)
    return kv, ckv, qd, _dot(qd, wuq_ref[...])


def _mla_out(x, attn, mod_ref, g_ref, b_ref, wo_ref):
    y = _dot(attn, wo_ref[...])
    return _layer_norm(ALPHA * x + mod_ref[5:6, :] * y, g_ref[...], b_ref[...])


def _mla_prompt_body(x_ref, mod_ref, g_ref, b_ref, wdkv_ref, kvg_ref, wdq_ref, qg_ref, wuq_ref,
                     wuk_ref, wuv_ref, wo_ref, o_ref, ckv_ref, kr_ref):
    for s in range(x_ref.shape[0] // SEQ):
        rows = slice(s * SEQ, (s + 1) * SEQ)
        x = x_ref[rows, :]
        kv, ckv, _, q = _mla_project(x, mod_ref, wdkv_ref, kvg_ref, wdq_ref, qg_ref, wuq_ref)
        ckv_ref[rows, :] = ckv
        kr_ref[rows, :] = kv[:, KV_LORA:KV_LORA + QK_ROPE]
        q16 = q.astype(BF16)
        ckv16 = ckv.astype(BF16)
        kn16 = _dot(ckv16, wuk_ref[...]).astype(BF16)
        v16 = _dot(ckv16, wuv_ref[...]).astype(BF16)
        kr16 = kv[:, KV_LORA:KV_LORA + LANES].astype(BF16)
        heads = []
        for hd in range(N_HEADS):
            vl = slice(hd * V_DIM, (hd + 1) * V_DIM)
            k_h = jnp.concatenate([kn16[:, hd * QK_NOPE:(hd + 1) * QK_NOPE], kr16], axis=-1)
            heads.append(_attend(q16[:, hd * HEAD_W:(hd + 1) * HEAD_W], [k_h], [v16[:, vl]]).astype(BF16))
        attn = jnp.concatenate(heads, axis=-1)
        o_ref[rows, :] = _mla_out(x, attn, mod_ref, g_ref, b_ref, wo_ref)


def _store_keys(k_ref, rows, kn16, kr16):
    for hd in range(N_HEADS):
        k_ref[rows, hd * HEAD_W:hd * HEAD_W + QK_NOPE] = kn16[:, hd * QK_NOPE:(hd + 1) * QK_NOPE]
        k_ref[rows, hd * HEAD_W + QK_NOPE:(hd + 1) * HEAD_W] = kr16


def _mla_latent_body(x_ref, mod_ref, g_ref, b_ref, wdkv_ref, kvg_ref, wdq_ref, qg_ref, wuq_ref,
                     wuk_ref, wuv_ref, wo_ref, wuqs_ref, cos_ref, sin_ref, cckv_ref, ckr_ref,
                     o_ref, q_s, k_s, v_s, kc_s, vc_s):
    c16 = cckv_ref[...].astype(BF16)
    vc_s[...] = _dot(c16, wuv_ref[...]).astype(BF16)
    _store_keys(kc_s, slice(None), _dot(c16, wuk_ref[...]).astype(BF16), ckr_ref[...].astype(BF16))
    for s in range(DEC_SEQ // ATTN_Q_TILE):
        rows = slice(s * ATTN_Q_TILE, (s + 1) * ATTN_Q_TILE)
        cos = cos_ref[rows, :]
        sin = sin_ref[rows, :]
        kv, ckv, qd, q = _mla_project(x_ref[rows, :], mod_ref, wdkv_ref, kvg_ref, wdq_ref, qg_ref, wuq_ref)
        q_swapped = _dot(qd, wuqs_ref[...])
        kr = kv[:, KV_LORA:KV_LORA + LANES] * cos + kv[:, KV_LORA + LANES:KV_LORA + 2 * LANES] * sin
        ckv16 = ckv.astype(BF16)
        v_s[rows, :] = _dot(ckv16, wuv_ref[...]).astype(BF16)
        _store_keys(k_s, rows, _dot(ckv16, wuk_ref[...]).astype(BF16), kr.astype(BF16))
        for hd in range(N_HEADS):
            lo = hd * HEAD_W
            q_s[rows, lo:lo + QK_NOPE] = q[:, lo:lo + QK_NOPE].astype(BF16)
            q_rope = q[:, lo + QK_NOPE:lo + HEAD_W] * cos + q_swapped[:, hd * LANES:(hd + 1) * LANES] * sin
            q_s[rows, lo + QK_NOPE:lo + HEAD_W] = q_rope.astype(BF16)

    def q_block(i, carry):
        rows = pl.ds(pl.multiple_of(i * ATTN_Q_TILE, ATTN_Q_TILE), ATTN_Q_TILE)
        heads = []
        for hd in range(N_HEADS):
            ql = slice(hd * HEAD_W, (hd + 1) * HEAD_W)
            vl = slice(hd * V_DIM, (hd + 1) * V_DIM)
            heads.append(_attend(q_s[rows, ql], [k_s[:, ql], kc_s[:, ql]],
                                 [v_s[:, vl], vc_s[:, vl]]).astype(BF16))
        attn = jnp.concatenate(heads, axis=-1)
        o_ref[rows, :] = _mla_out(x_ref[rows, :], attn, mod_ref, g_ref, b_ref, wo_ref)
        return carry

    lax.fori_loop(0, DEC_SEQ // ATTN_Q_TILE, q_block, 0)


def _mla_mixer(x, mod, ln_g, ln_b, w, cache_ckv, cache_kr, *, latent):
    tile = MLA_TILE
    first = N_PROMPT // tile if latent else 0
    n_rows = DEC_BATCH * DEC_SEQ if latent else N_PROMPT
    dkv = w["dkv"] if latent else w["dkv"][:, :KV_LORA + LANES]
    in_specs = [
        pl.BlockSpec((tile, D_MODEL), lambda t: (first + t, 0)),
        pl.BlockSpec((None, N_MOD, D_MODEL), lambda t: (_group_of_tile(first + t, tile), 0, 0)),
        _const_spec((1, D_MODEL)),
        _const_spec((1, D_MODEL)),
        _const_spec(dkv.shape),
        _const_spec((1, KV_LORA)),
        _const_spec((D_MODEL, Q_LORA)),
        _const_spec((1, Q_LORA)),
        _const_spec((Q_LORA, N_HEADS * HEAD_W)),
        _const_spec((KV_LORA, N_HEADS * QK_NOPE)),
        _const_spec((KV_LORA, N_HEADS * V_DIM)),
        _const_spec((N_HEADS * V_DIM, D_MODEL)),
    ]
    args = [x, mod, ln_g.reshape(1, D_MODEL), ln_b.reshape(1, D_MODEL), dkv, w["kv_g"], w["dq"],
            w["q_g"], w["uq"], w["uk"], w["uv"], w["o"]]
    out_shape = [jax.ShapeDtypeStruct((n_rows, D_MODEL), F32)]
    out_specs = [pl.BlockSpec((tile, D_MODEL), lambda t: (t, 0))]
    scratch = []
    if latent:
        assert tile == DEC_SEQ
        in_specs += [
            _const_spec((Q_LORA, N_HEADS * LANES)),
            _const_spec((DEC_SEQ, LANES)),
            _const_spec((DEC_SEQ, LANES)),
            pl.BlockSpec((None, PAST_LEN, KV_LORA), lambda t: (t, 0, 0)),
            pl.BlockSpec((None, PAST_LEN, LANES), lambda t: (t, 0, 0)),
        ]
        args += [w["uq_swapped"], w["cos"], w["sin"], cache_ckv, cache_kr]
        scratch = [
            pltpu.VMEM((DEC_SEQ, N_HEADS * HEAD_W), BF16),
            pltpu.VMEM((DEC_SEQ, N_HEADS * HEAD_W), BF16),
            pltpu.VMEM((DEC_SEQ, N_HEADS * V_DIM), BF16),
            pltpu.VMEM((PAST_LEN, N_HEADS * HEAD_W), BF16),
            pltpu.VMEM((PAST_LEN, N_HEADS * V_DIM), BF16),
        ]
    else:
        out_shape += [jax.ShapeDtypeStruct((n_rows, KV_LORA), F32),
                      jax.ShapeDtypeStruct((n_rows, QK_ROPE), F32)]
        out_specs += [pl.BlockSpec((tile, KV_LORA), lambda t: (t, 0)),
                      pl.BlockSpec((tile, QK_ROPE), lambda t: (t, 0))]
    outs = pl.pallas_call(
        _mla_latent_body if latent else _mla_prompt_body,
        out_shape=out_shape,
        grid=(n_rows // tile,),
        in_specs=in_specs,
        out_specs=out_specs,
        scratch_shapes=scratch,
        compiler_params=_params(),
        name="mla_latent" if latent else "mla_prompt",
    )(*args)
    return outs[0] if latent else outs


def _rope_swap_perm():
    idx = np.arange(QK_ROPE)
    return np.where((idx % 32) < ROPE_AXIS_PAIRS, idx + ROPE_AXIS_PAIRS, idx - ROPE_AXIS_PAIRS)


def _rope_tables():
    n = DEC_SEQ
    row = np.repeat(np.arange(n // GRID_W), GRID_W)
    col = np.tile(np.arange(GRID_W), n // GRID_W)
    inv = ROPE_BASE ** (-np.arange(ROPE_AXIS_PAIRS, dtype=np.float64) / ROPE_AXIS_PAIRS)
    ar = row[:, None] * inv
    ac = col[:, None] * inv
    zeros = np.zeros((n, LANES - QK_ROPE))
    cos = np.concatenate([np.cos(ar), np.cos(ar), np.cos(ac), np.cos(ac), zeros], axis=-1)
    sin = np.concatenate([-np.sin(ar), np.sin(ar), -np.sin(ac), np.sin(ac), zeros], axis=-1)
    return jnp.asarray(cos, F32), jnp.asarray(sin, F32)


def _mla_weights(dq16, q_norm_g, w_uq, w_dkv, kv_norm_g, w_ukv, o16):
    swap = _rope_swap_perm()
    z64 = jnp.zeros((D_MODEL, LANES - QK_ROPE), F32)
    kr_cols = w_dkv[:, KV_LORA:]
    dkv = jnp.concatenate([w_dkv[:, :KV_LORA], kr_cols, z64, kr_cols[:, swap], z64], axis=-1)
    uq = w_uq.reshape(Q_LORA, N_HEADS, QK_NOPE + QK_ROPE)
    zq = jnp.zeros((Q_LORA, N_HEADS, LANES - QK_ROPE), F32)
    uq_main = jnp.concatenate([uq, zq], axis=-1).reshape(Q_LORA, N_HEADS * HEAD_W)
    uq_swapped = jnp.concatenate([uq[:, :, QK_NOPE:][:, :, swap], zq], axis=-1).reshape(Q_LORA, N_HEADS * LANES)
    ukv = w_ukv.reshape(KV_LORA, N_HEADS, QK_NOPE + V_DIM)
    cos, sin = _rope_tables()
    return {
        "dkv": dkv.astype(BF16),
        "kv_g": kv_norm_g.reshape(1, KV_LORA),
        "dq": dq16,
        "q_g": q_norm_g.reshape(1, Q_LORA),
        "uq": uq_main.astype(BF16),
        "uq_swapped": uq_swapped.astype(BF16),
        "uk": ukv[:, :, :QK_NOPE].reshape(KV_LORA, N_HEADS * QK_NOPE).astype(BF16),
        "uv": ukv[:, :, QK_NOPE:].reshape(KV_LORA, N_HEADS * V_DIM).astype(BF16),
        "o": o16,
        "cos": cos,
        "sin": sin,
    }


def kernel(x_prompt, x_sample, cache_mla_ckv, cache_mla_krope, c, c_ctx, w_ada, b_ada, ln_g, ln_b, ffn_w1, ffn_w3, ffn_w2, cp_w_in, conv_w, conv_b, conv_norm_g, conv_norm_b, pool_w, pool_scale, cp_w_out, mla_w_dq, mla_q_norm_g, mla_w_uq, mla_w_dkv, mla_kv_norm_g, mla_w_ukv, mla_w_o):
    c_all = jnp.concatenate(
        [c_ctx[None, :], c, jnp.zeros((N_GROUPS - 1 - DEC_BATCH, D_MODEL), F32)], axis=0)
    ffn_f32 = (ffn_w1, ffn_w3, ffn_w2)
    n_even, n_odd = cp_w_in.shape[0], mla_w_o.shape[0]
    cast_mats = [(wf, (0, 0)) for wf in ffn_f32]
    cast_mats += [(wm, (jj,)) for jj in range(n_even) for wm in (cp_w_in, cp_w_out)]
    cast_mats += [(wm, (jj,)) for jj in range(n_odd) for wm in (mla_w_dq, mla_w_o)]
    ada0, casts = _ada(c_all, w_ada, b_ada, 0, cast_mats)
    w_bf16, casts = casts[:3], casts[3:]
    cp_bf16, mla_bf16 = casts[:2 * n_even], casts[2 * n_even:]
    ada = {0: ada0}

    x = (x_prompt.reshape(N_PROMPT, D_MODEL), x_sample.reshape(DEC_BATCH * DEC_SEQ, D_MODEL))
    new_ckv, new_krope = [], []
    for i in range(DEPTH):
        mod = ada[i].reshape(N_GROUPS, N_MOD, D_MODEL)
        j = i // 2
        x, w_bf16 = _ffn(x, mod, ln_g[i, 0], ln_b[i, 0], *w_bf16, ffn_f32, (i, 1), k0=0)
        if i % 2 == 0:
            x, ada_next = _conv_pool_mixer(
                x, mod, ln_g[i, 1], ln_b[i, 1], cp_bf16[2 * j], conv_w[j], conv_b[j],
                conv_norm_g[j], conv_norm_b[j], pool_w[j].astype(BF16), pool_scale[j],
                cp_bf16[2 * j + 1], c_all, w_ada, b_ada, min(i + 1, DEPTH - 1))
            if i + 1 < DEPTH:
                ada[i + 1] = ada_next
        else:
            w = _mla_weights(mla_bf16[2 * j], mla_q_norm_g[j], mla_w_uq[j], mla_w_dkv[j],
                             mla_kv_norm_g[j], mla_w_ukv[j], mla_bf16[2 * j + 1])
            attn_args = (x, mod, ln_g[i, 1], ln_b[i, 1], w)
            y_p, ckv_p, kr_p = _mla_mixer(*attn_args, None, None, latent=False)
            new_ckv.append(ckv_p.reshape(BATCH, SEQ, KV_LORA))
            new_krope.append(kr_p.reshape(BATCH, SEQ, QK_ROPE))
            kr_cache = jnp.pad(cache_mla_krope[:, j], ((0, 0), (0, 0), (0, LANES - QK_ROPE)))
            x = (y_p, _mla_mixer(*attn_args, cache_mla_ckv[:, j], kr_cache, latent=True))
        last = i == DEPTH - 1
        x, w_bf16 = _ffn(x, mod, ln_g[i, 2], ln_b[i, 2], *w_bf16,
                         None if last else ffn_f32, None if last else (i + 1, 0),
                         k0=6, split_out=last)

    y_prompt = x[0].reshape(BATCH, SEQ, D_MODEL)
    y_sample = x[1].reshape(DEC_BATCH, DEC_SEQ, D_MODEL)
    return (y_prompt, y_sample, jnp.stack(new_ckv, axis=1), jnp.stack(new_krope, axis=1))
```

```python
import functools
import math

import jax
import jax.numpy as jnp
import numpy as np
from jax import lax
from jax.experimental import pallas as pl
from jax.experimental.pallas import tpu as pltpu

F32 = jnp.float32
BF16 = jnp.bfloat16

D_MODEL = 1024
BATCH = 32
SEQ = 256
DEPTH = 2
DEC_BATCH = 4
DEC_SEQ = 1024
PAST_LEN = 512
GRID_W = 64
N_MOD = 9
D_FF = 2816
D_CONV = 512
CONV_WIDTH = 31
D_POOL = 512
POOL_WINDOWS = (2, 4, 8, 16)
POOL_GROUP = 128
N_HEADS = 8
QK_NOPE = 128
QK_ROPE = 64
V_DIM = 128
KV_LORA = 256
Q_LORA = 384
ROPE_AXIS_PAIRS = 16
ROPE_BASE = 10000.0
ALPHA = (2 * DEPTH) ** 0.25
LN_EPS = 1e-5
RMS_EPS = 1e-6

LANES = 128
SUBLANES = 8

N_PROMPT = BATCH * SEQ
N_TOKENS = N_PROMPT + DEC_BATCH * DEC_SEQ
N_GROUPS = 8
HEAD_W = 256
CONV_PAD = 16
CONV_STRIDE = 4
CONV_ROWS = SUBLANES * CONV_STRIDE

VMEM_LIMIT_BYTES = 60 * 1024 * 1024

FFN_TILES = (1024, 512)
FFN_WORK_BYTES = 4 * 1024 * 1024
FFN_SUB = 256
MIX_TILE = 1024
MIX_SEG = 256
SEG_ROWS = MIX_SEG + 2 * CONV_PAD
MLA_TILE = 1024
ATTN_Q_TILE = 256


def _layer_norm(z, g, b):
    mu = jnp.mean(z, axis=-1, keepdims=True)
    zc = z - mu
    var = jnp.mean(zc * zc, axis=-1, keepdims=True)
    return zc * lax.rsqrt(var + LN_EPS) * g + b


def _rms_norm(z, g):
    return z * lax.rsqrt(jnp.mean(z * z, axis=-1, keepdims=True) + RMS_EPS) * g


def _silu(z):
    return z * jax.nn.sigmoid(z)


def _dot(a, b):
    return jnp.dot(a, b, preferred_element_type=F32)


def _dot_nt(a, b):
    return lax.dot_general(a, b, (((1,), (1,)), ((), ())), preferred_element_type=F32)


def _const_spec(shape):
    nd = len(shape)
    return pl.BlockSpec(shape, lambda *_: (0,) * nd, pipeline_mode=pl.Buffered(1))


def _group_of_tile(t, tile):
    n_p = N_PROMPT // tile
    per_seq = DEC_SEQ // tile
    return jnp.where(t < n_p, 0, 1 + (t - n_p) // per_seq)


def _params(n_axes=1):
    return pltpu.CompilerParams(
        dimension_semantics=("arbitrary",) * n_axes,
        vmem_limit_bytes=VMEM_LIMIT_BYTES,
    )


def _ada_body(c_ref, w_ref, b_ref, o_ref):
    s = _silu(c_ref[...]).astype(BF16)
    o_ref[...] = _dot(s, w_ref[...].astype(BF16)) + b_ref[...]


def _slab_rows(n_rows, n_steps):
    packed_rows = 2 * SUBLANES
    n_slabs = max(d for d in range(1, n_steps + 1)
                  if n_rows % d == 0 and (n_rows // d) % packed_rows == 0)
    return n_rows // n_slabs, n_slabs


def _cast_side_job(mats, n_steps):
    in_specs, out_shapes, out_specs = [], [], []
    for wn, lead in mats:
        n_rows, n_cols = wn.shape[-2:]
        rows, n_slabs = _slab_rows(n_rows, n_steps)
        in_specs.append(pl.BlockSpec(
            (None,) * len(lead) + (rows, n_cols),
            lambda t, n_slabs=n_slabs, lead=tuple(lead): lead + (jnp.minimum(t, n_slabs - 1), 0)))
        out_shapes.append(jax.ShapeDtypeStruct((n_rows, n_cols), BF16))
        out_specs.append(pl.BlockSpec(
            (rows, n_cols), lambda t, n_slabs=n_slabs: (jnp.minimum(t, n_slabs - 1), 0)))
    return in_specs, [wn for wn, _ in mats], out_shapes, out_specs


def _cast_slabs(srcs, dsts):
    for src, dst in zip(srcs, dsts):
        dst[...] = src[...].astype(BF16)


def _ada_cast_body(c_ref, w_ref, b_ref, *refs):
    n = (len(refs) - 1) // 2
    _cast_slabs(refs[:n], refs[n + 1:])
    _ada_body(c_ref, w_ref, b_ref, refs[n])


def _ada(c_all, w_ada, b_ada, layer, cast_mats):
    tn = D_MODEL
    c_specs, c_args, c_shapes, c_out_specs = _cast_side_job(cast_mats, N_MOD)
    outs = pl.pallas_call(
        _ada_cast_body,
        out_shape=[jax.ShapeDtypeStruct((N_GROUPS, N_MOD * D_MODEL), F32)] + c_shapes,
        grid=(N_MOD,),
        in_specs=[
            pl.BlockSpec((N_GROUPS, D_MODEL), lambda k: (0, 0)),
            pl.BlockSpec((None, D_MODEL, tn), lambda k: (layer, 0, k)),
            pl.BlockSpec((None, 1, tn), lambda k: (layer, 0, k)),
        ] + c_specs,
        out_specs=[pl.BlockSpec((N_GROUPS, tn), lambda k: (0, k))] + c_out_specs,
        compiler_params=_params(),
        name="ada",
    )(c_all, w_ada, b_ada.reshape(DEPTH, 1, N_MOD * D_MODEL), *c_args)
    return outs[0], tuple(outs[1:])


def _ffn_body(*refs, k0, split_in, split_out, cast_next):
    refs = list(refs)
    n_x = 2 if split_in else 1
    x_refs, refs = refs[:n_x], refs[n_x:]
    (mod_ref, g_ref, b_ref, w1_ref, w3_ref, w2_ref), refs = refs[:6], refs[6:]
    if cast_next:
        next_f32, refs = refs[:3], refs[3:]
    n_o = 2 if split_out else 1
    o_refs, refs = refs[:n_o], refs[n_o:]
    if cast_next:
        next_bf16, refs = refs[:3], refs[3:]
    tile = x_refs[0].shape[0]
    is_prompt = pl.program_id(0) < N_PROMPT // tile
    if cast_next:
        _cast_slabs(next_f32, next_bf16)
    shift = mod_ref[k0:k0 + 1, :]
    scale = mod_ref[k0 + 1:k0 + 2, :]
    gate = mod_ref[k0 + 2:k0 + 3, :]
    res_ref = refs[0] if split_out else o_refs[0]
    for s in range(tile // FFN_SUB):
        rows = slice(s * FFN_SUB, (s + 1) * FFN_SUB)
        if split_in:
            x = jnp.where(is_prompt, x_refs[0][rows, :], x_refs[1][rows, :])
        else:
            x = x_refs[0][rows, :]
        h = (x * (1.0 + scale) + shift).astype(BF16)
        a = (_silu(_dot(h, w1_ref[...])) * _dot(h, w3_ref[...])).astype(BF16)
        y = _dot(a, w2_ref[...])
        res_ref[rows, :] = _layer_norm(ALPHA * x + (0.5 * gate) * y, g_ref[...], b_ref[...])
    if split_out:
        @pl.when(is_prompt)
        def _():
            o_refs[0][...] = res_ref[...]

        @pl.when(jnp.logical_not(is_prompt))
        def _():
            o_refs[1][...] = res_ref[...]


def _ffn_tile(n_in, split_out, next_w):
    for tm in FFN_TILES:
        n_steps = N_TOKENS // tm
        tile_bytes = tm * D_MODEL * 4
        est = 3 * D_MODEL * D_FF * 2
        est += (n_in + (2 if split_out else 1)) * 2 * tile_bytes
        est += FFN_WORK_BYTES + (tile_bytes if split_out else 0)
        if next_w is not None:
            for wn in next_w:
                rows, _ = _slab_rows(wn.shape[2], n_steps)
                est += 2 * rows * wn.shape[3] * (4 + 2)
        if est <= VMEM_LIMIT_BYTES:
            return tm
    raise ValueError("no FFN tile fits in VMEM")


def _ffn(xs, mod, ln_g, ln_b, w1, w3, w2, next_w=None, next_idx=None, *, k0, split_out=False):
    split_in = isinstance(xs, (tuple, list))
    xs = list(xs) if split_in else [xs]
    tm = _ffn_tile(len(xs), split_out, next_w)
    n_steps = N_TOKENS // tm
    n_p = N_PROMPT // tm
    prompt_spec = pl.BlockSpec((tm, D_MODEL), lambda t: (jnp.minimum(t, n_p - 1), 0))
    latent_spec = pl.BlockSpec((tm, D_MODEL), lambda t: (jnp.maximum(t - n_p, 0), 0))
    merged_spec = pl.BlockSpec((tm, D_MODEL), lambda t: (t, 0))
    in_specs = ([prompt_spec, latent_spec] if split_in else [merged_spec]) + [
        pl.BlockSpec((None, N_MOD, D_MODEL), lambda t: (_group_of_tile(t, tm), 0, 0)),
        _const_spec((1, D_MODEL)),
        _const_spec((1, D_MODEL)),
        _const_spec((D_MODEL, D_FF)),
        _const_spec((D_MODEL, D_FF)),
        _const_spec((D_FF, D_MODEL)),
    ]
    args = xs + [mod, ln_g.reshape(1, D_MODEL), ln_b.reshape(1, D_MODEL), w1, w3, w2]
    if split_out:
        out_shape = [jax.ShapeDtypeStruct((N_PROMPT, D_MODEL), F32),
                     jax.ShapeDtypeStruct((N_TOKENS - N_PROMPT, D_MODEL), F32)]
        out_specs = [prompt_spec, latent_spec]
    else:
        out_shape = [jax.ShapeDtypeStruct((N_TOKENS, D_MODEL), F32)]
        out_specs = [merged_spec]
    if next_w is not None:
        c_specs, c_args, c_shapes, c_out_specs = _cast_side_job(
            [(wn, next_idx) for wn in next_w], n_steps)
        in_specs += c_specs
        args += c_args
        out_shape += c_shapes
        out_specs += c_out_specs
    outs = pl.pallas_call(
        functools.partial(_ffn_body, k0=k0, split_in=split_in, split_out=split_out,
                          cast_next=next_w is not None),
        out_shape=out_shape,
        grid=(n_steps,),
        in_specs=in_specs,
        out_specs=out_specs,
        scratch_shapes=[pltpu.VMEM((tm, D_MODEL), F32)] if split_out else [],
        compiler_params=_params(),
        name="ffn",
    )(*args)
    n_tok = 2 if split_out else 1
    tok = tuple(outs[:n_tok]) if split_out else outs[0]
    return tok, tuple(outs[n_tok:])


def _strided_rows(start):
    return pl.ds(start, SUBLANES, stride=CONV_STRIDE)


def _chunk_rows(j):
    chunks_per_seg = MIX_SEG // CONV_ROWS
    r0 = (j % chunks_per_seg) * CONV_ROWS
    return (j // chunks_per_seg) * SEG_ROWS + r0 + CONV_PAD, j * CONV_ROWS, r0


def _conv_chunk(j, *, slab, cw_ref, cb_ref, pad_ref, conv_ref):
    p0, o0, _ = _chunk_rows(j)
    half = CONV_WIDTH // 2
    n_part = 2
    acc = [[None] * n_part for _ in range(CONV_STRIDE)]
    weights = {}
    for off in range(-half, half + CONV_STRIDE):
        tap = pad_ref[slab, _strided_rows(p0 + off), :]
        for rho in range(CONV_STRIDE):
            k = off - rho + half
            if 0 <= k < CONV_WIDTH:
                if k not in weights:
                    weights[k] = cw_ref[slab, pl.ds(k, SUBLANES, stride=0), :]
                term = weights[k] * tap
                part = acc[rho][k % n_part]
                acc[rho][k % n_part] = term if part is None else part + term
    bias = cb_ref[slab, pl.ds(0, SUBLANES, stride=0), :]
    for rho in range(CONV_STRIDE):
        conv_ref[slab, _strided_rows(o0 + rho), :] = (acc[rho][0] + acc[rho][1]) + bias


def _pool_chunk(j, *, pad_ref, pool_ref, is_prompt):
    p0, o0, r0 = _chunk_rows(j)
    at_edge = r0 == 0 or r0 == MIX_SEG - CONV_ROWS
    if at_edge:
        seq_len = jnp.where(is_prompt, SEQ, DEC_SEQ)
        t0 = jnp.where(is_prompt, r0, o0) + CONV_STRIDE * lax.broadcasted_iota(jnp.int32, (SUBLANES, LANES), 0)
    for gi, w in enumerate(POOL_WINDOWS):
        left = w // 2
        right = w - 1 - left
        slab = D_CONV // LANES + gi
        total = [None] * CONV_STRIDE
        for off in range(-left, right + CONV_STRIDE):
            tap = pad_ref[slab, _strided_rows(p0 + off), :]
            for rho in range(CONV_STRIDE):
                if -left <= off - rho <= right:
                    total[rho] = tap if total[rho] is None else total[rho] + tap
        for rho in range(CONV_STRIDE):
            if at_edge:
                t = t0 + rho
                cnt = (jnp.minimum(t + right, seq_len - 1) - jnp.maximum(t - left, 0) + 1).astype(F32)
                mean = total[rho] / cnt
            else:
                mean = total[rho] / float(w)
            tok = pad_ref[slab, _strided_rows(p0 + rho), :]
            pool_ref[gi, _strided_rows(o0 + rho), :] = mean - tok


def _cp_body(x_ref, mod_ref, g_ref, b_ref, win_ref, cw_ref, cb_ref, cng_ref, cnb_ref,
             pw_ref, ps_ref, wout_ref, c_ref, wada_ref, bada_ref, o_ref, ada_ref,
             pad_ref, conv_ref, pool_ref):
    is_prompt = pl.program_id(0) < N_PROMPT // MIX_TILE
    _ada_body(c_ref, wada_ref, bada_ref, ada_ref)
    shift = mod_ref[3:4, :]
    scale = mod_ref[4:5, :]
    gate = mod_ref[5:6, :]
    n_seg = MIX_TILE // MIX_SEG
    n_slab = (D_CONV + D_POOL) // LANES
    feats = []
    for s in range(n_seg):
        x = x_ref[s * MIX_SEG:(s + 1) * MIX_SEG, :]
        h = (x * (1.0 + scale) + shift).astype(BF16)
        proj = _dot(h, win_ref[...])
        glu = proj[:, :D_CONV] * jax.nn.sigmoid(proj[:, D_CONV:2 * D_CONV])
        feats.append(jnp.concatenate([glu, proj[:, 2 * D_CONV:]], axis=-1))
    zeros = jnp.zeros((CONV_PAD, LANES), F32)
    for s in range(n_seg):
        base = s * SEG_ROWS
        for slab in range(n_slab):
            lanes = slice(slab * LANES, (slab + 1) * LANES)
            pad_ref[slab, base:base + CONV_PAD, :] = (
                zeros if s == 0 else jnp.where(is_prompt, zeros, feats[s - 1][MIX_SEG - CONV_PAD:, lanes]))
            pad_ref[slab, base + CONV_PAD:base + CONV_PAD + MIX_SEG, :] = feats[s][:, lanes]
            pad_ref[slab, base + CONV_PAD + MIX_SEG:base + SEG_ROWS, :] = (
                zeros if s == n_seg - 1 else jnp.where(is_prompt, zeros, feats[s + 1][:CONV_PAD, lanes]))
    chunks_per_seg = MIX_SEG // CONV_ROWS
    for s in range(n_seg):
        rows = slice(s * MIX_SEG, (s + 1) * MIX_SEG)
        seg_chunks = range(s * chunks_per_seg, (s + 1) * chunks_per_seg)
        for slab in range(D_CONV // LANES):
            for j in seg_chunks:
                _conv_chunk(j, slab=slab, cw_ref=cw_ref, cb_ref=cb_ref, pad_ref=pad_ref, conv_ref=conv_ref)
        for j in seg_chunks:
            _pool_chunk(j, pad_ref=pad_ref, pool_ref=pool_ref, is_prompt=is_prompt)
        conv = jnp.concatenate([conv_ref[c, rows, :] for c in range(D_CONV // LANES)], axis=-1)
        a = _silu(_layer_norm(conv, cng_ref[...], cnb_ref[...]))
        pooled = [_dot(pool_ref[gi, rows, :].astype(BF16), pw_ref[gi]) for gi in range(len(POOL_WINDOWS))]
        bmix = jnp.concatenate(pooled, axis=-1) * ps_ref[...]
        cat = jnp.concatenate([a, bmix], axis=-1).astype(BF16)
        y = _dot(cat, wout_ref[...])
        o_ref[rows, :] = _layer_norm(ALPHA * x_ref[rows, :] + gate * y, g_ref[...], b_ref[...])


def _conv_pool_mixer(x, mod, ln_g, ln_b, w_in, conv_w, conv_b, cn_g, cn_b, pool_w, pool_scale, w_out,
                     c_all, w_ada, b_ada, next_layer):
    tm = MIX_TILE
    n_steps = N_TOKENS // tm
    ada_cols = N_MOD * D_MODEL // n_steps
    assert tm == DEC_SEQ and MIX_SEG == SEQ and N_PROMPT % tm == 0 and ada_cols % LANES == 0
    n_cslab = D_CONV // LANES
    row_spec = pl.BlockSpec((tm, D_MODEL), lambda t: (t, 0))
    in_specs = [
        row_spec,
        pl.BlockSpec((None, N_MOD, D_MODEL), lambda t: (_group_of_tile(t, tm), 0, 0)),
        _const_spec((1, D_MODEL)),
        _const_spec((1, D_MODEL)),
        _const_spec((D_MODEL, 2 * D_CONV + D_POOL)),
        _const_spec((n_cslab, CONV_WIDTH, LANES)),
        _const_spec((n_cslab, 1, LANES)),
        _const_spec((1, D_CONV)),
        _const_spec((1, D_CONV)),
        _const_spec((len(POOL_WINDOWS), POOL_GROUP, POOL_GROUP)),
        _const_spec((1, D_POOL)),
        _const_spec((D_CONV + D_POOL, D_MODEL)),
        _const_spec((N_GROUPS, D_MODEL)),
        pl.BlockSpec((None, D_MODEL, ada_cols), lambda t: (next_layer, 0, t)),
        pl.BlockSpec((None, 1, ada_cols), lambda t: (next_layer, 0, t)),
    ]
    conv_w_slabs = conv_w.reshape(CONV_WIDTH, n_cslab, LANES).transpose(1, 0, 2)
    args = [x, mod, ln_g.reshape(1, D_MODEL), ln_b.reshape(1, D_MODEL), w_in, conv_w_slabs,
            conv_b.reshape(n_cslab, 1, LANES), cn_g.reshape(1, D_CONV), cn_b.reshape(1, D_CONV),
            pool_w, pool_scale.reshape(1, D_POOL), w_out,
            c_all, w_ada, b_ada.reshape(DEPTH, 1, N_MOD * D_MODEL)]
    return pl.pallas_call(
        _cp_body,
        out_shape=[jax.ShapeDtypeStruct((N_TOKENS, D_MODEL), F32),
                   jax.ShapeDtypeStruct((N_GROUPS, N_MOD * D_MODEL), F32)],
        grid=(n_steps,),
        in_specs=in_specs,
        out_specs=[row_spec, pl.BlockSpec((N_GROUPS, ada_cols), lambda t: (0, t))],
        scratch_shapes=[
            pltpu.VMEM(((D_CONV + D_POOL) // LANES, tm // MIX_SEG * SEG_ROWS, LANES), F32),
            pltpu.VMEM((n_cslab, tm, LANES), F32),
            pltpu.VMEM((D_POOL // LANES, tm, LANES), F32),
        ],
        compiler_params=_params(),
        name="conv_pool",
    )(*args)


def _attend(q, keys, values, keys_transposed=False):
    scores = [_dot(q, k) if keys_transposed else _dot_nt(q, k) for k in keys]
    m = functools.reduce(jnp.maximum, [jnp.max(s, axis=-1, keepdims=True) for s in scores])
    c = (QK_NOPE + QK_ROPE) ** -0.5 * math.log2(math.e)
    e = [jnp.exp2((s - m) * c) for s in scores]
    denom = functools.reduce(jnp.add, [jnp.sum(ei, axis=-1, keepdims=True) for ei in e])
    out = functools.reduce(jnp.add, [_dot(ei.astype(BF16), v) for ei, v in zip(e, values)])
    return out * (1.0 / denom)


def _mla_project(x, mod_ref, wdkv_ref, kvg_ref, wdq_ref, qg_ref, wuq_ref):
    h = (x * (1.0 + mod_ref[4:5, :]) + mod_ref[3:4, :]).astype(BF16)
    kv = _dot(h, wdkv_ref[...])
    ckv = _rms_norm(kv[:, :KV_LORA], kvg_ref[...])
    qd = _rms_norm(_dot(h, wdq_ref[...]), qg_ref[...]).astype(BF16)
    return kv, ckv, qd, _dot(qd, wuq_ref[...])


def _mla_out(x, attn, mod_ref, g_ref, b_ref, wo_ref):
    y = _dot(attn, wo_ref[...])
    return _layer_norm(ALPHA * x + mod_ref[5:6, :] * y, g_ref[...], b_ref[...])


def _mla_prompt_body(x_ref, mod_ref, g_ref, b_ref, wdkv_ref, kvg_ref, wdq_ref, qg_ref, wuq_ref,
                     wuk_ref, wuv_ref, wo_ref, o_ref, ckv_ref, kr_ref):
    for s in range(x_ref.shape[0] // SEQ):
        rows = slice(s * SEQ, (s + 1) * SEQ)
        x = x_ref[rows, :]
        kv, ckv, _, q = _mla_project(x, mod_ref, wdkv_ref, kvg_ref, wdq_ref, qg_ref, wuq_ref)
        ckv_ref[rows, :] = ckv
        kr_ref[rows, :] = kv[:, KV_LORA:KV_LORA + QK_ROPE]
        q16 = q.astype(BF16)
        ckv16 = ckv.astype(BF16)
        kn16 = _dot(ckv16, wuk_ref[...]).astype(BF16)
        v16 = _dot(ckv16, wuv_ref[...]).astype(BF16)
        kr16 = kv[:, KV_LORA:KV_LORA + LANES].astype(BF16)
        heads = []
        for hd in range(N_HEADS):
            vl = slice(hd * V_DIM, (hd + 1) * V_DIM)
            k_h = jnp.concatenate([kn16[:, hd * QK_NOPE:(hd + 1) * QK_NOPE], kr16], axis=-1)
            heads.append(_attend(q16[:, hd * HEAD_W:(hd + 1) * HEAD_W], [k_h], [v16[:, vl]]).astype(BF16))
        attn = jnp.concatenate(heads, axis=-1)
        o_ref[rows, :] = _mla_out(x, attn, mod_ref, g_ref, b_ref, wo_ref)


def _store_keys_t(kt_ref, cols, kn, kr):
    kn_t = kn.T.astype(BF16)
    kr_t = kr.T.astype(BF16)
    for hd in range(N_HEADS):
        kt_ref[hd * HEAD_W:hd * HEAD_W + QK_NOPE, cols] = kn_t[hd * QK_NOPE:(hd + 1) * QK_NOPE, :]
        kt_ref[hd * HEAD_W + QK_NOPE:(hd + 1) * HEAD_W, cols] = kr_t


def _mla_latent_body(x_ref, mod_ref, g_ref, b_ref, wdkv_ref, kvg_ref, wdq_ref, qg_ref, wuq_ref,
                     wuk_ref, wuv_ref, wo_ref, wuqs_ref, cos_ref, sin_ref, cckv_ref, ckr_ref,
                     o_ref, q_s, kt_s, v_s):
    c16 = cckv_ref[...].astype(BF16)
    cached = slice(DEC_SEQ, DEC_SEQ + PAST_LEN)
    v_s[cached, :] = _dot(c16, wuv_ref[...]).astype(BF16)
    _store_keys_t(kt_s, cached, _dot(c16, wuk_ref[...]), ckr_ref[...])
    for s in range(DEC_SEQ // ATTN_Q_TILE):
        rows = slice(s * ATTN_Q_TILE, (s + 1) * ATTN_Q_TILE)
        cos = cos_ref[rows, :]
        sin = sin_ref[rows, :]
        kv, ckv, qd, q = _mla_project(x_ref[rows, :], mod_ref, wdkv_ref, kvg_ref, wdq_ref, qg_ref, wuq_ref)
        q_swapped = _dot(qd, wuqs_ref[...])
        kr = kv[:, KV_LORA:KV_LORA + LANES] * cos + kv[:, KV_LORA + LANES:KV_LORA + 2 * LANES] * sin
        ckv16 = ckv.astype(BF16)
        v_s[rows, :] = _dot(ckv16, wuv_ref[...]).astype(BF16)
        _store_keys_t(kt_s, rows, _dot(ckv16, wuk_ref[...]), kr)
        for hd in range(N_HEADS):
            lo = hd * HEAD_W
            q_s[rows, lo:lo + QK_NOPE] = q[:, lo:lo + QK_NOPE].astype(BF16)
            q_rope = q[:, lo + QK_NOPE:lo + HEAD_W] * cos + q_swapped[:, hd * LANES:(hd + 1) * LANES] * sin
            q_s[rows, lo + QK_NOPE:lo + HEAD_W] = q_rope.astype(BF16)

    def q_block(i, carry):
        rows = pl.ds(pl.multiple_of(i * ATTN_Q_TILE, ATTN_Q_TILE), ATTN_Q_TILE)
        heads = []
        for hd in range(N_HEADS):
            ql = slice(hd * HEAD_W, (hd + 1) * HEAD_W)
            vl = slice(hd * V_DIM, (hd + 1) * V_DIM)
            heads.append(_attend(q_s[rows, ql], [kt_s[ql, :]], [v_s[:, vl]],
                                 keys_transposed=True).astype(BF16))
        attn = jnp.concatenate(heads, axis=-1)
        o_ref[rows, :] = _mla_out(x_ref[rows, :], attn, mod_ref, g_ref, b_ref, wo_ref)
        return carry

    lax.fori_loop(0, DEC_SEQ // ATTN_Q_TILE, q_block, 0)


def _mla_mixer(x, mod, ln_g, ln_b, w, cache_ckv, cache_kr, *, latent):
    tile = MLA_TILE
    first = N_PROMPT // tile if latent else 0
    n_rows = DEC_BATCH * DEC_SEQ if latent else N_PROMPT
    dkv = w["dkv"] if latent else w["dkv"][:, :KV_LORA + LANES]
    in_specs = [
        pl.BlockSpec((tile, D_MODEL), lambda t: (first + t, 0)),
        pl.BlockSpec((None, N_MOD, D_MODEL), lambda t: (_group_of_tile(first + t, tile), 0, 0)),
        _const_spec((1, D_MODEL)),
        _const_spec((1, D_MODEL)),
        _const_spec(dkv.shape),
        _const_spec((1, KV_LORA)),
        _const_spec((D_MODEL, Q_LORA)),
        _const_spec((1, Q_LORA)),
        _const_spec((Q_LORA, N_HEADS * HEAD_W)),
        _const_spec((KV_LORA, N_HEADS * QK_NOPE)),
        _const_spec((KV_LORA, N_HEADS * V_DIM)),
        _const_spec((N_HEADS * V_DIM, D_MODEL)),
    ]
    args = [x, mod, ln_g.reshape(1, D_MODEL), ln_b.reshape(1, D_MODEL), dkv, w["kv_g"], w["dq"],
            w["q_g"], w["uq"], w["uk"], w["uv"], w["o"]]
    out_shape = [jax.ShapeDtypeStruct((n_rows, D_MODEL), F32)]
    out_specs = [pl.BlockSpec((tile, D_MODEL), lambda t: (t, 0))]
    scratch = []
    if latent:
        assert tile == DEC_SEQ
        in_specs += [
            _const_spec((Q_LORA, N_HEADS * LANES)),
            _const_spec((DEC_SEQ, LANES)),
            _const_spec((DEC_SEQ, LANES)),
            pl.BlockSpec((None, PAST_LEN, KV_LORA), lambda t: (t, 0, 0)),
            pl.BlockSpec((None, PAST_LEN, LANES), lambda t: (t, 0, 0)),
        ]
        args += [w["uq_swapped"], w["cos"], w["sin"], cache_ckv, cache_kr]
        n_keys = DEC_SEQ + PAST_LEN
        scratch = [
            pltpu.VMEM((DEC_SEQ, N_HEADS * HEAD_W), BF16),
            pltpu.VMEM((N_HEADS * HEAD_W, n_keys), BF16),
            pltpu.VMEM((n_keys, N_HEADS * V_DIM), BF16),
        ]
    else:
        out_shape += [jax.ShapeDtypeStruct((n_rows, KV_LORA), F32),
                      jax.ShapeDtypeStruct((n_rows, QK_ROPE), F32)]
        out_specs += [pl.BlockSpec((tile, KV_LORA), lambda t: (t, 0)),
                      pl.BlockSpec((tile, QK_ROPE), lambda t: (t, 0))]
    outs = pl.pallas_call(
        _mla_latent_body if latent else _mla_prompt_body,
        out_shape=out_shape,
        grid=(n_rows // tile,),
        in_specs=in_specs,
        out_specs=out_specs,
        scratch_shapes=scratch,
        compiler_params=_params(),
        name="mla_latent" if latent else "mla_prompt",
    )(*args)
    return outs[0] if latent else outs


def _rope_swap_perm():
    idx = np.arange(QK_ROPE)
    return np.where((idx % 32) < ROPE_AXIS_PAIRS, idx + ROPE_AXIS_PAIRS, idx - ROPE_AXIS_PAIRS)


def _rope_tables():
    n = DEC_SEQ
    row = np.repeat(np.arange(n // GRID_W), GRID_W)
    col = np.tile(np.arange(GRID_W), n // GRID_W)
    inv = ROPE_BASE ** (-np.arange(ROPE_AXIS_PAIRS, dtype=np.float64) / ROPE_AXIS_PAIRS)
    ar = row[:, None] * inv
    ac = col[:, None] * inv
    zeros = np.zeros((n, LANES - QK_ROPE))
    cos = np.concatenate([np.cos(ar), np.cos(ar), np.cos(ac), np.cos(ac), zeros], axis=-1)
    sin = np.concatenate([-np.sin(ar), np.sin(ar), -np.sin(ac), np.sin(ac), zeros], axis=-1)
    return jnp.asarray(cos, F32), jnp.asarray(sin, F32)


def _mla_weights(dq16, q_norm_g, w_uq, w_dkv, kv_norm_g, w_ukv, o16):
    swap = _rope_swap_perm()
    z64 = jnp.zeros((D_MODEL, LANES - QK_ROPE), F32)
    kr_cols = w_dkv[:, KV_LORA:]
    dkv = jnp.concatenate([w_dkv[:, :KV_LORA], kr_cols, z64, kr_cols[:, swap], z64], axis=-1)
    uq = w_uq.reshape(Q_LORA, N_HEADS, QK_NOPE + QK_ROPE)
    zq = jnp.zeros((Q_LORA, N_HEADS, LANES - QK_ROPE), F32)
    uq_main = jnp.concatenate([uq, zq], axis=-1).reshape(Q_LORA, N_HEADS * HEAD_W)
    uq_swapped = jnp.concatenate([uq[:, :, QK_NOPE:][:, :, swap], zq], axis=-1).reshape(Q_LORA, N_HEADS * LANES)
    ukv = w_ukv.reshape(KV_LORA, N_HEADS, QK_NOPE + V_DIM)
    cos, sin = _rope_tables()
    return {
        "dkv": dkv.astype(BF16),
        "kv_g": kv_norm_g.reshape(1, KV_LORA),
        "dq": dq16,
        "q_g": q_norm_g.reshape(1, Q_LORA),
        "uq": uq_main.astype(BF16),
        "uq_swapped": uq_swapped.astype(BF16),
        "uk": ukv[:, :, :QK_NOPE].reshape(KV_LORA, N_HEADS * QK_NOPE).astype(BF16),
        "uv": ukv[:, :, QK_NOPE:].reshape(KV_LORA, N_HEADS * V_DIM).astype(BF16),
        "o": o16,
        "cos": cos,
        "sin": sin,
    }


def kernel(x_prompt, x_sample, cache_mla_ckv, cache_mla_krope, c, c_ctx, w_ada, b_ada, ln_g, ln_b, ffn_w1, ffn_w3, ffn_w2, cp_w_in, conv_w, conv_b, conv_norm_g, conv_norm_b, pool_w, pool_scale, cp_w_out, mla_w_dq, mla_q_norm_g, mla_w_uq, mla_w_dkv, mla_kv_norm_g, mla_w_ukv, mla_w_o):
    c_all = jnp.concatenate(
        [c_ctx[None, :], c, jnp.zeros((N_GROUPS - 1 - DEC_BATCH, D_MODEL), F32)], axis=0)
    ffn_f32 = (ffn_w1, ffn_w3, ffn_w2)
    n_even, n_odd = cp_w_in.shape[0], mla_w_o.shape[0]
    cast_mats = [(wf, (0, 0)) for wf in ffn_f32]
    cast_mats += [(wm, (jj,)) for jj in range(n_even) for wm in (cp_w_in, cp_w_out)]
    cast_mats += [(wm, (jj,)) for jj in range(n_odd) for wm in (mla_w_dq, mla_w_o)]
    ada0, casts = _ada(c_all, w_ada, b_ada, 0, cast_mats)
    w_bf16, casts = casts[:3], casts[3:]
    cp_bf16, mla_bf16 = casts[:2 * n_even], casts[2 * n_even:]
    ada = {0: ada0}

    x = (x_prompt.reshape(N_PROMPT, D_MODEL), x_sample.reshape(DEC_BATCH * DEC_SEQ, D_MODEL))
    new_ckv, new_krope = [], []
    for i in range(DEPTH):
        mod = ada[i].reshape(N_GROUPS, N_MOD, D_MODEL)
        j = i // 2
        x, w_bf16 = _ffn(x, mod, ln_g[i, 0], ln_b[i, 0], *w_bf16, ffn_f32, (i, 1), k0=0)
        if i % 2 == 0:
            x, ada_next = _conv_pool_mixer(
                x, mod, ln_g[i, 1], ln_b[i, 1], cp_bf16[2 * j], conv_w[j], conv_b[j],
                conv_norm_g[j], conv_norm_b[j], pool_w[j].astype(BF16), pool_scale[j],
                cp_bf16[2 * j + 1], c_all, w_ada, b_ada, min(i + 1, DEPTH - 1))
            if i + 1 < DEPTH:
                ada[i + 1] = ada_next
        else:
            w = _mla_weights(mla_bf16[2 * j], mla_q_norm_g[j], mla_w_uq[j], mla_w_dkv[j],
                             mla_kv_norm_g[j], mla_w_ukv[j], mla_bf16[2 * j + 1])
            attn_args = (x, mod, ln_g[i, 1], ln_b[i, 1], w)
            y_p, ckv_p, kr_p = _mla_mixer(*attn_args, None, None, latent=False)
            new_ckv.append(ckv_p.reshape(BATCH, SEQ, KV_LORA))
            new_krope.append(kr_p.reshape(BATCH, SEQ, QK_ROPE))
            kr_cache = jnp.pad(cache_mla_krope[:, j], ((0, 0), (0, 0), (0, LANES - QK_ROPE)))
            x = (y_p, _mla_mixer(*attn_args, cache_mla_ckv[:, j], kr_cache, latent=True))
        last = i == DEPTH - 1
        x, w_bf16 = _ffn(x, mod, ln_g[i, 2], ln_b[i, 2], *w_bf16,
                         None if last else ffn_f32, None if last else (i + 1, 0),
                         k0=6, split_out=last)

    y_prompt = x[0].reshape(BATCH, SEQ, D_MODEL)
    y_sample = x[1].reshape(DEC_BATCH, DEC_SEQ, D_MODEL)
    return (y_prompt, y_sample, jnp.stack(new_ckv, axis=1), jnp.stack(new_krope, axis=1))
```
